```python
import math
import jax
import jax.numpy as jnp
from jax import lax
import numpy as np

D_MODEL = 1024
BATCH = 16
SEQ = 2048
DEPTH = 4

GRID_W = 64
CTX_LEN = 256
N_MIXERS = 2
N_MOD = 6
EPS = 1e-6

GDN_HK = 8
GDN_HV = 16
GDN_DK = 128
GDN_DV = 128
GDN_QK_W = GDN_HK * GDN_DK
GDN_V_W = GDN_HV * GDN_DV
GDN_QKV_W = 2 * GDN_QK_W + GDN_V_W
GDN_IN_W = GDN_QKV_W + GDN_V_W + 4 * GDN_HV
CONV_W = 5
CHUNK = 64

ATT_HQ = 8
ATT_HKV = 2
ATT_G = ATT_HQ // ATT_HKV
ATT_DH = 128
ATT_IN_W = (ATT_HQ + 2 * ATT_HKV) * ATT_DH
Q_BLOCK = 128
ROPE_THETA = 10000.0

D_FF = ((8 * D_MODEL + 3 * 256 - 1) // (3 * 256)) * 256

ALPHA = (2 * DEPTH) ** 0.25
BETA_INIT = (8 * DEPTH) ** -0.25
N_GDN = (DEPTH + N_MIXERS - 1) // N_MIXERS
N_ATT = DEPTH // N_MIXERS

kernel_name = "hybrid_gdn_gqa_dit_trunk"


def layer_norm(x, g, b):
    xf = x.astype(jnp.float32)
    mu = jnp.mean(xf, axis=-1, keepdims=True)
    var = jnp.mean(jnp.square(xf - mu), axis=-1, keepdims=True)
    return ((xf - mu) * lax.rsqrt(var + EPS) * g + b).astype(x.dtype)


def rms_norm(x, g):
    xf = x.astype(jnp.float32)
    return xf * lax.rsqrt(jnp.mean(jnp.square(xf), axis=-1, keepdims=True) + EPS) * g


def l2_norm(x):
    return x * lax.rsqrt(jnp.sum(jnp.square(x), axis=-1, keepdims=True) + EPS)


def modulation(cond, w, b):
    m = jax.nn.silu(cond) @ w + b
    return m.reshape(m.shape[:-1] + (N_MOD, D_MODEL))


def modulate(x, shift, scale):
    return x * (1.0 + scale) + shift


def post_norm(x, delta, gate, g, b):
    return layer_norm(ALPHA * x + gate * delta, g, b)


def rope_tables(n_tokens):
    rows = n_tokens // GRID_W
    row = jnp.repeat(jnp.arange(rows), GRID_W).astype(jnp.float32)
    col = jnp.tile(jnp.arange(GRID_W), rows).astype(jnp.float32)
    n_freq = ATT_DH // 4
    freqs = ROPE_THETA ** (-jnp.arange(n_freq, dtype=jnp.float32) / n_freq)
    ang_r = row[:, None] * freqs
    ang_c = col[:, None] * freqs
    ang = jnp.concatenate([ang_r, ang_r, ang_c, ang_c], axis=-1)
    return jnp.cos(ang), jnp.sin(ang)


def rope_2d(x, cos, sin):
    xs = x.reshape(x.shape[:-1] + (2, 2, ATT_DH // 4))
    x1, x2 = xs[..., 0, :], xs[..., 1, :]
    rot = jnp.stack([-x2, x1], axis=-2).reshape(x.shape)
    return x * cos + rot * sin


def short_conv(x, w):
    pad = CONV_W // 2
    n_tok = x.shape[1]
    xp = jnp.pad(x, ((0, 0), (pad, pad), (0, 0)))
    y = xp[:, 0:n_tok] * w[0]
    for j in range(1, CONV_W):
        y = y + xp[:, j:j + n_tok] * w[j]
    return y


def chunk_gated_delta(q, k, v, g, beta, s0):
    f32 = jnp.float32
    B, T, H, DK = q.shape
    n = T // CHUNK

    def chunks(a):
        a = a.astype(f32).reshape((B, n, CHUNK, H) + a.shape[3:])
        return jnp.moveaxis(a, (1, 3), (0, 2))

    qc = chunks(q) * (DK ** -0.5)
    kc = chunks(k)
    vc = chunks(v)
    bc = chunks(beta)
    gcum = jnp.cumsum(chunks(g), axis=-1)
    incl = jnp.tril(jnp.ones((CHUNK, CHUNK), bool))
    strict = jnp.tril(jnp.ones((CHUNK, CHUNK), bool), -1)
    diff = gcum[..., :, None] - gcum[..., None, :]
    decay = jnp.where(incl, jnp.exp(jnp.where(incl, diff, 0.0)), 0.0)
    kb = kc * bc[..., None]
    a_mat = jnp.where(strict, jnp.einsum('nbhik,nbhjk->nbhij', kb, kc) * decay, 0.0)
    eye = jnp.eye(CHUNK, dtype=f32)
    t_inv = lax.linalg.triangular_solve(eye + a_mat, jnp.broadcast_to(eye, a_mat.shape),
                                        left_side=True, lower=True)
    u = jnp.einsum('nbhij,nbhjv->nbhiv', t_inv, vc * bc[..., None])
    w = jnp.einsum('nbhij,nbhjk->nbhik', t_inv, kb * jnp.exp(gcum)[..., None])
    qk = jnp.einsum('nbhik,nbhjk->nbhij', qc, kc) * decay

    def step(s, inp):
        q_i, k_i, u_i, w_i, qk_i, g_i = inp
        v_new = u_i - jnp.einsum('bhck,bhkv->bhcv', w_i, s)
        o_i = (jnp.einsum('bhck,bhkv->bhcv', q_i * jnp.exp(g_i)[..., None], s)
               + jnp.einsum('bhij,bhjv->bhiv', qk_i, v_new))
        g_last = g_i[..., -1:]
        s = (s * jnp.exp(g_last)[..., None]
             + jnp.einsum('bhck,bhcv->bhkv', k_i * jnp.exp(g_last - g_i)[..., None], v_new))
        return s, o_i

    s_fin, o = lax.scan(step, s0.astype(f32), (qc, kc, u, w, qk, gcum))
    o = jnp.moveaxis(o, (0, 2), (1, 3)).reshape(B, T, H, -1)
    return o, s_fin


def gdn_direction(q, k, v, g, beta, s0, reverse):
    if reverse:
        q, k, v, g, beta = (jnp.flip(a, axis=1) for a in (q, k, v, g, beta))
    o, s = chunk_gated_delta(q, k, v, g, beta, s0)
    if reverse:
        o = jnp.flip(o, axis=1)
    return o, s


def gdn_project(h, w_in, conv_w, a_log, dt_bias):
    B, T, _ = h.shape
    p = (h @ w_in).astype(jnp.float32)
    qkv = jax.nn.silu(short_conv(p[..., :GDN_QKV_W], conv_w.astype(jnp.float32)))
    z = p[..., GDN_QKV_W:GDN_QKV_W + GDN_V_W]
    off = GDN_QKV_W + GDN_V_W
    b_raw = p[..., off:off + 2 * GDN_HV].reshape(B, T, 2, GDN_HV)
    a_raw = p[..., off + 2 * GDN_HV:].reshape(B, T, 2, GDN_HV)
    rep = GDN_HV // GDN_HK
    q = jnp.repeat(l2_norm(qkv[..., :GDN_QK_W].reshape(B, T, GDN_HK, GDN_DK)), rep, axis=2)
    k = jnp.repeat(l2_norm(qkv[..., GDN_QK_W:2 * GDN_QK_W].reshape(B, T, GDN_HK, GDN_DK)), rep, axis=2)
    v = qkv[..., 2 * GDN_QK_W:].reshape(B, T, GDN_HV, GDN_DV)
    beta = jax.nn.sigmoid(b_raw)
    g = -jnp.exp(a_log.astype(jnp.float32)) * jax.nn.softplus(a_raw + dt_bias.astype(jnp.float32))
    return q, k, v, z, beta, g


def gdn_output(o, z, norm_g, w_out, dtype):
    B, T = o.shape[:2]
    y = rms_norm(o, norm_g) * jax.nn.silu(z.reshape(B, T, GDN_HV, GDN_DV))
    return y.reshape(B, T, GDN_V_W).astype(dtype) @ w_out


def gdn_mixer(h_lat, h_ctx, w_in, conv_w, a_log, dt_bias, norm_g, w_out, with_ctx_out):
    ql, kl, vl, zl, bl, gl = gdn_project(h_lat, w_in, conv_w, a_log, dt_bias)
    qc, kc, vc, zc, bc, gc = gdn_project(h_ctx, w_in, conv_w, a_log, dt_bias)
    s_zero = jnp.zeros((h_lat.shape[0], GDN_HV, GDN_DK, GDN_DV), jnp.float32)
    o_lat = []
    o_ctx = []
    for d, reverse in enumerate((False, True)):
        oc, s_ctx = gdn_direction(qc, kc, vc, gc[:, :, d], bc[:, :, d], s_zero, reverse)
        ol, _ = gdn_direction(ql, kl, vl, gl[:, :, d], bl[:, :, d], s_ctx, reverse)
        o_lat.append(ol)
        o_ctx.append(oc)
    out_lat = gdn_output(o_lat[0] + o_lat[1], zl, norm_g, w_out, h_lat.dtype)
    out_ctx = gdn_output(o_ctx[0] + o_ctx[1], zc, norm_g, w_out, h_ctx.dtype) if with_ctx_out else None
    return out_lat, out_ctx


def attn_project(h, w_qkv, q_norm, k_norm):
    B, T, _ = h.shape
    p = h @ w_qkv
    nq = ATT_HQ * ATT_DH
    nk = ATT_HKV * ATT_DH
    q = rms_norm(p[..., :nq].reshape(B, T, ATT_HQ, ATT_DH), q_norm)
    k = rms_norm(p[..., nq:nq + nk].reshape(B, T, ATT_HKV, ATT_DH), k_norm)
    v = p[..., nq + nk:].reshape(B, T, ATT_HKV, ATT_DH).astype(jnp.float32)
    q = q.reshape(B, T, ATT_HKV, ATT_G, ATT_DH).transpose(0, 2, 3, 1, 4)
    return q, k.transpose(0, 2, 1, 3), v.transpose(0, 2, 1, 3)


def softmax_attend(q, k, v):
    s = jnp.einsum('bhgqd,bhsd->bhgqs', q, k).astype(jnp.float32) * (ATT_DH ** -0.5)
    p = jax.nn.softmax(s, axis=-1)
    return jnp.einsum('bhgqs,bhsd->bhgqd', p, v)


def merge_heads(o):
    B, _, _, T, _ = o.shape
    return o.transpose(0, 3, 1, 2, 4).reshape(B, T, ATT_HQ * ATT_DH)


def attn_mixer(h_lat, h_ctx, w_qkv, q_norm, k_norm, w_out, cos, sin, with_ctx_out):
    ql, kl, vl = attn_project(h_lat, w_qkv, q_norm, k_norm)
    ql = rope_2d(ql, cos, sin)
    kl = rope_2d(kl, cos, sin)
    qc, kc, vc = attn_project(h_ctx, w_qkv, q_norm, k_norm)
    k_all = jnp.concatenate([kc, kl], axis=2)
    v_all = jnp.concatenate([vc, vl], axis=2)
    B, _, _, T, _ = ql.shape
    nblk = T // Q_BLOCK
    qb = jnp.moveaxis(ql.reshape(B, ATT_HKV, ATT_G, nblk, Q_BLOCK, ATT_DH), 3, 0)
    ob = lax.map(lambda q_blk: softmax_attend(q_blk, k_all, v_all), qb)
    ol = jnp.moveaxis(ob, 0, 3).reshape(B, ATT_HKV, ATT_G, T, ATT_DH)
    out_lat = merge_heads(ol).astype(h_lat.dtype) @ w_out
    out_ctx = merge_heads(softmax_attend(qc, kc, vc)).astype(h_ctx.dtype) @ w_out if with_ctx_out else None
    return out_lat, out_ctx


def swiglu(h, w_in, w_out):
    gu = h @ w_in
    return (jax.nn.silu(gu[..., :D_FF]) * gu[..., D_FF:]) @ w_out


def setup_inputs(seed: int = 0) -> dict:
    key = jax.random.key(seed)
    ks = jax.random.split(key, 24)
    f32 = jnp.float32
    D = D_MODEL

    def nrm(k, shape, scale):
        return jax.random.normal(k, shape, f32) * scale

    dt = jnp.exp(jax.random.uniform(ks[13], (N_GDN, 2, GDN_HV), f32, math.log(1e-3), math.log(1e-1)))
    return {
        "x": nrm(ks[0], (BATCH, SEQ, D), 1.0),
        "c": nrm(ks[1], (BATCH, D), 1.0),
        "ctx": nrm(ks[2], (BATCH, CTX_LEN, D), 1.0),
        "c_ctx": nrm(ks[3], (D,), 1.0),
        "w_mod": nrm(ks[4], (DEPTH, D, N_MOD * D), D ** -0.5),
        "b_mod": nrm(ks[5], (DEPTH, N_MOD * D), 0.02),
        "ln_g": 1.0 + nrm(ks[6], (DEPTH, 2, D), 0.05),
        "ln_b": nrm(ks[7], (DEPTH, 2, D), 0.02),
        "w_ffn_in": nrm(ks[8], (DEPTH, D, 2 * D_FF), D ** -0.5),
        "w_ffn_out": nrm(ks[9], (DEPTH, D_FF, D), D_FF ** -0.5 * BETA_INIT),
        "gdn_w_in": nrm(ks[10], (N_GDN, D, GDN_IN_W), D ** -0.5),
        "gdn_conv": nrm(ks[11], (N_GDN, CONV_W, GDN_QKV_W), CONV_W ** -0.5),
        "gdn_a_log": jnp.log(jax.random.uniform(ks[12], (N_GDN, 2, GDN_HV), f32, 1.0, 16.0)),
        "gdn_dt_bias": dt + jnp.log(-jnp.expm1(-dt)),
        "gdn_norm_g": 1.0 + nrm(ks[14], (N_GDN, GDN_DV), 0.05),
        "gdn_w_out": nrm(ks[15], (N_GDN, GDN_V_W, D), GDN_V_W ** -0.5 * BETA_INIT),
        "attn_w_qkv": nrm(ks[16], (N_ATT, D, ATT_IN_W), D ** -0.5),
        "attn_q_norm": 1.0 + nrm(ks[17], (N_ATT, ATT_DH), 0.05),
        "attn_k_norm": 1.0 + nrm(ks[18], (N_ATT, ATT_DH), 0.05),
        "attn_w_out": nrm(ks[19], (N_ATT, ATT_HQ * ATT_DH, D), (ATT_HQ * ATT_DH) ** -0.5 * BETA_INIT),
    }


def reference(x, c, ctx, c_ctx, w_mod, b_mod, ln_g, ln_b, w_ffn_in, w_ffn_out,
              gdn_w_in, gdn_conv, gdn_a_log, gdn_dt_bias, gdn_norm_g, gdn_w_out,
              attn_w_qkv, attn_q_norm, attn_k_norm, attn_w_out):
    cos, sin = rope_tables(x.shape[1])
    xl, xc = x, ctx
    for i in range(DEPTH):
        with_ctx_out = i < DEPTH - 1
        ml = modulation(c, w_mod[i], b_mod[i])
        mc = modulation(c_ctx, w_mod[i], b_mod[i])
        sh_l, sc_l, ga_l, shf_l, scf_l, gaf_l = (ml[:, j, None, :] for j in range(N_MOD))
        sh_c, sc_c, ga_c, shf_c, scf_c, gaf_c = (mc[j] for j in range(N_MOD))
        hl = modulate(xl, sh_l, sc_l)
        hc = modulate(xc, sh_c, sc_c)
        j = i // N_MIXERS
        if i % N_MIXERS == 0:
            dl, dc = gdn_mixer(hl, hc, gdn_w_in[j], gdn_conv[j], gdn_a_log[j], gdn_dt_bias[j],
                               gdn_norm_g[j], gdn_w_out[j], with_ctx_out)
        else:
            dl, dc = attn_mixer(hl, hc, attn_w_qkv[j], attn_q_norm[j], attn_k_norm[j], attn_w_out[j],
                                cos, sin, with_ctx_out)
        xl = post_norm(xl, dl, ga_l, ln_g[i, 0], ln_b[i, 0])
        xl = post_norm(xl, swiglu(modulate(xl, shf_l, scf_l), w_ffn_in[i], w_ffn_out[i]),
                       gaf_l, ln_g[i, 1], ln_b[i, 1])
        if with_ctx_out:
            xc = post_norm(xc, dc, ga_c, ln_g[i, 0], ln_b[i, 0])
            xc = post_norm(xc, swiglu(modulate(xc, shf_c, scf_c), w_ffn_in[i], w_ffn_out[i]),
                           gaf_c, ln_g[i, 1], ln_b[i, 1])
    return xl
```

```python
import functools

import jax
import jax.numpy as jnp
from jax import lax
from jax.experimental import pallas as pl
from jax.experimental.pallas import tpu as pltpu

F32 = jnp.float32
BF16 = jnp.bfloat16

N_MOD = 6
EPS = 1e-6
GDN_DK = 128
CHUNK = 64
ATT_DH = 128
GRID_W = 64
ROPE_THETA = 10000.0

LANES = 128
SUBLANES = 8
PAIR = 2 * CHUNK
VMEM_LIMIT_BYTES = 56 * 1024 * 1024
TOKEN_TILE = 256
NEG_INF = -1e30

assert PAIR == LANES


def _params(sem):
    return pltpu.CompilerParams(dimension_semantics=sem, vmem_limit_bytes=VMEM_LIMIT_BYTES)


def _dot(a, b):
    return jnp.dot(a, b, preferred_element_type=F32)


def _split2(a):
    hi = a.astype(BF16)
    return hi, (a - hi.astype(F32)).astype(BF16)


def _dot_hi(a, b):
    a1, a2 = _split2(a)
    b1, b2 = _split2(b)
    return _dot(a1, b1) + (_dot(a1, b2) + _dot(a2, b1))


def _sigmoid(x):
    return 1.0 / (1.0 + jnp.exp(-x))


def _silu(x):
    return x * _sigmoid(x)


def _layer_norm(r, g, b):
    mu = jnp.mean(r, axis=-1, keepdims=True)
    rc = r - mu
    var = jnp.mean(rc * rc, axis=-1, keepdims=True)
    return rc * lax.rsqrt(var + EPS) * g + b


def _mod_body(cond_ref, w_ref, b_ref, o_ref):
    o_ref[0] = _dot_hi(_silu(cond_ref[...]), w_ref[0]) + b_ref[0]


def _modulation(cond, w_mod, b_mod):
    n_layer, d, n = w_mod.shape
    mp = cond.shape[0]
    tn = n // 4
    return pl.pallas_call(
        _mod_body,
        grid=(n_layer, n // tn),
        in_specs=[
            pl.BlockSpec((mp, d), lambda l, j: (0, 0)),
            pl.BlockSpec((1, d, tn), lambda l, j: (l, 0, j)),
            pl.BlockSpec((1, 1, tn), lambda l, j: (l, 0, j)),
        ],
        out_specs=pl.BlockSpec((1, mp, tn), lambda l, j: (l, 0, j)),
        out_shape=jax.ShapeDtypeStruct((n_layer, mp, n), F32),
        compiler_params=_params(("parallel", "parallel")),
        name="modulation",
    )(cond, w_mod, b_mod.reshape(n_layer, 1, n))


def _mod_row_map(n_batch, n_ctx_tiles):
    return lambda b, t: (jnp.where(t < n_ctx_tiles, n_batch, b), 0, 0)


def _in_proj_body(x_ref, mod_ref, w_ref, *rest, n_chunk, has_extra):
    if has_extra:
        wx_ref, o_ref, ox_ref = rest
    else:
        (o_ref,) = rest
    h = x_ref[0] * (1.0 + mod_ref[0, 1:2, :]) + mod_ref[0, 0:1, :]
    hb = h.astype(BF16)
    tn = w_ref.shape[1] // n_chunk
    for j in range(n_chunk):
        o_ref[0, :, j * tn:(j + 1) * tn] = _dot(hb, w_ref[:, j * tn:(j + 1) * tn]).astype(o_ref.dtype)
    if has_extra:
        ox_ref[0] = _dot_hi(h, wx_ref[...])


def _in_proj(xs, mods, w, w_extra, n_ctx_tiles, n_chunk):
    n_batch, t_tot, d = xs.shape
    n = w.shape[1]
    tm = TOKEN_TILE
    has_extra = w_extra is not None
    in_specs = [
        pl.BlockSpec((1, tm, d), lambda b, t: (b, t, 0)),
        pl.BlockSpec((1, N_MOD, d), _mod_row_map(n_batch, n_ctx_tiles)),
        pl.BlockSpec((d, n), lambda b, t: (0, 0)),
    ]
    out_specs = [pl.BlockSpec((1, tm, n), lambda b, t: (b, t, 0))]
    out_shape = [jax.ShapeDtypeStruct((n_batch, t_tot, n), BF16)]
    args = [xs, mods, w]
    if has_extra:
        in_specs.append(pl.BlockSpec((d, LANES), lambda b, t: (0, 0)))
        out_specs.append(pl.BlockSpec((1, tm, LANES), lambda b, t: (b, t, 0)))
        out_shape.append(jax.ShapeDtypeStruct((n_batch, t_tot, LANES), F32))
        args.append(w_extra)
    return pl.pallas_call(
        functools.partial(_in_proj_body, n_chunk=n_chunk, has_extra=has_extra),
        grid=(n_batch, t_tot // tm),
        in_specs=in_specs,
        out_specs=out_specs,
        out_shape=out_shape,
        compiler_params=_params(("parallel", "parallel")),
        name="in_proj",
    )(*args)


def _gates_body(raw_ref, alog_ref, dtb_ref, col_ref, row_ref, *, n_hv):
    raw = raw_ref[0]
    beta = _sigmoid(raw)
    xs = raw + dtb_ref[...]
    g = -jnp.exp(alog_ref[...]) * (jnp.maximum(xs, 0.0) + jnp.log1p(jnp.exp(-jnp.abs(xs))))
    r = lax.broadcasted_iota(jnp.int32, (PAIR, PAIR), 0)
    c = lax.broadcasted_iota(jnp.int32, (PAIR, PAIR), 1)
    same = (r < CHUNK) == (c < CHUNK)
    tri_f = jnp.where(jnp.logical_and(same, c <= r), 1.0, 0.0).astype(BF16)
    tri_r = jnp.where(jnp.logical_and(same, c >= r), 1.0, 0.0).astype(BF16)
    lane = lax.broadcasted_iota(jnp.int32, (PAIR, LANES), 1)
    for p in range(raw.shape[0] // PAIR):
        gp = g[p * PAIR:(p + 1) * PAIR]
        g1 = gp.astype(BF16)
        r1 = gp - g1.astype(F32)
        g2 = r1.astype(BF16)
        g3 = (r1 - g2.astype(F32)).astype(BF16)
        cum_f = _dot(tri_f, g1) + (_dot(tri_f, g2) + _dot(tri_f, g3))
        cum_r = _dot(tri_r, g1) + (_dot(tri_r, g2) + _dot(tri_r, g3))
        col = jnp.where(lane < 2 * n_hv, beta[p * PAIR:(p + 1) * PAIR],
                        jnp.where(lane < 3 * n_hv, cum_f, cum_r))
        col_ref[0, p * PAIR:(p + 1) * PAIR, :] = col
        row_ref[0, p] = col.T


def _gdn_gates(raw, a_log_lanes, dt_bias_lanes, n_hv):
    n_batch, t_tot, _ = raw.shape
    tg = TOKEN_TILE
    return pl.pallas_call(
        functools.partial(_gates_body, n_hv=n_hv),
        grid=(n_batch, t_tot // tg),
        in_specs=[
            pl.BlockSpec((1, tg, LANES), lambda b, t: (b, t, 0)),
            pl.BlockSpec((1, LANES), lambda b, t: (0, 0)),
            pl.BlockSpec((1, LANES), lambda b, t: (0, 0)),
        ],
        out_specs=[
            pl.BlockSpec((1, tg, LANES), lambda b, t: (b, t, 0)),
            pl.BlockSpec((1, tg // PAIR, LANES, PAIR), lambda b, t: (b, t, 0, 0)),
        ],
        out_shape=[
            jax.ShapeDtypeStruct((n_batch, t_tot, LANES), F32),
            jax.ShapeDtypeStruct((n_batch, t_tot // PAIR, LANES, PAIR), F32),
        ],
        compiler_params=_params(("parallel", "parallel")),
        name="gdn_gates",
    )(raw, a_log_lanes, dt_bias_lanes)


CONV_HALO = 16
CONV_TILE = 256


def _conv_tile(src_ref, w, r0, t_tot, ctx_len, conv_w):
    lo = pl.multiple_of(jnp.maximum(r0 - CONV_HALO, 0), CONV_HALO)
    hi = pl.multiple_of(jnp.minimum(r0 + CONV_TILE, t_tot - CONV_HALO), CONV_HALO)
    ext = jnp.concatenate([src_ref[0, pl.ds(lo, CONV_HALO), :],
                           src_ref[0, pl.ds(r0, CONV_TILE), :],
                           src_ref[0, pl.ds(hi, CONV_HALO), :]], axis=0).astype(F32)
    n_ext = CONV_TILE + 2 * CONV_HALO
    t = r0 + lax.broadcasted_iota(jnp.int32, (CONV_TILE, ext.shape[1]), 0)
    seg_lo = jnp.where(t < ctx_len, 0, ctx_len)
    seg_hi = jnp.where(t < ctx_len, ctx_len, t_tot)
    pad = conv_w // 2
    y = None
    for j in range(conv_w):
        off = j - pad
        xs = ext if off == 0 else pltpu.roll(ext, (-off) % n_ext, axis=0)
        xs = xs[CONV_HALO:CONV_HALO + CONV_TILE]
        if off != 0:
            src_t = t + off
            xs = jnp.where(jnp.logical_and(src_t >= seg_lo, src_t < seg_hi), xs, 0.0)
        term = xs * w[j:j + 1]
        y = term if y is None else y + term
    return _silu(y)


def _l2_norm(x):
    return x * lax.rsqrt(jnp.sum(x * x, axis=-1, keepdims=True) + EPS)


def _tri_inverse(a, reverse):
    n = a.shape[0]
    r = lax.broadcasted_iota(jnp.int32, (n, n), 0)
    c = lax.broadcasted_iota(jnp.int32, (n, n), 1)
    x = jnp.where(r == c, 1.0, 0.0)
    for lg in range(n.bit_length() - 1):
        same_big = (r >> (lg + 1)) == (c >> (lg + 1))
        diff_small = (r >> lg) != (c >> lg)
        off = jnp.where(jnp.logical_and(same_big, diff_small), a, 0.0)
        if lg == 0:
            x = x - off
        else:
            x = x - _dot_hi(x, _dot_hi(off, x))
    return x


def _delta_body(q_ref, k_ref, v_ref, z_ref, cq_ref, ck_ref, cv_ref, colg_ref, rowg_ref, ng_ref,
                y_ref, q_s, k_s, v_s, o_s, st_s, cg_s, *, ctx_len, n_hv, rep, conv_w):
    hk = pl.program_id(1)
    t_tot = q_ref.shape[1]
    dk = q_ref.shape[2]
    dv = v_ref.shape[2] // rep
    n_pairs = t_tot // PAIR
    n_ctx_pairs = ctx_len // PAIR

    lane_shift = (LANES - rep * hk) % LANES

    def prep(i, carry):
        r0 = pl.multiple_of(i * CONV_TILE, CONV_TILE)
        q = _l2_norm(_conv_tile(q_ref, cq_ref[...], r0, t_tot, ctx_len, conv_w)) * (dk ** -0.5)
        q_s[pl.ds(r0, CONV_TILE), :] = q
        k_s[pl.ds(r0, CONV_TILE), :] = _l2_norm(_conv_tile(k_ref, ck_ref[...], r0, t_tot, ctx_len, conv_w))
        v_s[pl.ds(r0, CONV_TILE), :] = _conv_tile(v_ref, cv_ref[...], r0, t_tot, ctx_len, conv_w)
        cg_s[pl.ds(r0, CONV_TILE), :] = pltpu.roll(colg_ref[0, pl.ds(r0, CONV_TILE), :], lane_shift, axis=1)
        return carry

    lax.fori_loop(0, t_tot // CONV_TILE, prep, 0)
    o_s[...] = jnp.zeros_like(o_s)
    st_s[...] = jnp.zeros_like(st_s)

    ri = lax.broadcasted_iota(jnp.int32, (CHUNK, CHUNK), 0)
    ci = lax.broadcasted_iota(jnp.int32, (CHUNK, CHUNK), 1)

    def chunk_step(p, half, d):
        t0 = pl.multiple_of(p * PAIR + half * CHUNK, CHUNK)
        q = q_s[pl.ds(t0, CHUNK), :]
        k = k_s[pl.ds(t0, CHUNK), :]
        qb = q.astype(BF16)
        kb = k.astype(BF16)
        kk = lax.dot_general(kb, kb, (((1,), (1,)), ((), ())), preferred_element_type=F32)
        qk = lax.dot_general(qb, kb, (((1,), (1,)), ((), ())), preferred_element_type=F32)
        cg = cg_s[pl.ds(t0, CHUNK), :]
        incl = (ci >= ri) if d else (ci <= ri)
        strict = (ci > ri) if d else (ci < ri)
        for j in range(rep):
            ch = d * rep + j
            beta_c = cg[:, d * n_hv + j:d * n_hv + j + 1]
            gc_c = cg[:, (2 + d) * n_hv + j:(2 + d) * n_hv + j + 1]
            rows_b = rowg_ref[0, p, pl.ds(d * n_hv + rep * hk + j, 1), :]
            rows_g = rowg_ref[0, p, pl.ds((2 + d) * n_hv + rep * hk + j, 1), :]
            beta_r = rows_b[:, half * CHUNK:(half + 1) * CHUNK]
            gc_r = rows_g[:, half * CHUNK:(half + 1) * CHUNK]
            decay = jnp.where(incl, jnp.exp(jnp.where(incl, gc_c - gc_r, 0.0)), 0.0)
            a = jnp.where(strict, kk * decay, 0.0) * beta_c
            t_inv = _tri_inverse(a, d)
            tb = t_inv * beta_r
            v = v_s[pl.ds(t0, CHUNK), j * dv:(j + 1) * dv]
            u = _dot(tb.astype(BF16), v.astype(BF16))
            w = _dot((tb * jnp.exp(gc_r)).astype(BF16), kb)
            s = st_s[ch]
            sb = s.astype(BF16)
            v_new = u - _dot(w.astype(BF16), sb)
            vb = v_new.astype(BF16)
            o = _dot(qb, sb) * jnp.exp(gc_c) + _dot((qk * decay).astype(BF16), vb)
            g_last = gc_r[:, 0:1] if d else gc_r[:, CHUNK - 1:CHUNK]
            kd = (k * jnp.exp(g_last - gc_c)).astype(BF16)
            st_s[ch] = s * jnp.exp(g_last) + lax.dot_general(
                kd, vb, (((0,), (0,)), ((), ())), preferred_element_type=F32)
            o_s[pl.ds(t0, CHUNK), j * dv:(j + 1) * dv] += o

    def step(i, carry):
        chunk_step(i, 0, 0)
        chunk_step(i, 1, 0)
        pr = jnp.where(i < n_ctx_pairs, n_ctx_pairs - 1 - i, n_pairs + n_ctx_pairs - 1 - i)
        chunk_step(pr, 1, 1)
        chunk_step(pr, 0, 1)
        return carry

    lax.fori_loop(0, n_pairs, step, 0)

    def finish(i, carry):
        r0 = pl.multiple_of(i * CONV_TILE, CONV_TILE)
        z = z_ref[0, pl.ds(r0, CONV_TILE), :].astype(F32)
        for j in range(rep):
            o = o_s[pl.ds(r0, CONV_TILE), j * dv:(j + 1) * dv]
            y = o * lax.rsqrt(jnp.mean(o * o, axis=-1, keepdims=True) + EPS) * ng_ref[...]
            y_ref[0, pl.ds(r0, CONV_TILE), j * dv:(j + 1) * dv] = (y * _silu(z[:, j * dv:(j + 1) * dv])).astype(y_ref.dtype)
        return carry

    lax.fori_loop(0, t_tot // CONV_TILE, finish, 0)


def _gdn_delta(p_main, conv, colg, rowg, norm_g, ctx_len, n_hk, n_hv):
    n_batch, t_tot, _ = p_main.shape
    conv_w, qkv_w = conv.shape
    dk = GDN_DK
    dv = norm_g.shape[-1]
    rep = n_hv // n_hk
    qk_w = n_hk * dk
    vb = rep * dv
    v_blk0 = (2 * qk_w) // vb
    z_blk0 = qkv_w // vb
    n_pairs = t_tot // PAIR
    return pl.pallas_call(
        functools.partial(_delta_body, ctx_len=ctx_len, n_hv=n_hv, rep=rep, conv_w=conv_w),
        grid=(n_batch, n_hk),
        in_specs=[
            pl.BlockSpec((1, t_tot, dk), lambda b, h: (b, 0, h)),
            pl.BlockSpec((1, t_tot, dk), lambda b, h: (b, 0, n_hk + h)),
            pl.BlockSpec((1, t_tot, vb), lambda b, h: (b, 0, v_blk0 + h)),
            pl.BlockSpec((1, t_tot, vb), lambda b, h: (b, 0, z_blk0 + h)),
            pl.BlockSpec((conv_w, dk), lambda b, h: (0, h)),
            pl.BlockSpec((conv_w, dk), lambda b, h: (0, n_hk + h)),
            pl.BlockSpec((conv_w, vb), lambda b, h: (0, v_blk0 + h)),
            pl.BlockSpec((1, t_tot, LANES), lambda b, h: (b, 0, 0)),
            pl.BlockSpec((1, n_pairs, LANES, PAIR), lambda b, h: (b, 0, 0, 0)),
            pl.BlockSpec((1, dv), lambda b, h: (0, 0)),
        ],
        out_specs=pl.BlockSpec((1, t_tot, vb), lambda b, h: (b, 0, h)),
        out_shape=jax.ShapeDtypeStruct((n_batch, t_tot, n_hv * dv), BF16),
        scratch_shapes=[
            pltpu.VMEM((t_tot, dk), F32),
            pltpu.VMEM((t_tot, dk), F32),
            pltpu.VMEM((t_tot, vb), F32),
            pltpu.VMEM((t_tot, vb), F32),
            pltpu.VMEM((2 * rep, dk, dv), F32),
            pltpu.VMEM((t_tot, LANES), F32),
        ],
        compiler_params=_params(("parallel", "parallel")),
        name="gdn_delta",
    )(p_main, p_main, p_main, p_main, conv, conv, conv, colg, rowg, norm_g)


def _rope(x, cos, sin_a, sin_b):
    quarter = ATT_DH // 4
    return x * cos + pltpu.roll(x, ATT_DH - quarter, axis=1) * sin_a + pltpu.roll(x, quarter, axis=1) * sin_b


def _rms_norm(x, g):
    return x * lax.rsqrt(jnp.mean(x * x, axis=-1, keepdims=True) + EPS) * g


def _attn_body(q_ref, k_ref, v_ref, cos_ref, sa_ref, sb_ref, qn_ref, kn_ref, o_ref, kt_s,
               *, ctx_len, n_group):
    qi = pl.program_id(2)
    tq = q_ref.shape[1]
    t_tot = k_ref.shape[1]
    dh = k_ref.shape[2]

    @pl.when(qi == 0)
    def _():
        k = _rms_norm(k_ref[0].astype(F32), kn_ref[...])
        k = _rope(k, cos_ref[...], sa_ref[...], sb_ref[...])
        kt_s[...] = k.T.astype(kt_s.dtype)

    r0 = pl.multiple_of(qi * tq, tq)
    cos = cos_ref[pl.ds(r0, tq), :]
    sin_a = sa_ref[pl.ds(r0, tq), :]
    sin_b = sb_ref[pl.ds(r0, tq), :]
    limit = jnp.where(r0 < ctx_len, ctx_len, t_tot)
    visible = lax.broadcasted_iota(jnp.int32, (tq, t_tot), 1) < limit
    v = v_ref[0]
    for g in range(n_group):
        q = _rms_norm(q_ref[0, :, g * dh:(g + 1) * dh].astype(F32), qn_ref[...])
        q = _rope(q, cos, sin_a, sin_b) * (dh ** -0.5)
        s = jnp.where(visible, _dot(q.astype(BF16), kt_s[...]), NEG_INF)
        p = jnp.exp(s - jnp.max(s, axis=-1, keepdims=True))
        l = jnp.sum(p, axis=-1, keepdims=True)
        o_ref[0, :, g * dh:(g + 1) * dh] = (_dot(p.astype(BF16), v) / l).astype(o_ref.dtype)


def _attention(p_qkv, tables, q_norm, k_norm, ctx_len, n_hq, n_hkv):
    n_batch, t_tot, _ = p_qkv.shape
    dh = ATT_DH
    n_group = n_hq // n_hkv
    tq = TOKEN_TILE
    cos, sin_a, sin_b = tables
    full = lambda shape: pl.BlockSpec(shape, lambda b, h, q: (0,) * len(shape))
    return pl.pallas_call(
        functools.partial(_attn_body, ctx_len=ctx_len, n_group=n_group),
        grid=(n_batch, n_hkv, t_tot // tq),
        in_specs=[
            pl.BlockSpec((1, tq, n_group * dh), lambda b, h, q: (b, q, h)),
            pl.BlockSpec((1, t_tot, dh), lambda b, h, q: (b, 0, n_hq + h)),
            pl.BlockSpec((1, t_tot, dh), lambda b, h, q: (b, 0, n_hq + n_hkv + h)),
            full((t_tot, dh)), full((t_tot, dh)), full((t_tot, dh)),
            full((1, dh)), full((1, dh)),
        ],
        out_specs=pl.BlockSpec((1, tq, n_group * dh), lambda b, h, q: (b, q, h)),
        out_shape=jax.ShapeDtypeStruct((n_batch, t_tot, n_hq * dh), BF16),
        scratch_shapes=[pltpu.VMEM((dh, t_tot), BF16)],
        compiler_params=_params(("parallel", "parallel", "arbitrary")),
        name="attention",
    )(p_qkv, p_qkv, p_qkv, cos, sin_a, sin_b, q_norm, k_norm)


def _rope_tables(ctx_len, seq_len):
    rows = seq_len // GRID_W
    row = jnp.repeat(jnp.arange(rows), GRID_W).astype(F32)
    col = jnp.tile(jnp.arange(GRID_W), rows).astype(F32)
    n_freq = ATT_DH // 4
    freqs = ROPE_THETA ** (-jnp.arange(n_freq, dtype=F32) / n_freq)
    ang_r = row[:, None] * freqs
    ang_c = col[:, None] * freqs
    ang = jnp.concatenate([ang_r, ang_r, ang_c, ang_c], axis=-1)
    cos = jnp.concatenate([jnp.ones((ctx_len, ATT_DH), F32), jnp.cos(ang)], axis=0)
    sin = jnp.concatenate([jnp.zeros((ctx_len, ATT_DH), F32), jnp.sin(ang)], axis=0)
    first_half = (jnp.arange(ATT_DH) % (2 * n_freq)) < n_freq
    return cos, jnp.where(first_half, -sin, 0.0), jnp.where(first_half, 0.0, sin)


def _mix_out_body(y_ref, w_ref, x_ref, mod_ref, lng_ref, lnb_ref, o_ref, *, alpha):
    delta = _dot(y_ref[0], w_ref[...])
    r = alpha * x_ref[0] + mod_ref[0, 2:3, :] * delta
    o_ref[0] = _layer_norm(r, lng_ref[...], lnb_ref[...])


def _mix_out(y, w, xs, mods, ln_g, ln_b, n_ctx_tiles, alpha):
    n_batch, t_tot, d = xs.shape
    k = y.shape[-1]
    tm = TOKEN_TILE
    return pl.pallas_call(
        functools.partial(_mix_out_body, alpha=alpha),
        grid=(n_batch, t_tot // tm),
        in_specs=[
            pl.BlockSpec((1, tm, k), lambda b, t: (b, t, 0)),
            pl.BlockSpec((k, d), lambda b, t: (0, 0)),
            pl.BlockSpec((1, tm, d), lambda b, t: (b, t, 0)),
            pl.BlockSpec((1, N_MOD, d), _mod_row_map(n_batch, n_ctx_tiles)),
            pl.BlockSpec((1, d), lambda b, t: (0, 0)),
            pl.BlockSpec((1, d), lambda b, t: (0, 0)),
        ],
        out_specs=pl.BlockSpec((1, tm, d), lambda b, t: (b, t, 0)),
        out_shape=jax.ShapeDtypeStruct(xs.shape, F32),
        compiler_params=_params(("parallel", "parallel")),
        name="mix_out",
    )(y, w, xs, mods, ln_g, ln_b)


def _ffn_body(x_ref, mod_ref, win_ref, wout_ref, lng_ref, lnb_ref, o_ref, *, n_chunk, alpha):
    x = x_ref[0]
    hb = (x * (1.0 + mod_ref[0, 4:5, :]) + mod_ref[0, 3:4, :]).astype(BF16)
    d_ff = wout_ref.shape[0]
    tc = d_ff // n_chunk
    acc = None
    for c in range(n_chunk):
        gate = _dot(hb, win_ref[:, c * tc:(c + 1) * tc])
        up = _dot(hb, win_ref[:, d_ff + c * tc:d_ff + (c + 1) * tc])
        part = _dot((_silu(gate) * up).astype(BF16), wout_ref[c * tc:(c + 1) * tc, :])
        acc = part if acc is None else acc + part
    r = alpha * x + mod_ref[0, 5:6, :] * acc
    o_ref[0] = _layer_norm(r, lng_ref[...], lnb_ref[...])


def _ffn(xs, mods, w_in, w_out, ln_g, ln_b, n_ctx_tiles, alpha):
    n_batch, t_tot, d = xs.shape
    d_ff = w_out.shape[0]
    tm = TOKEN_TILE
    n_chunk = 2 if d_ff % (2 * LANES) == 0 else 1
    return pl.pallas_call(
        functools.partial(_ffn_body, n_chunk=n_chunk, alpha=alpha),
        grid=(n_batch, t_tot // tm),
        in_specs=[
            pl.BlockSpec((1, tm, d), lambda b, t: (b, t, 0)),
            pl.BlockSpec((1, N_MOD, d), _mod_row_map(n_batch, n_ctx_tiles)),
            pl.BlockSpec((d, 2 * d_ff), lambda b, t: (0, 0)),
            pl.BlockSpec((d_ff, d), lambda b, t: (0, 0)),
            pl.BlockSpec((1, d), lambda b, t: (0, 0)),
            pl.BlockSpec((1, d), lambda b, t: (0, 0)),
        ],
        out_specs=pl.BlockSpec((1, tm, d), lambda b, t: (b, t, 0)),
        out_shape=jax.ShapeDtypeStruct(xs.shape, F32),
        compiler_params=_params(("parallel", "parallel")),
        name="ffn",
    )(xs, mods, w_in, w_out, ln_g, ln_b)


def kernel(x, c, ctx, c_ctx, w_mod, b_mod, ln_g, ln_b, w_ffn_in, w_ffn_out, gdn_w_in, gdn_conv, gdn_a_log,
           gdn_dt_bias, gdn_norm_g, gdn_w_out, attn_w_qkv, attn_q_norm, attn_k_norm, attn_w_out):
    n_batch, seq_len, d = x.shape
    ctx_len = ctx.shape[1]
    depth = w_mod.shape[0]
    alpha = (2 * depth) ** 0.25
    n_hv = gdn_a_log.shape[-1]
    dv = gdn_norm_g.shape[-1]
    v_w = n_hv * dv
    qkv_w = gdn_conv.shape[-1]
    n_hk = (qkv_w - v_w) // (2 * GDN_DK)
    n_hq = attn_w_out.shape[1] // ATT_DH
    n_hkv = (attn_w_qkv.shape[-1] // ATT_DH - n_hq) // 2
    assert ctx_len % TOKEN_TILE == 0 and seq_len % TOKEN_TILE == 0 and seq_len % GRID_W == 0
    assert 4 * n_hv <= LANES and d % LANES == 0
    n_ctx_tiles = ctx_len // TOKEN_TILE

    xs = jnp.concatenate([ctx, x], axis=1)

    mp = -(-(n_batch + 1) // SUBLANES) * SUBLANES
    cond = jnp.concatenate([c, c_ctx[None, :], jnp.zeros((mp - n_batch - 1, d), F32)], axis=0)
    mods = _modulation(cond, w_mod, b_mod).reshape(depth, mp, N_MOD, d)
    tables = _rope_tables(ctx_len, seq_len)

    for i in range(depth):
        j = i // 2
        m = mods[i]
        if i % 2 == 0:
            w_in = gdn_w_in[j]
            gate_w = jnp.pad(w_in[:, qkv_w + v_w:], ((0, 0), (0, LANES - 4 * n_hv)))
            p_main, raw = _in_proj(xs, m, w_in[:, :qkv_w + v_w].astype(BF16), gate_w, n_ctx_tiles, n_chunk=6)
            lanes_pad = (0, LANES - 4 * n_hv)
            a_log = jnp.pad(jnp.concatenate([jnp.zeros((2 * n_hv,), F32), gdn_a_log[j].reshape(-1)]), lanes_pad)
            dt_b = jnp.pad(jnp.concatenate([jnp.zeros((2 * n_hv,), F32), gdn_dt_bias[j].reshape(-1)]), lanes_pad)
            colg, rowg = _gdn_gates(raw, a_log[None, :], dt_b[None, :], n_hv)
            y = _gdn_delta(p_main, gdn_conv[j], colg, rowg, gdn_norm_g[j][None, :], ctx_len, n_hk, n_hv)
            w_o = gdn_w_out[j]
        else:
            (p_qkv,) = _in_proj(xs, m, attn_w_qkv[j].astype(BF16), None, n_ctx_tiles, n_chunk=3)
            y = _attention(p_qkv, tables, attn_q_norm[j][None, :], attn_k_norm[j][None, :], ctx_len, n_hq, n_hkv)
            w_o = attn_w_out[j]
        xs = _mix_out(y, w_o.astype(BF16), xs, m, ln_g[i, 0][None, :], ln_b[i, 0][None, :], n_ctx_tiles, alpha)
        xs = _ffn(xs, m, w_ffn_in[i].astype(BF16), w_ffn_out[i].astype(BF16),
                  ln_g[i, 1][None, :], ln_b[i, 1][None, :], n_ctx_tiles, alpha)
    return xs[:, ctx_len:]
```

```python
import functools

import jax
import jax.numpy as jnp
from jax import lax
from jax.experimental import pallas as pl
from jax.experimental.pallas import tpu as pltpu

F32 = jnp.float32
BF16 = jnp.bfloat16

N_MOD = 6
EPS = 1e-6
GDN_DK = 128
CHUNK = 64
ATT_DH = 128
GRID_W = 64
ROPE_THETA = 10000.0

LANES = 128
SUBLANES = 8
PAIR = 2 * CHUNK
VMEM_LIMIT_BYTES = 56 * 1024 * 1024
TOKEN_TILE = 256
NEG_INF = -1e30

assert PAIR == LANES


def _params(sem):
    return pltpu.CompilerParams(dimension_semantics=sem, vmem_limit_bytes=VMEM_LIMIT_BYTES)


def _dot(a, b):
    return jnp.dot(a, b, preferred_element_type=F32)


def _split2(a):
    hi = a.astype(BF16)
    return hi, (a - hi.astype(F32)).astype(BF16)


def _dot_hi(a, b):
    a1, a2 = _split2(a)
    b1, b2 = _split2(b)
    return _dot(a1, b1) + (_dot(a1, b2) + _dot(a2, b1))


def _sigmoid(x):
    return 1.0 / (1.0 + jnp.exp(-x))


def _silu(x):
    return x * _sigmoid(x)


def _layer_norm(r, g, b):
    mu = jnp.mean(r, axis=-1, keepdims=True)
    rc = r - mu
    var = jnp.mean(rc * rc, axis=-1, keepdims=True)
    return rc * lax.rsqrt(var + EPS) * g + b


def _mod_body(cond_ref, w_ref, b_ref, o_ref):
    o_ref[0] = _dot_hi(_silu(cond_ref[...]), w_ref[0]) + b_ref[0]


def _modulation(cond, w_mod, b_mod):
    n_layer, d, n = w_mod.shape
    mp = cond.shape[0]
    tn = n // 4
    return pl.pallas_call(
        _mod_body,
        grid=(n_layer, n // tn),
        in_specs=[
            pl.BlockSpec((mp, d), lambda l, j: (0, 0)),
            pl.BlockSpec((1, d, tn), lambda l, j: (l, 0, j)),
            pl.BlockSpec((1, 1, tn), lambda l, j: (l, 0, j)),
        ],
        out_specs=pl.BlockSpec((1, mp, tn), lambda l, j: (l, 0, j)),
        out_shape=jax.ShapeDtypeStruct((n_layer, mp, n), F32),
        compiler_params=_params(("parallel", "parallel")),
        name="modulation",
    )(cond, w_mod, b_mod.reshape(n_layer, 1, n))


def _mod_row_map(n_batch, n_ctx_tiles):
    return lambda b, t: (jnp.where(t < n_ctx_tiles, n_batch, b), 0, 0)


def _in_proj_body(x_ref, mod_ref, w_ref, *rest, n_chunk, has_extra):
    if has_extra:
        wx_ref, o_ref, ox_ref = rest
    else:
        (o_ref,) = rest
    h = x_ref[0] * (1.0 + mod_ref[0, 1:2, :]) + mod_ref[0, 0:1, :]
    hb = h.astype(BF16)
    tn = w_ref.shape[1] // n_chunk
    for j in range(n_chunk):
        o_ref[0, :, j * tn:(j + 1) * tn] = _dot(hb, w_ref[:, j * tn:(j + 1) * tn]).astype(o_ref.dtype)
    if has_extra:
        ox_ref[0] = _dot_hi(h, wx_ref[...])


def _in_proj(xs, mods, w, w_extra, n_ctx_tiles, n_chunk):
    n_batch, t_tot, d = xs.shape
    n = w.shape[1]
    tm = TOKEN_TILE
    has_extra = w_extra is not None
    in_specs = [
        pl.BlockSpec((1, tm, d), lambda b, t: (b, t, 0)),
        pl.BlockSpec((1, N_MOD, d), _mod_row_map(n_batch, n_ctx_tiles)),
        pl.BlockSpec((d, n), lambda b, t: (0, 0)),
    ]
    out_specs = [pl.BlockSpec((1, tm, n), lambda b, t: (b, t, 0))]
    out_shape = [jax.ShapeDtypeStruct((n_batch, t_tot, n), BF16)]
    args = [xs, mods, w]
    if has_extra:
        in_specs.append(pl.BlockSpec((d, LANES), lambda b, t: (0, 0)))
        out_specs.append(pl.BlockSpec((1, tm, LANES), lambda b, t: (b, t, 0)))
        out_shape.append(jax.ShapeDtypeStruct((n_batch, t_tot, LANES), F32))
        args.append(w_extra)
    return pl.pallas_call(
        functools.partial(_in_proj_body, n_chunk=n_chunk, has_extra=has_extra),
        grid=(n_batch, t_tot // tm),
        in_specs=in_specs,
        out_specs=out_specs,
        out_shape=out_shape,
        compiler_params=_params(("parallel", "parallel")),
        name="in_proj",
    )(*args)


def _gates_body(raw_ref, alog_ref, dtb_ref, col_ref, row_ref, *, n_hv):
    raw = raw_ref[0]
    beta = _sigmoid(raw)
    xs = raw + dtb_ref[...]
    g = -jnp.exp(alog_ref[...]) * (jnp.maximum(xs, 0.0) + jnp.log1p(jnp.exp(-jnp.abs(xs))))
    r = lax.broadcasted_iota(jnp.int32, (PAIR, PAIR), 0)
    c = lax.broadcasted_iota(jnp.int32, (PAIR, PAIR), 1)
    same = (r < CHUNK) == (c < CHUNK)
    tri_f = jnp.where(jnp.logical_and(same, c <= r), 1.0, 0.0).astype(BF16)
    tri_r = jnp.where(jnp.logical_and(same, c >= r), 1.0, 0.0).astype(BF16)
    lane = lax.broadcasted_iota(jnp.int32, (PAIR, LANES), 1)
    for p in range(raw.shape[0] // PAIR):
        gp = g[p * PAIR:(p + 1) * PAIR]
        g1 = gp.astype(BF16)
        r1 = gp - g1.astype(F32)
        g2 = r1.astype(BF16)
        g3 = (r1 - g2.astype(F32)).astype(BF16)
        cum_f = _dot(tri_f, g1) + (_dot(tri_f, g2) + _dot(tri_f, g3))
        cum_r = _dot(tri_r, g1) + (_dot(tri_r, g2) + _dot(tri_r, g3))
        col = jnp.where(lane < 2 * n_hv, beta[p * PAIR:(p + 1) * PAIR],
                        jnp.where(lane < 3 * n_hv, cum_f, cum_r))
        col_ref[0, p * PAIR:(p + 1) * PAIR, :] = col
        row_ref[0, p] = col.T


def _gdn_gates(raw, a_log_lanes, dt_bias_lanes, n_hv):
    n_batch, t_tot, _ = raw.shape
    tg = TOKEN_TILE
    return pl.pallas_call(
        functools.partial(_gates_body, n_hv=n_hv),
        grid=(n_batch, t_tot // tg),
        in_specs=[
            pl.BlockSpec((1, tg, LANES), lambda b, t: (b, t, 0)),
            pl.BlockSpec((1, LANES), lambda b, t: (0, 0)),
            pl.BlockSpec((1, LANES), lambda b, t: (0, 0)),
        ],
        out_specs=[
            pl.BlockSpec((1, tg, LANES), lambda b, t: (b, t, 0)),
            pl.BlockSpec((1, tg // PAIR, LANES, PAIR), lambda b, t: (b, t, 0, 0)),
        ],
        out_shape=[
            jax.ShapeDtypeStruct((n_batch, t_tot, LANES), F32),
            jax.ShapeDtypeStruct((n_batch, t_tot // PAIR, LANES, PAIR), F32),
        ],
        compiler_params=_params(("parallel", "parallel")),
        name="gdn_gates",
    )(raw, a_log_lanes, dt_bias_lanes)


CONV_HALO = 16
CONV_TILE = 256
HEADS_PER_STEP = 2


def _conv_tile(src_ref, w, r0, t_tot, ctx_len, conv_w):
    lo = pl.multiple_of(jnp.maximum(r0 - CONV_HALO, 0), CONV_HALO)
    hi = pl.multiple_of(jnp.minimum(r0 + CONV_TILE, t_tot - CONV_HALO), CONV_HALO)
    ext = jnp.concatenate([src_ref[0, pl.ds(lo, CONV_HALO), :],
                           src_ref[0, pl.ds(r0, CONV_TILE), :],
                           src_ref[0, pl.ds(hi, CONV_HALO), :]], axis=0).astype(F32)
    n_ext = CONV_TILE + 2 * CONV_HALO
    t = r0 + lax.broadcasted_iota(jnp.int32, (CONV_TILE, ext.shape[1]), 0)
    seg_lo = jnp.where(t < ctx_len, 0, ctx_len)
    seg_hi = jnp.where(t < ctx_len, ctx_len, t_tot)
    pad = conv_w // 2
    y = None
    for j in range(conv_w):
        off = j - pad
        xs = ext if off == 0 else pltpu.roll(ext, (-off) % n_ext, axis=0)
        xs = xs[CONV_HALO:CONV_HALO + CONV_TILE]
        if off != 0:
            src_t = t + off
            xs = jnp.where(jnp.logical_and(src_t >= seg_lo, src_t < seg_hi), xs, 0.0)
        term = xs * w[j:j + 1]
        y = term if y is None else y + term
    return _silu(y)


def _l2_norm(x):
    return x * lax.rsqrt(jnp.sum(x * x, axis=-1, keepdims=True) + EPS)


def _delta_body(q_ref, k_ref, v_ref, z_ref, cq_ref, ck_ref, cv_ref, colg_ref, rowg_ref, ng_ref,
                y_ref, q_s, k_s, v_s, o_s, st_s, cg_s, *, ctx_len, n_hv, conv_w, n_m):
    g_idx = pl.program_id(1)
    t_tot = q_ref.shape[1]
    dk = q_ref.shape[2] // n_m
    dv = v_ref.shape[2] // (2 * n_m)
    n_pairs = t_tot // PAIR
    n_ctx_pairs = ctx_len // PAIR

    lane_shift = (LANES - 2 * n_m * g_idx) % LANES

    def prep(i, carry):
        r0 = pl.multiple_of(i * CONV_TILE, CONV_TILE)
        rows = pl.ds(r0, CONV_TILE)
        q = _conv_tile(q_ref, cq_ref[...], r0, t_tot, ctx_len, conv_w)
        k = _conv_tile(k_ref, ck_ref[...], r0, t_tot, ctx_len, conv_w)
        for m in range(n_m):
            q_s[rows, m * dk:(m + 1) * dk] = _l2_norm(q[:, m * dk:(m + 1) * dk]) * (dk ** -0.5)
            k_s[rows, m * dk:(m + 1) * dk] = _l2_norm(k[:, m * dk:(m + 1) * dk])
        v_s[rows, :] = _conv_tile(v_ref, cv_ref[...], r0, t_tot, ctx_len, conv_w)
        cg_s[rows, :] = pltpu.roll(colg_ref[0, rows, :], lane_shift, axis=1)
        return carry

    lax.fori_loop(0, t_tot // CONV_TILE, prep, 0)
    o_s[...] = jnp.zeros_like(o_s)
    st_s[...] = jnp.zeros_like(st_s)

    r = lax.broadcasted_iota(jnp.int32, (CHUNK, LANES), 0)
    lane = lax.broadcasted_iota(jnp.int32, (CHUNK, LANES), 1)
    c = lane & (CHUNK - 1)
    left = lane < CHUNK
    left_row = lax.broadcasted_iota(jnp.int32, (1, LANES), 1) < CHUNK
    keep_left = jnp.where(left, 1.0, 0.0).astype(BF16)
    keep_right = jnp.where(left, 0.0, 1.0).astype(BF16)
    eye2 = jnp.where(c == r, 1.0, 0.0)
    n_lvl = CHUNK.bit_length() - 1
    lvl_mask = [jnp.logical_and((r >> (lg + 1)) == (c >> (lg + 1)), (r >> lg) != (c >> lg)) for lg in range(n_lvl)]
    incl = (c <= r, c >= r)
    strict = (c < r, c > r)
    zero_b = jnp.zeros((CHUNK, dv), BF16)

    def block_diag(y):
        return jnp.concatenate([y * keep_left, y * keep_right], axis=0)

    def step(i, carry):
        pr = jnp.where(i < n_ctx_pairs, n_ctx_pairs - 1 - i, n_pairs + n_ctx_pairs - 1 - i)
        units = []
        for m in range(n_m):
            units += [dict(m=m, d=0, p=i, half=0, seq=0), dict(m=m, d=0, p=i, half=1, seq=1),
                      dict(m=m, d=1, p=pr, half=1, seq=0), dict(m=m, d=1, p=pr, half=0, seq=1)]

        for u in units:
            m, d, half = u["m"], u["d"], u["half"]
            u["rows"] = pl.ds(pl.multiple_of(u["p"] * PAIR + half * CHUNK, CHUNK), CHUNK)
            q = q_s[u["rows"], m * dk:(m + 1) * dk]
            k = k_s[u["rows"], m * dk:(m + 1) * dk]
            u["q"], u["k"] = q, k
            qb, kb = q.astype(BF16), k.astype(BF16)
            cg = cg_s[u["rows"], :]
            b0, g0 = d * n_hv + 2 * m, (2 + d) * n_hv + 2 * m
            u["beta_c"] = (cg[:, b0:b0 + 1], cg[:, b0 + 1:b0 + 2])
            u["gc_c"] = (cg[:, g0:g0 + 1], cg[:, g0 + 1:g0 + 2])
            head0 = 2 * (n_m * g_idx + m)

            def packed_rows(base, half=half, p=u["p"], head0=head0):
                rows = rowg_ref[0, p, pl.ds(base + head0, 2), :]
                a, b = rows[0:1], rows[1:2]
                if half == 0:
                    return jnp.where(left_row, a, pltpu.roll(b, CHUNK, axis=1))
                return jnp.where(left_row, pltpu.roll(a, CHUNK, axis=1), b)

            u["beta_r"] = packed_rows(d * n_hv)
            u["gc_r"] = packed_rows((2 + d) * n_hv)
            u["gram"] = lax.dot_general(jnp.concatenate([kb, qb], axis=0), jnp.concatenate([kb, kb], axis=0),
                                        (((1,), (1,)), ((), ())), preferred_element_type=F32)

        for u in units:
            d = u["d"]
            gc_c2 = jnp.where(left, u["gc_c"][0], u["gc_c"][1])
            beta_c2 = jnp.where(left, u["beta_c"][0], u["beta_c"][1])
            decay = jnp.where(incl[d], jnp.exp(jnp.where(incl[d], gc_c2 - u["gc_r"], 0.0)), 0.0)
            a = jnp.where(strict[d], u["gram"][0:CHUNK] * decay, 0.0) * beta_c2
            u["qkd"] = (u["gram"][CHUNK:2 * CHUNK] * decay).astype(BF16)
            u["a"] = a
            u["x"] = eye2 - jnp.where(lvl_mask[0], a, 0.0)

        for lg in range(1, n_lvl):
            for u in units:
                u["xb"] = u["x"].astype(BF16)
                off = jnp.where(lvl_mask[lg], u["a"], 0.0).astype(BF16)
                u["z"] = _dot(off, block_diag(u["xb"]))
            for u in units:
                u["x"] = u["x"] - _dot(u["xb"], block_diag(u["z"].astype(BF16)))

        for u in units:
            m = u["m"]
            tb = (u["x"] * u["beta_r"]).astype(BF16)
            v2 = v_s[u["rows"], 2 * m * dv:2 * (m + 1) * dv]
            ke0 = (u["k"] * jnp.exp(u["gc_c"][0])).astype(BF16)
            ke1 = (u["k"] * jnp.exp(u["gc_c"][1])).astype(BF16)
            top = jnp.concatenate([v2[:, :dv].astype(BF16), zero_b, ke0, zero_b], axis=1)
            bot = jnp.concatenate([zero_b, v2[:, dv:].astype(BF16), zero_b, ke1], axis=1)
            u["uw"] = _dot(tb, jnp.concatenate([top, bot], axis=0))

        for seq in (0, 1):
            now = [u for u in units if u["seq"] == seq]
            for u in now:
                uw = u["uw"]
                u["s2"] = st_s[2 * u["m"] + u["d"]]
                lhs = jnp.concatenate([uw[:, 2 * dv:3 * dv], uw[:, 3 * dv:], u["q"]], axis=0).astype(BF16)
                u["ws"] = _dot(lhs, u["s2"].astype(BF16))
            for u in now:
                d, uw, ws = u["d"], u["uw"], u["ws"]
                vn0 = (uw[:, :dv] - ws[0:CHUNK, :dv]).astype(BF16)
                vn1 = (uw[:, dv:2 * dv] - ws[CHUNK:2 * CHUNK, dv:]).astype(BF16)
                vn_bd = jnp.concatenate([jnp.concatenate([vn0, zero_b], axis=1),
                                         jnp.concatenate([zero_b, vn1], axis=1)], axis=0)
                last = (CHUNK - 1, 2 * CHUNK - 1) if d == 0 else (0, CHUNK)
                gl = [u["gc_r"][:, l:l + 1] for l in last]
                u["gl"] = gl
                kd = jnp.concatenate([u["k"] * jnp.exp(gl[0] - u["gc_c"][0]),
                                      u["k"] * jnp.exp(gl[1] - u["gc_c"][1])], axis=0)
                lhs = jnp.concatenate([u["qkd"], kd.T.astype(BF16)], axis=0)
                u["ov"] = _dot(lhs, vn_bd)
            for u in now:
                m, ws, ov = u["m"], u["ws"], u["ov"]
                qs = jnp.concatenate([ws[2 * CHUNK:, :dv] * jnp.exp(u["gc_c"][0]),
                                      ws[2 * CHUNK:, dv:] * jnp.exp(u["gc_c"][1])], axis=1)
                o_s[u["rows"], 2 * m * dv:2 * (m + 1) * dv] += qs + ov[0:CHUNK]
                egl = jnp.concatenate([jnp.broadcast_to(jnp.exp(u["gl"][0]), (1, dv)),
                                       jnp.broadcast_to(jnp.exp(u["gl"][1]), (1, dv))], axis=1)
                st_s[2 * m + u["d"]] = u["s2"] * egl + ov[CHUNK:]
        return carry

    lax.fori_loop(0, n_pairs, step, 0)

    def finish(i, carry):
        rows = pl.ds(pl.multiple_of(i * CONV_TILE, CONV_TILE), CONV_TILE)
        for j in range(2 * n_m):
            o = o_s[rows, j * dv:(j + 1) * dv]
            z = z_ref[0, rows, j * dv:(j + 1) * dv].astype(F32)
            y = o * lax.rsqrt(jnp.mean(o * o, axis=-1, keepdims=True) + EPS) * ng_ref[...]
            y_ref[0, rows, j * dv:(j + 1) * dv] = (y * _silu(z)).astype(y_ref.dtype)
        return carry

    lax.fori_loop(0, t_tot // CONV_TILE, finish, 0)


def _gdn_delta(p_main, conv, colg, rowg, norm_g, ctx_len, n_hk, n_hv):
    n_batch, t_tot, _ = p_main.shape
    conv_w, qkv_w = conv.shape
    dk = GDN_DK
    dv = norm_g.shape[-1]
    assert n_hv == 2 * n_hk and dk == dv == LANES and n_hk % HEADS_PER_STEP == 0
    n_m = HEADS_PER_STEP
    qb = n_m * dk
    vb = 2 * n_m * dv
    k_blk0 = n_hk // n_m
    v_blk0 = (2 * n_hk * dk) // vb
    z_blk0 = qkv_w // vb
    n_pairs = t_tot // PAIR
    return pl.pallas_call(
        functools.partial(_delta_body, ctx_len=ctx_len, n_hv=n_hv, conv_w=conv_w, n_m=n_m),
        grid=(n_batch, n_hk // n_m),
        in_specs=[
            pl.BlockSpec((1, t_tot, qb), lambda b, h: (b, 0, h)),
            pl.BlockSpec((1, t_tot, qb), lambda b, h: (b, 0, k_blk0 + h)),
            pl.BlockSpec((1, t_tot, vb), lambda b, h: (b, 0, v_blk0 + h)),
            pl.BlockSpec((1, t_tot, vb), lambda b, h: (b, 0, z_blk0 + h)),
            pl.BlockSpec((conv_w, qb), lambda b, h: (0, h)),
            pl.BlockSpec((conv_w, qb), lambda b, h: (0, k_blk0 + h)),
            pl.BlockSpec((conv_w, vb), lambda b, h: (0, v_blk0 + h)),
            pl.BlockSpec((1, t_tot, LANES), lambda b, h: (b, 0, 0)),
            pl.BlockSpec((1, n_pairs, LANES, PAIR), lambda b, h: (b, 0, 0, 0)),
            pl.BlockSpec((1, dv), lambda b, h: (0, 0)),
        ],
        out_specs=pl.BlockSpec((1, t_tot, vb), lambda b, h: (b, 0, h)),
        out_shape=jax.ShapeDtypeStruct((n_batch, t_tot, n_hv * dv), BF16),
        scratch_shapes=[
            pltpu.VMEM((t_tot, qb), F32),
            pltpu.VMEM((t_tot, qb), F32),
            pltpu.VMEM((t_tot, vb), F32),
            pltpu.VMEM((t_tot, vb), F32),
            pltpu.VMEM((2 * n_m, dk, 2 * dv), F32),
            pltpu.VMEM((t_tot, LANES), F32),
        ],
        compiler_params=_params(("parallel", "parallel")),
        name="gdn_delta",
    )(p_main, p_main, p_main, p_main, conv, conv, conv, colg, rowg, norm_g)


def _rope(x, cos, sin_a, sin_b):
    quarter = ATT_DH // 4
    return x * cos + pltpu.roll(x, ATT_DH - quarter, axis=1) * sin_a + pltpu.roll(x, quarter, axis=1) * sin_b


def _rms_norm(x, g):
    return x * lax.rsqrt(jnp.mean(x * x, axis=-1, keepdims=True) + EPS) * g


def _attn_body(q_ref, k_ref, v_ref, cos_ref, sa_ref, sb_ref, qn_ref, kn_ref, o_ref, kt_s,
               *, ctx_len, n_group):
    qi = pl.program_id(2)
    tq = q_ref.shape[1]
    t_tot = k_ref.shape[1]
    dh = k_ref.shape[2]

    @pl.when(qi == 0)
    def _():
        k = _rms_norm(k_ref[0].astype(F32), kn_ref[...])
        k = _rope(k, cos_ref[...], sa_ref[...], sb_ref[...])
        kt_s[...] = k.T.astype(kt_s.dtype)

    r0 = pl.multiple_of(qi * tq, tq)
    cos = cos_ref[pl.ds(r0, tq), :]
    sin_a = sa_ref[pl.ds(r0, tq), :]
    sin_b = sb_ref[pl.ds(r0, tq), :]
    limit = jnp.where(r0 < ctx_len, ctx_len, t_tot)
    visible = lax.broadcasted_iota(jnp.int32, (tq, t_tot), 1) < limit
    v = v_ref[0]
    for g in range(n_group):
        q = _rms_norm(q_ref[0, :, g * dh:(g + 1) * dh].astype(F32), qn_ref[...])
        q = _rope(q, cos, sin_a, sin_b) * (dh ** -0.5)
        s = jnp.where(visible, _dot(q.astype(BF16), kt_s[...]), NEG_INF)
        p = jnp.exp(s - jnp.max(s, axis=-1, keepdims=True))
        l = jnp.sum(p, axis=-1, keepdims=True)
        o_ref[0, :, g * dh:(g + 1) * dh] = (_dot(p.astype(BF16), v) / l).astype(o_ref.dtype)


def _attention(p_qkv, tables, q_norm, k_norm, ctx_len, n_hq, n_hkv):
    n_batch, t_tot, _ = p_qkv.shape
    dh = ATT_DH
    n_group = n_hq // n_hkv
    tq = TOKEN_TILE
    cos, sin_a, sin_b = tables
    full = lambda shape: pl.BlockSpec(shape, lambda b, h, q: (0,) * len(shape))
    return pl.pallas_call(
        functools.partial(_attn_body, ctx_len=ctx_len, n_group=n_group),
        grid=(n_batch, n_hkv, t_tot // tq),
        in_specs=[
            pl.BlockSpec((1, tq, n_group * dh), lambda b, h, q: (b, q, h)),
            pl.BlockSpec((1, t_tot, dh), lambda b, h, q: (b, 0, n_hq + h)),
            pl.BlockSpec((1, t_tot, dh), lambda b, h, q: (b, 0, n_hq + n_hkv + h)),
            full((t_tot, dh)), full((t_tot, dh)), full((t_tot, dh)),
            full((1, dh)), full((1, dh)),
        ],
        out_specs=pl.BlockSpec((1, tq, n_group * dh), lambda b, h, q: (b, q, h)),
        out_shape=jax.ShapeDtypeStruct((n_batch, t_tot, n_hq * dh), BF16),
        scratch_shapes=[pltpu.VMEM((dh, t_tot), BF16)],
        compiler_params=_params(("parallel", "parallel", "arbitrary")),
        name="attention",
    )(p_qkv, p_qkv, p_qkv, cos, sin_a, sin_b, q_norm, k_norm)


def _rope_tables(ctx_len, seq_len):
    rows = seq_len // GRID_W
    row = jnp.repeat(jnp.arange(rows), GRID_W).astype(F32)
    col = jnp.tile(jnp.arange(GRID_W), rows).astype(F32)
    n_freq = ATT_DH // 4
    freqs = ROPE_THETA ** (-jnp.arange(n_freq, dtype=F32) / n_freq)
    ang_r = row[:, None] * freqs
    ang_c = col[:, None] * freqs
    ang = jnp.concatenate([ang_r, ang_r, ang_c, ang_c], axis=-1)
    cos = jnp.concatenate([jnp.ones((ctx_len, ATT_DH), F32), jnp.cos(ang)], axis=0)
    sin = jnp.concatenate([jnp.zeros((ctx_len, ATT_DH), F32), jnp.sin(ang)], axis=0)
    first_half = (jnp.arange(ATT_DH) % (2 * n_freq)) < n_freq
    return cos, jnp.where(first_half, -sin, 0.0), jnp.where(first_half, 0.0, sin)


def _mix_out_body(y_ref, w_ref, x_ref, mod_ref, lng_ref, lnb_ref, o_ref, *, alpha):
    delta = _dot(y_ref[0], w_ref[...])
    r = alpha * x_ref[0] + mod_ref[0, 2:3, :] * delta
    o_ref[0] = _layer_norm(r, lng_ref[...], lnb_ref[...])


def _mix_out(y, w, xs, mods, ln_g, ln_b, n_ctx_tiles, alpha):
    n_batch, t_tot, d = xs.shape
    k = y.shape[-1]
    tm = TOKEN_TILE
    return pl.pallas_call(
        functools.partial(_mix_out_body, alpha=alpha),
        grid=(n_batch, t_tot // tm),
        in_specs=[
            pl.BlockSpec((1, tm, k), lambda b, t: (b, t, 0)),
            pl.BlockSpec((k, d), lambda b, t: (0, 0)),
            pl.BlockSpec((1, tm, d), lambda b, t: (b, t, 0)),
            pl.BlockSpec((1, N_MOD, d), _mod_row_map(n_batch, n_ctx_tiles)),
            pl.BlockSpec((1, d), lambda b, t: (0, 0)),
            pl.BlockSpec((1, d), lambda b, t: (0, 0)),
        ],
        out_specs=pl.BlockSpec((1, tm, d), lambda b, t: (b, t, 0)),
        out_shape=jax.ShapeDtypeStruct(xs.shape, F32),
        compiler_params=_params(("parallel", "parallel")),
        name="mix_out",
    )(y, w, xs, mods, ln_g, ln_b)


def _ffn_body(x_ref, mod_ref, win_ref, wout_ref, lng_ref, lnb_ref, o_ref, *, n_chunk, alpha):
    x = x_ref[0]
    hb = (x * (1.0 + mod_ref[0, 4:5, :]) + mod_ref[0, 3:4, :]).astype(BF16)
    d_ff = wout_ref.shape[0]
    tc = d_ff // n_chunk
    acc = None
    for c in range(n_chunk):
        gate = _dot(hb, win_ref[:, c * tc:(c + 1) * tc])
        up = _dot(hb, win_ref[:, d_ff + c * tc:d_ff + (c + 1) * tc])
        part = _dot((_silu(gate) * up).astype(BF16), wout_ref[c * tc:(c + 1) * tc, :])
        acc = part if acc is None else acc + part
    r = alpha * x + mod_ref[0, 5:6, :] * acc
    o_ref[0] = _layer_norm(r, lng_ref[...], lnb_ref[...])


def _ffn(xs, mods, w_in, w_out, ln_g, ln_b, n_ctx_tiles, alpha):
    n_batch, t_tot, d = xs.shape
    d_ff = w_out.shape[0]
    tm = TOKEN_TILE
    n_chunk = 2 if d_ff % (2 * LANES) == 0 else 1
    return pl.pallas_call(
        functools.partial(_ffn_body, n_chunk=n_chunk, alpha=alpha),
        grid=(n_batch, t_tot // tm),
        in_specs=[
            pl.BlockSpec((1, tm, d), lambda b, t: (b, t, 0)),
            pl.BlockSpec((1, N_MOD, d), _mod_row_map(n_batch, n_ctx_tiles)),
            pl.BlockSpec((d, 2 * d_ff), lambda b, t: (0, 0)),
            pl.BlockSpec((d_ff, d), lambda b, t: (0, 0)),
            pl.BlockSpec((1, d), lambda b, t: (0, 0)),
            pl.BlockSpec((1, d), lambda b, t: (0, 0)),
        ],
        out_specs=pl.BlockSpec((1, tm, d), lambda b, t: (b, t, 0)),
        out_shape=jax.ShapeDtypeStruct(xs.shape, F32),
        compiler_params=_params(("parallel", "parallel")),
        name="ffn",
    )(xs, mods, w_in, w_out, ln_g, ln_b)


def kernel(x, c, ctx, c_ctx, w_mod, b_mod, ln_g, ln_b, w_ffn_in, w_ffn_out, gdn_w_in, gdn_conv, gdn_a_log,
           gdn_dt_bias, gdn_norm_g, gdn_w_out, attn_w_qkv, attn_q_norm, attn_k_norm, attn_w_out):
    n_batch, seq_len, d = x.shape
    ctx_len = ctx.shape[1]
    depth = w_mod.shape[0]
    alpha = (2 * depth) ** 0.25
    n_hv = gdn_a_log.shape[-1]
    dv = gdn_norm_g.shape[-1]
    v_w = n_hv * dv
    qkv_w = gdn_conv.shape[-1]
    n_hk = (qkv_w - v_w) // (2 * GDN_DK)
    n_hq = attn_w_out.shape[1] // ATT_DH
    n_hkv = (attn_w_qkv.shape[-1] // ATT_DH - n_hq) // 2
    assert ctx_len % TOKEN_TILE == 0 and seq_len % TOKEN_TILE == 0 and seq_len % GRID_W == 0
    assert 4 * n_hv <= LANES and d % LANES == 0
    n_ctx_tiles = ctx_len // TOKEN_TILE

    xs = jnp.concatenate([ctx, x], axis=1)

    mp = -(-(n_batch + 1) // SUBLANES) * SUBLANES
    cond = jnp.concatenate([c, c_ctx[None, :], jnp.zeros((mp - n_batch - 1, d), F32)], axis=0)
    mods = _modulation(cond, w_mod, b_mod).reshape(depth, mp, N_MOD, d)
    tables = _rope_tables(ctx_len, seq_len)

    for i in range(depth):
        j = i // 2
        m = mods[i]
        if i % 2 == 0:
            w_in = gdn_w_in[j]
            gate_w = jnp.pad(w_in[:, qkv_w + v_w:], ((0, 0), (0, LANES - 4 * n_hv)))
            p_main, raw = _in_proj(xs, m, w_in[:, :qkv_w + v_w].astype(BF16), gate_w, n_ctx_tiles, n_chunk=6)
            lanes_pad = (0, LANES - 4 * n_hv)
            a_log = jnp.pad(jnp.concatenate([jnp.zeros((2 * n_hv,), F32), gdn_a_log[j].reshape(-1)]), lanes_pad)
            dt_b = jnp.pad(jnp.concatenate([jnp.zeros((2 * n_hv,), F32), gdn_dt_bias[j].reshape(-1)]), lanes_pad)
            colg, rowg = _gdn_gates(raw, a_log[None, :], dt_b[None, :], n_hv)
            y = _gdn_delta(p_main, gdn_conv[j], colg, rowg, gdn_norm_g[j][None, :], ctx_len, n_hk, n_hv)
            w_o = gdn_w_out[j]
        else:
            (p_qkv,) = _in_proj(xs, m, attn_w_qkv[j].astype(BF16), None, n_ctx_tiles, n_chunk=3)
            y = _attention(p_qkv, tables, attn_q_norm[j][None, :], attn_k_norm[j][None, :], ctx_len, n_hq, n_hkv)
            w_o = attn_w_out[j]
        xs = _mix_out(y, w_o.astype(BF16), xs, m, ln_g[i, 0][None, :], ln_b[i, 0][None, :], n_ctx_tiles, alpha)
        xs = _ffn(xs, m, w_ffn_in[i].astype(BF16), w_ffn_out[i].astype(BF16),
                  ln_g[i, 1][None, :], ln_b[i, 1][None, :], n_ctx_tiles, alpha)
    return xs[:, ctx_len:]
```

```python
import functools
import math

import jax
import jax.numpy as jnp
from jax import lax
from jax.experimental import pallas as pl
from jax.experimental.pallas import tpu as pltpu

F32 = jnp.float32
BF16 = jnp.bfloat16

N_MOD = 6
EPS = 1e-6
GDN_DK = 128
CHUNK = 64
ATT_DH = 128
GRID_W = 64
ROPE_THETA = 10000.0

LANES = 128
SUBLANES = 8
PAIR = 2 * CHUNK
VMEM_LIMIT_BYTES = 56 * 1024 * 1024
TOKEN_TILE = 256
NEG_INF = -1e30
LOG2E = math.log2(math.e)

assert PAIR == LANES


def _params(sem):
    return pltpu.CompilerParams(dimension_semantics=sem, vmem_limit_bytes=VMEM_LIMIT_BYTES)


def _dot(a, b):
    return jnp.dot(a, b, preferred_element_type=F32)


def _split2(a):
    hi = a.astype(BF16)
    return hi, (a - hi.astype(F32)).astype(BF16)


def _dot_hi(a, b):
    a1, a2 = _split2(a)
    b1, b2 = _split2(b)
    return _dot(a1, b1) + (_dot(a1, b2) + _dot(a2, b1))


def _sigmoid(x):
    return 1.0 / (1.0 + jnp.exp(-x))


def _silu(x):
    return x * _sigmoid(x)


def _layer_norm(r, g, b):
    mu = jnp.mean(r, axis=-1, keepdims=True)
    rc = r - mu
    var = jnp.mean(rc * rc, axis=-1, keepdims=True)
    return rc * lax.rsqrt(var + EPS) * g + b


def _mod_body(cond_ref, w_ref, b_ref, o_ref):
    o_ref[0] = _dot_hi(_silu(cond_ref[...]), w_ref[0]) + b_ref[0]


def _modulation(cond, w_mod, b_mod):
    n_layer, d, n = w_mod.shape
    mp = cond.shape[0]
    tn = n // 4
    return pl.pallas_call(
        _mod_body,
        grid=(n_layer, n // tn),
        in_specs=[
            pl.BlockSpec((mp, d), lambda l, j: (0, 0)),
            pl.BlockSpec((1, d, tn), lambda l, j: (l, 0, j)),
            pl.BlockSpec((1, 1, tn), lambda l, j: (l, 0, j)),
        ],
        out_specs=pl.BlockSpec((1, mp, tn), lambda l, j: (l, 0, j)),
        out_shape=jax.ShapeDtypeStruct((n_layer, mp, n), F32),
        compiler_params=_params(("parallel", "parallel")),
        name="modulation",
    )(cond, w_mod, b_mod.reshape(n_layer, 1, n))


def _mod_row_map(n_batch, n_ctx_tiles):
    return lambda b, t: (jnp.where(t < n_ctx_tiles, n_batch, b), 0, 0)


def _in_proj_body(x_ref, mod_ref, w_ref, *rest, n_chunk, has_extra):
    if has_extra:
        wx_ref, o_ref, ox_ref = rest
    else:
        (o_ref,) = rest
    h = x_ref[0] * (1.0 + mod_ref[0, 1:2, :]) + mod_ref[0, 0:1, :]
    hb = h.astype(BF16)
    tn = w_ref.shape[1] // n_chunk
    for j in range(n_chunk):
        o_ref[0, :, j * tn:(j + 1) * tn] = _dot(hb, w_ref[:, j * tn:(j + 1) * tn]).astype(o_ref.dtype)
    if has_extra:
        ox_ref[0] = _dot_hi(h, wx_ref[...])


def _in_proj(xs, mods, w, w_extra, n_ctx_tiles, n_chunk):
    n_batch, t_tot, d = xs.shape
    n = w.shape[1]
    tm = TOKEN_TILE
    has_extra = w_extra is not None
    in_specs = [
        pl.BlockSpec((1, tm, d), lambda b, t: (b, t, 0)),
        pl.BlockSpec((1, N_MOD, d), _mod_row_map(n_batch, n_ctx_tiles)),
        pl.BlockSpec((d, n), lambda b, t: (0, 0)),
    ]
    out_specs = [pl.BlockSpec((1, tm, n), lambda b, t: (b, t, 0))]
    out_shape = [jax.ShapeDtypeStruct((n_batch, t_tot, n), BF16)]
    args = [xs, mods, w]
    if has_extra:
        in_specs.append(pl.BlockSpec((d, LANES), lambda b, t: (0, 0)))
        out_specs.append(pl.BlockSpec((1, tm, LANES), lambda b, t: (b, t, 0)))
        out_shape.append(jax.ShapeDtypeStruct((n_batch, t_tot, LANES), F32))
        args.append(w_extra)
    return pl.pallas_call(
        functools.partial(_in_proj_body, n_chunk=n_chunk, has_extra=has_extra),
        grid=(n_batch, t_tot // tm),
        in_specs=in_specs,
        out_specs=out_specs,
        out_shape=out_shape,
        compiler_params=_params(("parallel", "parallel")),
        name="in_proj",
    )(*args)


def _gates_body(raw_ref, alog_ref, dtb_ref, col_ref, row_ref, *, n_hv):
    raw = raw_ref[0]
    beta = _sigmoid(raw)
    xs = raw + dtb_ref[...]
    g = -jnp.exp(alog_ref[...]) * (jnp.maximum(xs, 0.0) + jnp.log1p(jnp.exp(-jnp.abs(xs))))
    r = lax.broadcasted_iota(jnp.int32, (PAIR, PAIR), 0)
    c = lax.broadcasted_iota(jnp.int32, (PAIR, PAIR), 1)
    same = (r < CHUNK) == (c < CHUNK)
    tri_f = jnp.where(jnp.logical_and(same, c <= r), 1.0, 0.0).astype(BF16)
    tri_r = jnp.where(jnp.logical_and(same, c >= r), 1.0, 0.0).astype(BF16)
    lane = lax.broadcasted_iota(jnp.int32, (PAIR, LANES), 1)
    for p in range(raw.shape[0] // PAIR):
        gp = g[p * PAIR:(p + 1) * PAIR]
        g1 = gp.astype(BF16)
        r1 = gp - g1.astype(F32)
        g2 = r1.astype(BF16)
        g3 = (r1 - g2.astype(F32)).astype(BF16)
        cum_f = _dot(tri_f, g1) + (_dot(tri_f, g2) + _dot(tri_f, g3))
        cum_r = _dot(tri_r, g1) + (_dot(tri_r, g2) + _dot(tri_r, g3))
        col = jnp.where(lane < 2 * n_hv, beta[p * PAIR:(p + 1) * PAIR],
                        jnp.where(lane < 3 * n_hv, cum_f, cum_r))
        col_ref[0, p * PAIR:(p + 1) * PAIR, :] = col
        row_ref[0, p] = col.T


def _gdn_gates(raw, a_log_lanes, dt_bias_lanes, n_hv):
    n_batch, t_tot, _ = raw.shape
    tg = TOKEN_TILE
    return pl.pallas_call(
        functools.partial(_gates_body, n_hv=n_hv),
        grid=(n_batch, t_tot // tg),
        in_specs=[
            pl.BlockSpec((1, tg, LANES), lambda b, t: (b, t, 0)),
            pl.BlockSpec((1, LANES), lambda b, t: (0, 0)),
            pl.BlockSpec((1, LANES), lambda b, t: (0, 0)),
        ],
        out_specs=[
            pl.BlockSpec((1, tg, LANES), lambda b, t: (b, t, 0)),
            pl.BlockSpec((1, tg // PAIR, LANES, PAIR), lambda b, t: (b, t, 0, 0)),
        ],
        out_shape=[
            jax.ShapeDtypeStruct((n_batch, t_tot, LANES), F32),
            jax.ShapeDtypeStruct((n_batch, t_tot // PAIR, LANES, PAIR), F32),
        ],
        compiler_params=_params(("parallel", "parallel")),
        name="gdn_gates",
    )(raw, a_log_lanes, dt_bias_lanes)


CONV_HALO = 16
CONV_TILE = 256
HEADS_PER_STEP = 2
CHUNKS_PER_STEP = 4
SUPER = CHUNKS_PER_STEP * CHUNK


def _conv_tile(src_ref, w, r0, t_tot, ctx_len, conv_w):
    lo = pl.multiple_of(jnp.maximum(r0 - CONV_HALO, 0), CONV_HALO)
    hi = pl.multiple_of(jnp.minimum(r0 + CONV_TILE, t_tot - CONV_HALO), CONV_HALO)
    keep_lo = jnp.where(jnp.logical_or(r0 == 0, r0 == ctx_len), 0.0, 1.0)
    keep_hi = jnp.where(jnp.logical_or(r0 + CONV_TILE == ctx_len, r0 + CONV_TILE == t_tot), 0.0, 1.0)
    ext = jnp.concatenate([src_ref[0, pl.ds(lo, CONV_HALO), :].astype(F32) * keep_lo,
                           src_ref[0, pl.ds(r0, CONV_TILE), :].astype(F32),
                           src_ref[0, pl.ds(hi, CONV_HALO), :].astype(F32) * keep_hi], axis=0)
    n_ext = CONV_TILE + 2 * CONV_HALO
    pad = conv_w // 2
    y = None
    for j in range(conv_w):
        off = j - pad
        xs = ext if off == 0 else pltpu.roll(ext, (-off) % n_ext, axis=0)
        term = xs[CONV_HALO:CONV_HALO + CONV_TILE] * w[j:j + 1]
        y = term if y is None else y + term
    return _silu(y)


def _l2_norm(x):
    return x * lax.rsqrt(jnp.sum(x * x, axis=-1, keepdims=True) + EPS)


def _delta_body(q_ref, k_ref, v_ref, z_ref, cq_ref, ck_ref, cv_ref, colg_ref, rowg_ref, ng_ref,
                y_ref, q_s, k_s, v_s, o_s, st_s, cg_s, uw_s, qkd_s, kt2_s, *, ctx_len, n_hv, conv_w, n_m):
    g_idx = pl.program_id(1)
    t_tot = q_ref.shape[1]
    dk = q_ref.shape[2] // n_m
    dv = v_ref.shape[2] // (2 * n_m)
    n_sup = t_tot // SUPER
    n_ctx_sup = ctx_len // SUPER

    lane_shift = (LANES - 2 * n_m * g_idx) % LANES

    def prep(i, carry):
        r0 = pl.multiple_of(i * CONV_TILE, CONV_TILE)
        rows = pl.ds(r0, CONV_TILE)
        q = _conv_tile(q_ref, cq_ref[...], r0, t_tot, ctx_len, conv_w)
        k = _conv_tile(k_ref, ck_ref[...], r0, t_tot, ctx_len, conv_w)
        for m in range(n_m):
            q_s[rows, m * dk:(m + 1) * dk] = (_l2_norm(q[:, m * dk:(m + 1) * dk]) * (dk ** -0.5)).astype(q_s.dtype)
            km = _l2_norm(k[:, m * dk:(m + 1) * dk])
            k_s[rows, m * dk:(m + 1) * dk] = km.astype(k_s.dtype)
            for cc in range(CONV_TILE // CHUNK):
                kc = km[cc * CHUNK:(cc + 1) * CHUNK]
                kt2_s[i * (CONV_TILE // CHUNK) + cc, m] = jnp.concatenate([kc, kc], axis=0).T
        v_s[rows, :] = _conv_tile(v_ref, cv_ref[...], r0, t_tot, ctx_len, conv_w).astype(v_s.dtype)
        cg_s[rows, :] = pltpu.roll(colg_ref[0, rows, :], lane_shift, axis=1)
        return carry

    lax.fori_loop(0, t_tot // CONV_TILE, prep, 0)
    o_s[...] = jnp.zeros_like(o_s)
    st_s[...] = jnp.zeros_like(st_s)
    uw_s[...] = jnp.zeros_like(uw_s)
    qkd_s[...] = jnp.zeros_like(qkd_s)

    r = lax.broadcasted_iota(jnp.int32, (CHUNK, LANES), 0)
    lane = lax.broadcasted_iota(jnp.int32, (CHUNK, LANES), 1)
    c = lane & (CHUNK - 1)
    left = lane < CHUNK
    left_row = lax.broadcasted_iota(jnp.int32, (1, LANES), 1) < CHUNK
    keep_left = jnp.where(left, 1.0, 0.0).astype(BF16)
    keep_right = jnp.where(left, 0.0, 1.0).astype(BF16)
    eye2 = jnp.where(c == r, 1.0, 0.0)
    n_lvl = CHUNK.bit_length() - 1
    lvl_mask = [jnp.logical_and((r >> (lg + 1)) == (c >> (lg + 1)), (r >> lg) != (c >> lg)) for lg in range(n_lvl)]
    incl = (c <= r, c >= r)
    strict = (c < r, c > r)
    zero_b = jnp.zeros((CHUNK, dv), BF16)

    def block_diag(y):
        return jnp.concatenate([y * keep_left, y * keep_right], axis=0)

    def reverse_super(i):
        return jnp.where(i < n_ctx_sup, n_ctx_sup - 1 - i, n_sup + n_ctx_sup - 1 - i)

    def make_units(i):
        sup = (i, reverse_super(i))
        units = []
        for m in range(n_m):
            for d in (0, 1):
                for seq in range(CHUNKS_PER_STEP):
                    cidx = seq if d == 0 else CHUNKS_PER_STEP - 1 - seq
                    units.append(dict(
                        m=m, d=d, seq=seq, half=cidx % 2, uid=(m * 2 + d) * CHUNKS_PER_STEP + seq,
                        pair=sup[d] * (SUPER // PAIR) + cidx // 2, chunk=sup[d] * CHUNKS_PER_STEP + cidx,
                        rows=pl.ds(pl.multiple_of(sup[d] * SUPER + cidx * CHUNK, CHUNK), CHUNK)))
        return units

    def load_gates(u):
        m, d, half = u["m"], u["d"], u["half"]
        cg = cg_s[u["rows"], :]
        b0, g0 = d * n_hv + 2 * m, (2 + d) * n_hv + 2 * m
        u["beta_c"] = (cg[:, b0:b0 + 1], cg[:, b0 + 1:b0 + 2])
        u["gc_c"] = (cg[:, g0:g0 + 1], cg[:, g0 + 1:g0 + 2])
        head0 = 2 * (n_m * g_idx + m)

        def packed_rows(base):
            rows = rowg_ref[0, u["pair"], pl.ds(base + head0, 2), :]
            a, b = rows[0:1], rows[1:2]
            if half == 0:
                return jnp.where(left_row, a, pltpu.roll(b, CHUNK, axis=1))
            return jnp.where(left_row, pltpu.roll(a, CHUNK, axis=1), b)

        u["beta_r"] = packed_rows(d * n_hv)
        u["gc_r"] = packed_rows((2 + d) * n_hv)

    def state_free_part(i, slot):
        units = make_units(i)
        for u in units:
            m = u["m"]
            load_gates(u)
            qb, kb = q_s[u["rows"], m * dk:(m + 1) * dk], k_s[u["rows"], m * dk:(m + 1) * dk]
            u["kb"] = kb
            u["gram"] = lax.dot_general(jnp.concatenate([kb, qb], axis=0), jnp.concatenate([kb, kb], axis=0),
                                        (((1,), (1,)), ((), ())), preferred_element_type=F32)
        yield
        for u in units:
            d = u["d"]
            gc_c2 = jnp.where(left, u["gc_c"][0], u["gc_c"][1])
            beta_c2 = jnp.where(left, u["beta_c"][0], u["beta_c"][1])
            decay = jnp.where(incl[d], jnp.exp(jnp.where(incl[d], gc_c2 - u["gc_r"], 0.0)), 0.0)
            a = jnp.where(strict[d], u["gram"][0:CHUNK] * decay, 0.0) * beta_c2
            qkd_s[slot, u["uid"]] = (u["gram"][CHUNK:2 * CHUNK] * decay).astype(BF16)
            u["a"] = a
            u["x"] = eye2 - jnp.where(lvl_mask[0], a, 0.0)
        yield
        for lg in range(1, n_lvl):
            for u in units:
                u["xb"] = u["x"].astype(BF16)
                off = jnp.where(lvl_mask[lg], u["a"], 0.0).astype(BF16)
                u["z"] = _dot(off, block_diag(u["xb"]))
            yield
            for u in units:
                u["x"] = u["x"] - _dot(u["xb"], block_diag(u["z"].astype(BF16)))
            yield
        for u in units:
            m = u["m"]
            tb = u["x"] * u["beta_r"]
            tbe = tb * jnp.exp(u["gc_r"])
            v2 = v_s[u["rows"], 2 * m * dv:2 * (m + 1) * dv]
            v_bd = jnp.concatenate([jnp.concatenate([v2[:, :dv], zero_b], axis=1),
                                    jnp.concatenate([zero_b, v2[:, dv:]], axis=1)], axis=0)
            k_bd = jnp.concatenate([jnp.concatenate([u["kb"], zero_b], axis=1),
                                    jnp.concatenate([zero_b, u["kb"]], axis=1)], axis=0)
            uw_s[slot, u["uid"], :, :2 * dv] = _dot(tb.astype(BF16), v_bd)
            uw_s[slot, u["uid"], :, 2 * dv:] = _dot(tbe.astype(BF16), k_bd)
        yield

    def recurrence_part(i, slot):
        units = make_units(i)
        for seq in range(CHUNKS_PER_STEP):
            now = [u for u in units if u["seq"] == seq]
            for u in now:
                m = u["m"]
                load_gates(u)
                uw = uw_s[slot, u["uid"]]
                u["u"] = uw[:, :2 * dv]
                u["s2"] = st_s[2 * m + u["d"]]
                lhs = jnp.concatenate([uw[:, 2 * dv:3 * dv].astype(BF16), uw[:, 3 * dv:].astype(BF16),
                                       q_s[u["rows"], m * dk:(m + 1) * dk]], axis=0)
                u["ws"] = _dot(lhs, u["s2"].astype(BF16))
            yield
            for u in now:
                d, ws = u["d"], u["ws"]
                vn0 = (u["u"][:, :dv] - ws[0:CHUNK, :dv]).astype(BF16)
                vn1 = (u["u"][:, dv:] - ws[CHUNK:2 * CHUNK, dv:]).astype(BF16)
                vn_bd = jnp.concatenate([jnp.concatenate([vn0, zero_b], axis=1),
                                         jnp.concatenate([zero_b, vn1], axis=1)], axis=0)
                last = (CHUNK - 1, 2 * CHUNK - 1) if d == 0 else (0, CHUNK)
                u["gl"] = [u["gc_r"][:, l:l + 1] for l in last]
                gl2 = jnp.where(left_row, u["gl"][0], u["gl"][1])
                kdt = kt2_s[u["chunk"], u["m"]] * jnp.exp(gl2 - u["gc_r"])
                lhs = jnp.concatenate([qkd_s[slot, u["uid"]], kdt.astype(BF16)], axis=0)
                u["ov"] = _dot(lhs, vn_bd)
            yield
            for u in now:
                m, ws, ov = u["m"], u["ws"], u["ov"]
                qs = jnp.concatenate([ws[2 * CHUNK:, :dv] * jnp.exp(u["gc_c"][0]),
                                      ws[2 * CHUNK:, dv:] * jnp.exp(u["gc_c"][1])], axis=1)
                o_s[u["rows"], 2 * m * dv:2 * (m + 1) * dv] += qs + ov[0:CHUNK]
                egl = jnp.concatenate([jnp.broadcast_to(jnp.exp(u["gl"][0]), (1, dv)),
                                       jnp.broadcast_to(jnp.exp(u["gl"][1]), (1, dv))], axis=1)
                st_s[2 * m + u["d"]] = u["s2"] * egl + ov[CHUNK:]
            yield

    def step(i, carry):
        slot = i & 1
        ahead = state_free_part(jnp.minimum(i, n_sup - 1), slot)
        behind = recurrence_part(jnp.maximum(i - 1, 0), 1 - slot)
        live = [ahead, behind]
        while live:
            live = [g for g in live if next(g, "done") != "done"]
        return carry

    lax.fori_loop(0, n_sup + 1, step, 0)

    def finish(i, carry):
        rows = pl.ds(pl.multiple_of(i * CONV_TILE, CONV_TILE), CONV_TILE)
        for j in range(2 * n_m):
            o = o_s[rows, j * dv:(j + 1) * dv]
            z = z_ref[0, rows, j * dv:(j + 1) * dv].astype(F32)
            y = o * lax.rsqrt(jnp.mean(o * o, axis=-1, keepdims=True) + EPS) * ng_ref[...]
            y_ref[0, rows, j * dv:(j + 1) * dv] = (y * _silu(z)).astype(y_ref.dtype)
        return carry

    lax.fori_loop(0, t_tot // CONV_TILE, finish, 0)


def _gdn_delta(p_main, conv, colg, rowg, norm_g, ctx_len, n_hk, n_hv):
    n_batch, t_tot, _ = p_main.shape
    conv_w, qkv_w = conv.shape
    dk = GDN_DK
    dv = norm_g.shape[-1]
    assert n_hv == 2 * n_hk and dk == dv == LANES and n_hk % HEADS_PER_STEP == 0
    assert ctx_len % SUPER == 0 and t_tot % SUPER == 0
    n_m = HEADS_PER_STEP
    qb = n_m * dk
    vb = 2 * n_m * dv
    k_blk0 = n_hk // n_m
    v_blk0 = (2 * n_hk * dk) // vb
    z_blk0 = qkv_w // vb
    n_pairs = t_tot // PAIR
    n_units = 2 * n_m * CHUNKS_PER_STEP
    return pl.pallas_call(
        functools.partial(_delta_body, ctx_len=ctx_len, n_hv=n_hv, conv_w=conv_w, n_m=n_m),
        grid=(n_batch, n_hk // n_m),
        in_specs=[
            pl.BlockSpec((1, t_tot, qb), lambda b, h: (b, 0, h)),
            pl.BlockSpec((1, t_tot, qb), lambda b, h: (b, 0, k_blk0 + h)),
            pl.BlockSpec((1, t_tot, vb), lambda b, h: (b, 0, v_blk0 + h)),
            pl.BlockSpec((1, t_tot, vb), lambda b, h: (b, 0, z_blk0 + h)),
            pl.BlockSpec((conv_w, qb), lambda b, h: (0, h)),
            pl.BlockSpec((conv_w, qb), lambda b, h: (0, k_blk0 + h)),
            pl.BlockSpec((conv_w, vb), lambda b, h: (0, v_blk0 + h)),
            pl.BlockSpec((1, t_tot, LANES), lambda b, h: (b, 0, 0)),
            pl.BlockSpec((1, n_pairs, LANES, PAIR), lambda b, h: (b, 0, 0, 0)),
            pl.BlockSpec((1, dv), lambda b, h: (0, 0)),
        ],
        out_specs=pl.BlockSpec((1, t_tot, vb), lambda b, h: (b, 0, h)),
        out_shape=jax.ShapeDtypeStruct((n_batch, t_tot, n_hv * dv), BF16),
        scratch_shapes=[
            pltpu.VMEM((t_tot, qb), BF16),
            pltpu.VMEM((t_tot, qb), BF16),
            pltpu.VMEM((t_tot, vb), BF16),
            pltpu.VMEM((t_tot, vb), F32),
            pltpu.VMEM((2 * n_m, dk, 2 * dv), F32),
            pltpu.VMEM((t_tot, LANES), F32),
            pltpu.VMEM((2, n_units, CHUNK, 2 * dv + 2 * dk), F32),
            pltpu.VMEM((2, n_units, CHUNK, 2 * CHUNK), BF16),
            pltpu.VMEM((t_tot // CHUNK, n_m, dk, 2 * CHUNK), F32),
        ],
        compiler_params=_params(("parallel", "parallel")),
        name="gdn_delta",
    )(p_main, p_main, p_main, p_main, conv, conv, conv, colg, rowg, norm_g)


def _rope(x, cos, sin_a, sin_b):
    quarter = ATT_DH // 4
    return x * cos + pltpu.roll(x, ATT_DH - quarter, axis=1) * sin_a + pltpu.roll(x, quarter, axis=1) * sin_b


def _rms_norm(x, g):
    return x * lax.rsqrt(jnp.mean(x * x, axis=-1, keepdims=True) + EPS) * g


def _attn_body(q_ref, k_ref, v_ref, cos_ref, sa_ref, sb_ref, qn_ref, kn_ref, o_ref, kt_s, v1_s,
               *, ctx_len, n_group):
    qi = pl.program_id(2)
    tq = q_ref.shape[1]
    t_tot = k_ref.shape[1]
    dh = k_ref.shape[2]

    @pl.when(qi == 0)
    def _():
        k = _rms_norm(k_ref[0].astype(F32), kn_ref[...])
        k = _rope(k, cos_ref[...], sa_ref[...], sb_ref[...])
        kt_s[...] = k.T.astype(kt_s.dtype)
        v1_s[:, :dh] = v_ref[0]
        v1_s[:, dh:] = jnp.ones((t_tot, dh), v1_s.dtype)

    r0 = pl.multiple_of(qi * tq, tq)
    cos = cos_ref[pl.ds(r0, tq), :]
    sin_a = sa_ref[pl.ds(r0, tq), :]
    sin_b = sb_ref[pl.ds(r0, tq), :]

    def attend(n_keys):
        for g in range(n_group):
            q = _rms_norm(q_ref[0, :, g * dh:(g + 1) * dh].astype(F32), qn_ref[...])
            q = _rope(q, cos, sin_a, sin_b) * (dh ** -0.5 * LOG2E)
            s = _dot(q.astype(BF16), kt_s[:, :n_keys])
            p = jnp.exp2(s - jnp.max(s, axis=-1, keepdims=True))
            num_den = _dot(p.astype(BF16), v1_s[:n_keys, :])
            o_ref[0, :, g * dh:(g + 1) * dh] = (num_den[:, :dh] / num_den[:, dh:]).astype(o_ref.dtype)

    @pl.when(r0 < ctx_len)
    def _():
        attend(ctx_len)

    @pl.when(r0 >= ctx_len)
    def _():
        attend(t_tot)


def _attention(p_qkv, tables, q_norm, k_norm, ctx_len, n_hq, n_hkv):
    n_batch, t_tot, _ = p_qkv.shape
    dh = ATT_DH
    n_group = n_hq // n_hkv
    tq = TOKEN_TILE
    cos, sin_a, sin_b = tables
    full = lambda shape: pl.BlockSpec(shape, lambda b, h, q: (0,) * len(shape))
    return pl.pallas_call(
        functools.partial(_attn_body, ctx_len=ctx_len, n_group=n_group),
        grid=(n_batch, n_hkv, t_tot // tq),
        in_specs=[
            pl.BlockSpec((1, tq, n_group * dh), lambda b, h, q: (b, q, h)),
            pl.BlockSpec((1, t_tot, dh), lambda b, h, q: (b, 0, n_hq + h)),
            pl.BlockSpec((1, t_tot, dh), lambda b, h, q: (b, 0, n_hq + n_hkv + h)),
            full((t_tot, dh)), full((t_tot, dh)), full((t_tot, dh)),
            full((1, dh)), full((1, dh)),
        ],
        out_specs=pl.BlockSpec((1, tq, n_group * dh), lambda b, h, q: (b, q, h)),
        out_shape=jax.ShapeDtypeStruct((n_batch, t_tot, n_hq * dh), BF16),
        scratch_shapes=[pltpu.VMEM((dh, t_tot), BF16), pltpu.VMEM((t_tot, 2 * dh), BF16)],
        compiler_params=_params(("parallel", "parallel", "arbitrary")),
        name="attention",
    )(p_qkv, p_qkv, p_qkv, cos, sin_a, sin_b, q_norm, k_norm)


def _rope_tables(ctx_len, seq_len):
    rows = seq_len // GRID_W
    row = jnp.repeat(jnp.arange(rows), GRID_W).astype(F32)
    col = jnp.tile(jnp.arange(GRID_W), rows).astype(F32)
    n_freq = ATT_DH // 4
    freqs = ROPE_THETA ** (-jnp.arange(n_freq, dtype=F32) / n_freq)
    ang_r = row[:, None] * freqs
    ang_c = col[:, None] * freqs
    ang = jnp.concatenate([ang_r, ang_r, ang_c, ang_c], axis=-1)
    cos = jnp.concatenate([jnp.ones((ctx_len, ATT_DH), F32), jnp.cos(ang)], axis=0)
    sin = jnp.concatenate([jnp.zeros((ctx_len, ATT_DH), F32), jnp.sin(ang)], axis=0)
    first_half = (jnp.arange(ATT_DH) % (2 * n_freq)) < n_freq
    return cos, jnp.where(first_half, -sin, 0.0), jnp.where(first_half, 0.0, sin)


def _mix_out_body(y_ref, w_ref, x_ref, mod_ref, lng_ref, lnb_ref, o_ref, *, alpha):
    delta = _dot(y_ref[0], w_ref[...])
    r = alpha * x_ref[0] + mod_ref[0, 2:3, :] * delta
    o_ref[0] = _layer_norm(r, lng_ref[...], lnb_ref[...])


def _mix_out(y, w, xs, mods, ln_g, ln_b, n_ctx_tiles, alpha):
    n_batch, t_tot, d = xs.shape
    k = y.shape[-1]
    tm = TOKEN_TILE
    return pl.pallas_call(
        functools.partial(_mix_out_body, alpha=alpha),
        grid=(n_batch, t_tot // tm),
        in_specs=[
            pl.BlockSpec((1, tm, k), lambda b, t: (b, t, 0)),
            pl.BlockSpec((k, d), lambda b, t: (0, 0)),
            pl.BlockSpec((1, tm, d), lambda b, t: (b, t, 0)),
            pl.BlockSpec((1, N_MOD, d), _mod_row_map(n_batch, n_ctx_tiles)),
            pl.BlockSpec((1, d), lambda b, t: (0, 0)),
            pl.BlockSpec((1, d), lambda b, t: (0, 0)),
        ],
        out_specs=pl.BlockSpec((1, tm, d), lambda b, t: (b, t, 0)),
        out_shape=jax.ShapeDtypeStruct(xs.shape, F32),
        compiler_params=_params(("parallel", "parallel")),
        name="mix_out",
    )(y, w, xs, mods, ln_g, ln_b)


def _ffn_body(x_ref, mod_ref, win_ref, wout_ref, lng_ref, lnb_ref, o_ref, *, n_chunk, alpha):
    x = x_ref[0]
    hb = (x * (1.0 + mod_ref[0, 4:5, :]) + mod_ref[0, 3:4, :]).astype(BF16)
    d_ff = wout_ref.shape[0]
    tc = d_ff // n_chunk
    acc = None
    for c in range(n_chunk):
        gate = _dot(hb, win_ref[:, c * tc:(c + 1) * tc])
        up = _dot(hb, win_ref[:, d_ff + c * tc:d_ff + (c + 1) * tc])
        part = _dot((_silu(gate) * up).astype(BF16), wout_ref[c * tc:(c + 1) * tc, :])
        acc = part if acc is None else acc + part
    r = alpha * x + mod_ref[0, 5:6, :] * acc
    o_ref[0] = _layer_norm(r, lng_ref[...], lnb_ref[...])


def _ffn(xs, mods, w_in, w_out, ln_g, ln_b, n_ctx_tiles, alpha):
    n_batch, t_tot, d = xs.shape
    d_ff = w_out.shape[0]
    tm = TOKEN_TILE
    n_chunk = 2 if d_ff % (2 * LANES) == 0 else 1
    return pl.pallas_call(
        functools.partial(_ffn_body, n_chunk=n_chunk, alpha=alpha),
        grid=(n_batch, t_tot // tm),
        in_specs=[
            pl.BlockSpec((1, tm, d), lambda b, t: (b, t, 0)),
            pl.BlockSpec((1, N_MOD, d), _mod_row_map(n_batch, n_ctx_tiles)),
            pl.BlockSpec((d, 2 * d_ff), lambda b, t: (0, 0)),
            pl.BlockSpec((d_ff, d), lambda b, t: (0, 0)),
            pl.BlockSpec((1, d), lambda b, t: (0, 0)),
            pl.BlockSpec((1, d), lambda b, t: (0, 0)),
        ],
        out_specs=pl.BlockSpec((1, tm, d), lambda b, t: (b, t, 0)),
        out_shape=jax.ShapeDtypeStruct(xs.shape, F32),
        compiler_params=_params(("parallel", "parallel")),
        name="ffn",
    )(xs, mods, w_in, w_out, ln_g, ln_b)


def kernel(x, c, ctx, c_ctx, w_mod, b_mod, ln_g, ln_b, w_ffn_in, w_ffn_out, gdn_w_in, gdn_conv, gdn_a_log,
           gdn_dt_bias, gdn_norm_g, gdn_w_out, attn_w_qkv, attn_q_norm, attn_k_norm, attn_w_out):
    n_batch, seq_len, d = x.shape
    ctx_len = ctx.shape[1]
    depth = w_mod.shape[0]
    alpha = (2 * depth) ** 0.25
    n_hv = gdn_a_log.shape[-1]
    dv = gdn_norm_g.shape[-1]
    v_w = n_hv * dv
    qkv_w = gdn_conv.shape[-1]
    n_hk = (qkv_w - v_w) // (2 * GDN_DK)
    n_hq = attn_w_out.shape[1] // ATT_DH
    n_hkv = (attn_w_qkv.shape[-1] // ATT_DH - n_hq) // 2
    assert ctx_len % TOKEN_TILE == 0 and seq_len % TOKEN_TILE == 0 and seq_len % GRID_W == 0
    assert 4 * n_hv <= LANES and d % LANES == 0
    n_ctx_tiles = ctx_len // TOKEN_TILE

    xs = jnp.concatenate([ctx, x], axis=1)

    mp = -(-(n_batch + 1) // SUBLANES) * SUBLANES
    cond = jnp.concatenate([c, c_ctx[None, :], jnp.zeros((mp - n_batch - 1, d), F32)], axis=0)
    mods = _modulation(cond, w_mod, b_mod).reshape(depth, mp, N_MOD, d)
    tables = _rope_tables(ctx_len, seq_len)

    for i in range(depth):
        j = i // 2
        m = mods[i]
        if i % 2 == 0:
            w_in = gdn_w_in[j]
            gate_w = jnp.pad(w_in[:, qkv_w + v_w:], ((0, 0), (0, LANES - 4 * n_hv)))
            p_main, raw = _in_proj(xs, m, w_in[:, :qkv_w + v_w].astype(BF16), gate_w, n_ctx_tiles, n_chunk=6)
            lanes_pad = (0, LANES - 4 * n_hv)
            a_log = jnp.pad(jnp.concatenate([jnp.zeros((2 * n_hv,), F32), gdn_a_log[j].reshape(-1)]), lanes_pad)
            dt_b = jnp.pad(jnp.concatenate([jnp.zeros((2 * n_hv,), F32), gdn_dt_bias[j].reshape(-1)]), lanes_pad)
            colg, rowg = _gdn_gates(raw, a_log[None, :], dt_b[None, :], n_hv)
            y = _gdn_delta(p_main, gdn_conv[j], colg, rowg, gdn_norm_g[j][None, :], ctx_len, n_hk, n_hv)
            w_o = gdn_w_out[j]
        else:
            (p_qkv,) = _in_proj(xs, m, attn_w_qkv[j].astype(BF16), None, n_ctx_tiles, n_chunk=3)
            y = _attention(p_qkv, tables, attn_q_norm[j][None, :], attn_k_norm[j][None, :], ctx_len, n_hq, n_hkv)
            w_o = attn_w_out[j]
        xs = _mix_out(y, w_o.astype(BF16), xs, m, ln_g[i, 0][None, :], ln_b[i, 0][None, :], n_ctx_tiles, alpha)
        xs = _ffn(xs, m, w_ffn_in[i].astype(BF16), w_ffn_out[i].astype(BF16),
                  ln_g[i, 1][None, :], ln_b[i, 1][None, :], n_ctx_tiles, alpha)
    return xs[:, ctx_len:]
```

```python
import functools
import math

import jax
import jax.numpy as jnp
from jax import lax
from jax.experimental import pallas as pl
from jax.experimental.pallas import tpu as pltpu

F32 = jnp.float32
BF16 = jnp.bfloat16

N_MOD = 6
EPS = 1e-6
GDN_DK = 128
CHUNK = 64
ATT_DH = 128
GRID_W = 64
ROPE_THETA = 10000.0

LANES = 128
SUBLANES = 8
PAIR = 2 * CHUNK
VMEM_LIMIT_BYTES = 56 * 1024 * 1024
TOKEN_TILE = 256
NEG_INF = -1e30
LOG2E = math.log2(math.e)

assert PAIR == LANES


def _params(sem):
    return pltpu.CompilerParams(dimension_semantics=sem, vmem_limit_bytes=VMEM_LIMIT_BYTES)


def _dot(a, b):
    return jnp.dot(a, b, preferred_element_type=F32)


def _split2(a):
    hi = a.astype(BF16)
    return hi, (a - hi.astype(F32)).astype(BF16)


def _dot_hi(a, b):
    a1, a2 = _split2(a)
    b1, b2 = _split2(b)
    return _dot(a1, b1) + (_dot(a1, b2) + _dot(a2, b1))


def _sigmoid(x):
    return 1.0 / (1.0 + jnp.exp(-x))


def _silu(x):
    return x * _sigmoid(x)


def _layer_norm(r, g, b):
    mu = jnp.mean(r, axis=-1, keepdims=True)
    rc = r - mu
    var = jnp.mean(rc * rc, axis=-1, keepdims=True)
    return rc * lax.rsqrt(var + EPS) * g + b


def _mod_body(cond_ref, w_ref, b_ref, o_ref):
    o_ref[0] = _dot_hi(_silu(cond_ref[...]), w_ref[0]) + b_ref[0]


def _modulation(cond, w_mod, b_mod):
    n_layer, d, n = w_mod.shape
    mp = cond.shape[0]
    tn = n // 4
    return pl.pallas_call(
        _mod_body,
        grid=(n_layer, n // tn),
        in_specs=[
            pl.BlockSpec((mp, d), lambda l, j: (0, 0)),
            pl.BlockSpec((1, d, tn), lambda l, j: (l, 0, j)),
            pl.BlockSpec((1, 1, tn), lambda l, j: (l, 0, j)),
        ],
        out_specs=pl.BlockSpec((1, mp, tn), lambda l, j: (l, 0, j)),
        out_shape=jax.ShapeDtypeStruct((n_layer, mp, n), F32),
        compiler_params=_params(("parallel", "parallel")),
        name="modulation",
    )(cond, w_mod, b_mod.reshape(n_layer, 1, n))


def _mod_row_map(n_batch, n_ctx_tiles):
    return lambda b, t: (jnp.where(t < n_ctx_tiles, n_batch, b), 0, 0)


def _in_proj_body(x_ref, mod_ref, w_ref, o_ref, *, n_chunk):
    hb = (x_ref[0] * (1.0 + mod_ref[0, 1:2, :]) + mod_ref[0, 0:1, :]).astype(BF16)
    tn = w_ref.shape[1] // n_chunk
    for j in range(n_chunk):
        o_ref[0, :, j * tn:(j + 1) * tn] = _dot(hb, w_ref[:, j * tn:(j + 1) * tn]).astype(o_ref.dtype)


def _in_proj(xs, mods, w, n_ctx_tiles, n_chunk):
    n_batch, t_tot, d = xs.shape
    n = w.shape[1]
    tm = TOKEN_TILE
    return pl.pallas_call(
        functools.partial(_in_proj_body, n_chunk=n_chunk),
        grid=(n_batch, t_tot // tm),
        in_specs=[
            pl.BlockSpec((1, tm, d), lambda b, t: (b, t, 0)),
            pl.BlockSpec((1, N_MOD, d), _mod_row_map(n_batch, n_ctx_tiles)),
            pl.BlockSpec((d, n), lambda b, t: (0, 0)),
        ],
        out_specs=pl.BlockSpec((1, tm, n), lambda b, t: (b, t, 0)),
        out_shape=jax.ShapeDtypeStruct((n_batch, t_tot, n), BF16),
        compiler_params=_params(("parallel", "parallel")),
        name="in_proj",
    )(xs, mods, w)


CONV_HALO = SUBLANES


def _l2_norm(x):
    return x * lax.rsqrt(jnp.sum(x * x, axis=-1, keepdims=True) + EPS)


def _gdn_proj_body(x_ref, xp_ref, xn_ref, mod_ref, w_ref, wx_ref, conv_ref, o_ref, ox_ref,
                   *, ctx_len, qk_w, qkv_w, n_chunk, dk):
    tm = x_ref.shape[1]
    r0 = pl.program_id(1) * tm
    t_tot = pl.num_programs(1) * tm
    scale = 1.0 + mod_ref[0, 1:2, :]
    shift = mod_ref[0, 0:1, :]
    h = x_ref[0] * scale + shift
    hb = h.astype(BF16)
    hb_ext = jnp.concatenate([xp_ref[0] * scale + shift, h, xn_ref[0] * scale + shift], axis=0).astype(BF16)
    keep_lo = jnp.where(jnp.logical_or(r0 == 0, r0 == ctx_len), 0.0, 1.0)
    keep_hi = jnp.where(jnp.logical_or(r0 + tm == ctx_len, r0 + tm == t_tot), 0.0, 1.0)
    conv_w = conv_ref.shape[0]
    pad = conv_w // 2
    n_ext = tm + 2 * CONV_HALO
    mid = slice(CONV_HALO, CONV_HALO + tm)
    tn = w_ref.shape[1] // n_chunk
    for j in range(n_chunk):
        c0 = j * tn
        if c0 >= qkv_w:
            o_ref[0, :, c0:c0 + tn] = _dot(hb, w_ref[:, c0:c0 + tn]).astype(o_ref.dtype)
            continue
        p = _dot(hb_ext, w_ref[:, c0:c0 + tn])
        ext = jnp.concatenate([p[:CONV_HALO] * keep_lo, p[mid], p[CONV_HALO + tm:] * keep_hi], axis=0)
        y = None
        for tap in range(conv_w):
            off = tap - pad
            xs = ext if off == 0 else pltpu.roll(ext, (-off) % n_ext, axis=0)
            term = xs[mid] * conv_ref[tap:tap + 1, c0:c0 + tn]
            y = term if y is None else y + term
        y = _silu(y)
        if c0 >= 2 * qk_w:
            o_ref[0, :, c0:c0 + tn] = y.astype(o_ref.dtype)
            continue
        for hh in range(tn // dk):
            yh = _l2_norm(y[:, hh * dk:(hh + 1) * dk])
            if c0 < qk_w:
                yh = yh * (dk ** -0.5)
            o_ref[0, :, c0 + hh * dk:c0 + (hh + 1) * dk] = yh.astype(o_ref.dtype)
    ox_ref[0] = _dot_hi(h, wx_ref[...])


def _gdn_in_proj(xs, mods, w, w_extra, conv, n_ctx_tiles, qk_w, n_chunk):
    n_batch, t_tot, d = xs.shape
    n = w.shape[1]
    conv_w, qkv_w = conv.shape
    tm = TOKEN_TILE
    tn = n // n_chunk
    assert qk_w % tn == 0 and qkv_w % tn == 0 and tn % GDN_DK == 0
    halo_per_tile = tm // CONV_HALO
    last_halo = t_tot // CONV_HALO - 1
    return pl.pallas_call(
        functools.partial(_gdn_proj_body, ctx_len=n_ctx_tiles * tm, qk_w=qk_w, qkv_w=qkv_w, n_chunk=n_chunk,
                          dk=GDN_DK),
        grid=(n_batch, t_tot // tm),
        in_specs=[
            pl.BlockSpec((1, tm, d), lambda b, t: (b, t, 0)),
            pl.BlockSpec((1, CONV_HALO, d), lambda b, t: (b, jnp.maximum(t * halo_per_tile - 1, 0), 0)),
            pl.BlockSpec((1, CONV_HALO, d), lambda b, t: (b, jnp.minimum((t + 1) * halo_per_tile, last_halo), 0)),
            pl.BlockSpec((1, N_MOD, d), _mod_row_map(n_batch, n_ctx_tiles)),
            pl.BlockSpec((d, n), lambda b, t: (0, 0)),
            pl.BlockSpec((d, LANES), lambda b, t: (0, 0)),
            pl.BlockSpec((conv_w, qkv_w), lambda b, t: (0, 0)),
        ],
        out_specs=[pl.BlockSpec((1, tm, n), lambda b, t: (b, t, 0)),
                   pl.BlockSpec((1, tm, LANES), lambda b, t: (b, t, 0))],
        out_shape=[jax.ShapeDtypeStruct((n_batch, t_tot, n), BF16),
                   jax.ShapeDtypeStruct((n_batch, t_tot, LANES), F32)],
        compiler_params=_params(("parallel", "parallel")),
        name="gdn_in_proj",
    )(xs, xs, xs, mods, w, w_extra, conv)


def _gates_body(raw_ref, alog_ref, dtb_ref, col_ref, row_ref, *, n_hv):
    raw = raw_ref[0]
    beta = _sigmoid(raw)
    xs = raw + dtb_ref[...]
    g = -jnp.exp(alog_ref[...]) * (jnp.maximum(xs, 0.0) + jnp.log1p(jnp.exp(-jnp.abs(xs))))
    r = lax.broadcasted_iota(jnp.int32, (PAIR, PAIR), 0)
    c = lax.broadcasted_iota(jnp.int32, (PAIR, PAIR), 1)
    same = (r < CHUNK) == (c < CHUNK)
    tri_f = jnp.where(jnp.logical_and(same, c <= r), 1.0, 0.0).astype(BF16)
    tri_r = jnp.where(jnp.logical_and(same, c >= r), 1.0, 0.0).astype(BF16)
    lane = lax.broadcasted_iota(jnp.int32, (PAIR, LANES), 1)
    for p in range(raw.shape[0] // PAIR):
        gp = g[p * PAIR:(p + 1) * PAIR]
        g1 = gp.astype(BF16)
        r1 = gp - g1.astype(F32)
        g2 = r1.astype(BF16)
        g3 = (r1 - g2.astype(F32)).astype(BF16)
        cum_f = _dot(tri_f, g1) + (_dot(tri_f, g2) + _dot(tri_f, g3))
        cum_r = _dot(tri_r, g1) + (_dot(tri_r, g2) + _dot(tri_r, g3))
        col = jnp.where(lane < 2 * n_hv, beta[p * PAIR:(p + 1) * PAIR],
                        jnp.where(lane < 3 * n_hv, cum_f, cum_r))
        col_ref[0, p * PAIR:(p + 1) * PAIR, :] = col
        row_ref[0, p] = col.T


def _gdn_gates(raw, a_log_lanes, dt_bias_lanes, n_hv):
    n_batch, t_tot, _ = raw.shape
    tg = TOKEN_TILE
    return pl.pallas_call(
        functools.partial(_gates_body, n_hv=n_hv),
        grid=(n_batch, t_tot // tg),
        in_specs=[
            pl.BlockSpec((1, tg, LANES), lambda b, t: (b, t, 0)),
            pl.BlockSpec((1, LANES), lambda b, t: (0, 0)),
            pl.BlockSpec((1, LANES), lambda b, t: (0, 0)),
        ],
        out_specs=[
            pl.BlockSpec((1, tg, LANES), lambda b, t: (b, t, 0)),
            pl.BlockSpec((1, tg // PAIR, LANES, PAIR), lambda b, t: (b, t, 0, 0)),
        ],
        out_shape=[
            jax.ShapeDtypeStruct((n_batch, t_tot, LANES), F32),
            jax.ShapeDtypeStruct((n_batch, t_tot // PAIR, LANES, PAIR), F32),
        ],
        compiler_params=_params(("parallel", "parallel")),
        name="gdn_gates",
    )(raw, a_log_lanes, dt_bias_lanes)


STAGE_TILE = 256
HEADS_PER_STEP = 2
CHUNKS_PER_STEP = 4
SUPER = CHUNKS_PER_STEP * CHUNK


def _delta_body(q_ref, k_ref, v_ref, z_ref, colg_ref, rowg_ref, ng_ref,
                y_ref, o_s, st_s, cg_s, uw_s, qkd_s, kt2_s, *, ctx_len, n_hv, n_m):
    g_idx = pl.program_id(1)
    t_tot = q_ref.shape[1]
    dk = q_ref.shape[2] // n_m
    dv = v_ref.shape[2] // (2 * n_m)
    n_sup = t_tot // SUPER
    n_ctx_sup = ctx_len // SUPER

    lane_shift = (LANES - 2 * n_m * g_idx) % LANES

    def prep(i, carry):
        rows = pl.ds(pl.multiple_of(i * STAGE_TILE, STAGE_TILE), STAGE_TILE)
        for m in range(n_m):
            km = k_ref[0, rows, m * dk:(m + 1) * dk].astype(F32)
            for cc in range(STAGE_TILE // CHUNK):
                kc = km[cc * CHUNK:(cc + 1) * CHUNK]
                kt2_s[i * (STAGE_TILE // CHUNK) + cc, m] = jnp.concatenate([kc, kc], axis=0).T
        cg_s[rows, :] = pltpu.roll(colg_ref[0, rows, :], lane_shift, axis=1)
        return carry

    lax.fori_loop(0, t_tot // STAGE_TILE, prep, 0)
    o_s[...] = jnp.zeros_like(o_s)
    st_s[...] = jnp.zeros_like(st_s)
    uw_s[...] = jnp.zeros_like(uw_s)
    qkd_s[...] = jnp.zeros_like(qkd_s)

    r = lax.broadcasted_iota(jnp.int32, (CHUNK, LANES), 0)
    lane = lax.broadcasted_iota(jnp.int32, (CHUNK, LANES), 1)
    c = lane & (CHUNK - 1)
    left = lane < CHUNK
    left_row = lax.broadcasted_iota(jnp.int32, (1, LANES), 1) < CHUNK
    keep_left = jnp.where(left, 1.0, 0.0).astype(BF16)
    keep_right = jnp.where(left, 0.0, 1.0).astype(BF16)
    eye2 = jnp.where(c == r, 1.0, 0.0)
    n_lvl = CHUNK.bit_length() - 1
    lvl_mask = [jnp.logical_and((r >> (lg + 1)) == (c >> (lg + 1)), (r >> lg) != (c >> lg)) for lg in range(n_lvl)]
    incl = (c <= r, c >= r)
    strict = (c < r, c > r)
    zero_b = jnp.zeros((CHUNK, dv), BF16)

    def block_diag(y):
        return jnp.concatenate([y * keep_left, y * keep_right], axis=0)

    def reverse_super(i):
        return jnp.where(i < n_ctx_sup, n_ctx_sup - 1 - i, n_sup + n_ctx_sup - 1 - i)

    def make_units(i):
        sup = (i, reverse_super(i))
        units = []
        for m in range(n_m):
            for d in (0, 1):
                for seq in range(CHUNKS_PER_STEP):
                    cidx = seq if d == 0 else CHUNKS_PER_STEP - 1 - seq
                    units.append(dict(
                        m=m, d=d, seq=seq, half=cidx % 2, uid=(m * 2 + d) * CHUNKS_PER_STEP + seq,
                        pair=sup[d] * (SUPER // PAIR) + cidx // 2, chunk=sup[d] * CHUNKS_PER_STEP + cidx,
                        rows=pl.ds(pl.multiple_of(sup[d] * SUPER + cidx * CHUNK, CHUNK), CHUNK)))
        return units

    def load_gates(u):
        m, d, half = u["m"], u["d"], u["half"]
        cg = cg_s[u["rows"], :]
        b0, g0 = d * n_hv + 2 * m, (2 + d) * n_hv + 2 * m
        u["beta_c"] = (cg[:, b0:b0 + 1], cg[:, b0 + 1:b0 + 2])
        u["gc_c"] = (cg[:, g0:g0 + 1], cg[:, g0 + 1:g0 + 2])
        head0 = 2 * (n_m * g_idx + m)

        def packed_rows(base):
            rows = rowg_ref[0, u["pair"], pl.ds(base + head0, 2), :]
            a, b = rows[0:1], rows[1:2]
            if half == 0:
                return jnp.where(left_row, a, pltpu.roll(b, CHUNK, axis=1))
            return jnp.where(left_row, pltpu.roll(a, CHUNK, axis=1), b)

        u["beta_r"] = packed_rows(d * n_hv)
        u["gc_r"] = packed_rows((2 + d) * n_hv)

    def state_free_part(i, slot):
        units = make_units(i)
        for u in units:
            m = u["m"]
            load_gates(u)
            qb, kb = q_ref[0, u["rows"], m * dk:(m + 1) * dk], k_ref[0, u["rows"], m * dk:(m + 1) * dk]
            u["kb"] = kb
            u["gram"] = lax.dot_general(jnp.concatenate([kb, qb], axis=0), jnp.concatenate([kb, kb], axis=0),
                                        (((1,), (1,)), ((), ())), preferred_element_type=F32)
        yield
        for u in units:
            d = u["d"]
            gc_c2 = jnp.where(left, u["gc_c"][0], u["gc_c"][1])
            beta_c2 = jnp.where(left, u["beta_c"][0], u["beta_c"][1])
            decay = jnp.where(incl[d], jnp.exp(jnp.where(incl[d], gc_c2 - u["gc_r"], 0.0)), 0.0)
            a = jnp.where(strict[d], u["gram"][0:CHUNK] * decay, 0.0) * beta_c2
            qkd_s[slot, u["uid"]] = (u["gram"][CHUNK:2 * CHUNK] * decay).astype(BF16)
            u["a"] = a
            u["x"] = eye2 - jnp.where(lvl_mask[0], a, 0.0)
        yield
        for lg in range(1, n_lvl):
            for u in units:
                u["xb"] = u["x"].astype(BF16)
                off = jnp.where(lvl_mask[lg], u["a"], 0.0).astype(BF16)
                u["z"] = _dot(off, block_diag(u["xb"]))
            yield
            for u in units:
                u["x"] = u["x"] - _dot(u["xb"], block_diag(u["z"].astype(BF16)))
            yield
        for u in units:
            m = u["m"]
            tb = u["x"] * u["beta_r"]
            tbe = tb * jnp.exp(u["gc_r"])
            v2 = v_ref[0, u["rows"], 2 * m * dv:2 * (m + 1) * dv]
            v_bd = jnp.concatenate([jnp.concatenate([v2[:, :dv], zero_b], axis=1),
                                    jnp.concatenate([zero_b, v2[:, dv:]], axis=1)], axis=0)
            k_bd = jnp.concatenate([jnp.concatenate([u["kb"], zero_b], axis=1),
                                    jnp.concatenate([zero_b, u["kb"]], axis=1)], axis=0)
            uw_s[slot, u["uid"], :, :2 * dv] = _dot(tb.astype(BF16), v_bd)
            uw_s[slot, u["uid"], :, 2 * dv:] = _dot(tbe.astype(BF16), k_bd)
        yield

    def recurrence_part(i, slot):
        units = make_units(i)
        for seq in range(CHUNKS_PER_STEP):
            now = [u for u in units if u["seq"] == seq]
            for u in now:
                m = u["m"]
                load_gates(u)
                uw = uw_s[slot, u["uid"]]
                u["u"] = uw[:, :2 * dv]
                u["s2"] = st_s[2 * m + u["d"]]
                lhs = jnp.concatenate([uw[:, 2 * dv:3 * dv].astype(BF16), uw[:, 3 * dv:].astype(BF16),
                                       q_ref[0, u["rows"], m * dk:(m + 1) * dk]], axis=0)
                u["ws"] = _dot(lhs, u["s2"].astype(BF16))
            yield
            for u in now:
                d, ws = u["d"], u["ws"]
                vn0 = (u["u"][:, :dv] - ws[0:CHUNK, :dv]).astype(BF16)
                vn1 = (u["u"][:, dv:] - ws[CHUNK:2 * CHUNK, dv:]).astype(BF16)
                vn_bd = jnp.concatenate([jnp.concatenate([vn0, zero_b], axis=1),
                                         jnp.concatenate([zero_b, vn1], axis=1)], axis=0)
                last = (CHUNK - 1, 2 * CHUNK - 1) if d == 0 else (0, CHUNK)
                u["gl"] = [u["gc_r"][:, l:l + 1] for l in last]
                gl2 = jnp.where(left_row, u["gl"][0], u["gl"][1])
                kdt = kt2_s[u["chunk"], u["m"]] * jnp.exp(gl2 - u["gc_r"])
                lhs = jnp.concatenate([qkd_s[slot, u["uid"]], kdt.astype(BF16)], axis=0)
                u["ov"] = _dot(lhs, vn_bd)
            yield
            for u in now:
                m, ws, ov = u["m"], u["ws"], u["ov"]
                qs = jnp.concatenate([ws[2 * CHUNK:, :dv] * jnp.exp(u["gc_c"][0]),
                                      ws[2 * CHUNK:, dv:] * jnp.exp(u["gc_c"][1])], axis=1)
                o_s[u["rows"], 2 * m * dv:2 * (m + 1) * dv] += qs + ov[0:CHUNK]
                egl = jnp.concatenate([jnp.broadcast_to(jnp.exp(u["gl"][0]), (1, dv)),
                                       jnp.broadcast_to(jnp.exp(u["gl"][1]), (1, dv))], axis=1)
                st_s[2 * m + u["d"]] = u["s2"] * egl + ov[CHUNK:]
            yield

    def step(i, carry):
        slot = i & 1
        ahead = state_free_part(jnp.minimum(i, n_sup - 1), slot)
        behind = recurrence_part(jnp.maximum(i - 1, 0), 1 - slot)
        live = [ahead, behind]
        while live:
            live = [g for g in live if next(g, "done") != "done"]
        return carry

    lax.fori_loop(0, n_sup + 1, step, 0)

    def finish(i, carry):
        rows = pl.ds(pl.multiple_of(i * STAGE_TILE, STAGE_TILE), STAGE_TILE)
        for j in range(2 * n_m):
            o = o_s[rows, j * dv:(j + 1) * dv]
            z = z_ref[0, rows, j * dv:(j + 1) * dv].astype(F32)
            y = o * lax.rsqrt(jnp.mean(o * o, axis=-1, keepdims=True) + EPS) * ng_ref[...]
            y_ref[0, rows, j * dv:(j + 1) * dv] = (y * _silu(z)).astype(y_ref.dtype)
        return carry

    lax.fori_loop(0, t_tot // STAGE_TILE, finish, 0)


def _gdn_delta(p_main, colg, rowg, norm_g, ctx_len, n_hk, n_hv):
    n_batch, t_tot, _ = p_main.shape
    dk = GDN_DK
    dv = norm_g.shape[-1]
    assert n_hv == 2 * n_hk and dk == dv == LANES and n_hk % HEADS_PER_STEP == 0
    assert ctx_len % SUPER == 0 and t_tot % SUPER == 0 and SUPER % STAGE_TILE == 0
    n_m = HEADS_PER_STEP
    qb = n_m * dk
    vb = 2 * n_m * dv
    k_blk0 = n_hk // n_m
    v_blk0 = (2 * n_hk * dk) // vb
    z_blk0 = (2 * n_hk * dk + n_hv * dv) // vb
    n_pairs = t_tot // PAIR
    n_units = 2 * n_m * CHUNKS_PER_STEP
    return pl.pallas_call(
        functools.partial(_delta_body, ctx_len=ctx_len, n_hv=n_hv, n_m=n_m),
        grid=(n_batch, n_hk // n_m),
        in_specs=[
            pl.BlockSpec((1, t_tot, qb), lambda b, h: (b, 0, h)),
            pl.BlockSpec((1, t_tot, qb), lambda b, h: (b, 0, k_blk0 + h)),
            pl.BlockSpec((1, t_tot, vb), lambda b, h: (b, 0, v_blk0 + h)),
            pl.BlockSpec((1, t_tot, vb), lambda b, h: (b, 0, z_blk0 + h)),
            pl.BlockSpec((1, t_tot, LANES), lambda b, h: (b, 0, 0)),
            pl.BlockSpec((1, n_pairs, LANES, PAIR), lambda b, h: (b, 0, 0, 0)),
            pl.BlockSpec((1, dv), lambda b, h: (0, 0)),
        ],
        out_specs=pl.BlockSpec((1, t_tot, vb), lambda b, h: (b, 0, h)),
        out_shape=jax.ShapeDtypeStruct((n_batch, t_tot, n_hv * dv), BF16),
        scratch_shapes=[
            pltpu.VMEM((t_tot, vb), F32),
            pltpu.VMEM((2 * n_m, dk, 2 * dv), F32),
            pltpu.VMEM((t_tot, LANES), F32),
            pltpu.VMEM((2, n_units, CHUNK, 2 * dv + 2 * dk), F32),
            pltpu.VMEM((2, n_units, CHUNK, 2 * CHUNK), BF16),
            pltpu.VMEM((t_tot // CHUNK, n_m, dk, 2 * CHUNK), F32),
        ],
        compiler_params=_params(("parallel", "parallel")),
        name="gdn_delta",
    )(p_main, p_main, p_main, p_main, colg, rowg, norm_g)


def _rope(x, cos, sin_a, sin_b):
    quarter = ATT_DH // 4
    return x * cos + pltpu.roll(x, ATT_DH - quarter, axis=1) * sin_a + pltpu.roll(x, quarter, axis=1) * sin_b


def _rms_norm(x, g):
    return x * lax.rsqrt(jnp.mean(x * x, axis=-1, keepdims=True) + EPS) * g


def _attn_body(q_ref, k_ref, v_ref, cos_ref, sa_ref, sb_ref, qn_ref, kn_ref, o_ref, kt_s, v1_s,
               *, ctx_len, n_group, skip_tiles):
    qi = pl.program_id(2)
    tq = q_ref.shape[1]
    t_tot = k_ref.shape[1]
    dh = k_ref.shape[2]

    @pl.when(qi == 0)
    def _():
        k = _rms_norm(k_ref[0].astype(F32), kn_ref[...])
        k = _rope(k, cos_ref[...], sa_ref[...], sb_ref[...])
        kt_s[...] = k.T.astype(kt_s.dtype)
        v1_s[:, :dh] = v_ref[0]
        v1_s[:, dh:] = jnp.ones((t_tot, dh), v1_s.dtype)

    r0 = pl.multiple_of((qi + skip_tiles) * tq, tq)
    cos = cos_ref[pl.ds(r0, tq), :]
    sin_a = sa_ref[pl.ds(r0, tq), :]
    sin_b = sb_ref[pl.ds(r0, tq), :]

    def attend(n_keys):
        def scores(g):
            q = _rms_norm(q_ref[0, :, g * dh:(g + 1) * dh].astype(F32), qn_ref[...])
            q = _rope(q, cos, sin_a, sin_b) * (dh ** -0.5 * LOG2E)
            return _dot(q.astype(BF16), kt_s[:, :n_keys])

        s_next = scores(0)
        for g in range(n_group):
            s = s_next
            if g + 1 < n_group:
                s_next = scores(g + 1)
            p = jnp.exp2(s - jnp.max(s, axis=-1, keepdims=True))
            num_den = _dot(p.astype(BF16), v1_s[:n_keys, :])
            o_ref[0, :, g * dh:(g + 1) * dh] = (num_den[:, :dh] / num_den[:, dh:]).astype(o_ref.dtype)

    @pl.when(r0 < ctx_len)
    def _():
        attend(ctx_len)

    @pl.when(r0 >= ctx_len)
    def _():
        attend(t_tot)


def _attention(p_qkv, tables, q_norm, k_norm, ctx_len, n_hq, n_hkv, skip_tiles=0):
    n_batch, t_tot, _ = p_qkv.shape
    dh = ATT_DH
    n_group = n_hq // n_hkv
    tq = TOKEN_TILE
    cos, sin_a, sin_b = tables
    full = lambda shape: pl.BlockSpec(shape, lambda b, h, q: (0,) * len(shape))
    return pl.pallas_call(
        functools.partial(_attn_body, ctx_len=ctx_len, n_group=n_group, skip_tiles=skip_tiles),
        grid=(n_batch, n_hkv, t_tot // tq - skip_tiles),
        in_specs=[
            pl.BlockSpec((1, tq, n_group * dh), lambda b, h, q: (b, q + skip_tiles, h)),
            pl.BlockSpec((1, t_tot, dh), lambda b, h, q: (b, 0, n_hq + h)),
            pl.BlockSpec((1, t_tot, dh), lambda b, h, q: (b, 0, n_hq + n_hkv + h)),
            full((t_tot, dh)), full((t_tot, dh)), full((t_tot, dh)),
            full((1, dh)), full((1, dh)),
        ],
        out_specs=pl.BlockSpec((1, tq, n_group * dh), lambda b, h, q: (b, q, h)),
        out_shape=jax.ShapeDtypeStruct((n_batch, t_tot - skip_tiles * tq, n_hq * dh), BF16),
        scratch_shapes=[pltpu.VMEM((dh, t_tot), BF16), pltpu.VMEM((t_tot, 2 * dh), BF16)],
        compiler_params=_params(("parallel", "parallel", "arbitrary")),
        name="attention",
    )(p_qkv, p_qkv, p_qkv, cos, sin_a, sin_b, q_norm, k_norm)


def _rope_tables(ctx_len, seq_len):
    rows = seq_len // GRID_W
    row = jnp.repeat(jnp.arange(rows), GRID_W).astype(F32)
    col = jnp.tile(jnp.arange(GRID_W), rows).astype(F32)
    n_freq = ATT_DH // 4
    freqs = ROPE_THETA ** (-jnp.arange(n_freq, dtype=F32) / n_freq)
    ang_r = row[:, None] * freqs
    ang_c = col[:, None] * freqs
    ang = jnp.concatenate([ang_r, ang_r, ang_c, ang_c], axis=-1)
    cos = jnp.concatenate([jnp.ones((ctx_len, ATT_DH), F32), jnp.cos(ang)], axis=0)
    sin = jnp.concatenate([jnp.zeros((ctx_len, ATT_DH), F32), jnp.sin(ang)], axis=0)
    first_half = (jnp.arange(ATT_DH) % (2 * n_freq)) < n_freq
    return cos, jnp.where(first_half, -sin, 0.0), jnp.where(first_half, 0.0, sin)


def _mix_out_body(y_ref, w_ref, x_ref, mod_ref, lng_ref, lnb_ref, o_ref, *, alpha):
    delta = _dot(y_ref[0], w_ref[...])
    r = alpha * x_ref[0] + mod_ref[0, 2:3, :] * delta
    o_ref[0] = _layer_norm(r, lng_ref[...], lnb_ref[...])


def _mix_out(y, w, xs, mods, ln_g, ln_b, n_ctx_tiles, alpha, skip_tiles=0):
    n_batch, t_tot, d = xs.shape
    k = y.shape[-1]
    tm = TOKEN_TILE
    row_map = _mod_row_map(n_batch, n_ctx_tiles)
    y_skip = skip_tiles - (t_tot - y.shape[1]) // tm
    return pl.pallas_call(
        functools.partial(_mix_out_body, alpha=alpha),
        grid=(n_batch, t_tot // tm - skip_tiles),
        in_specs=[
            pl.BlockSpec((1, tm, k), lambda b, t: (b, t + y_skip, 0)),
            pl.BlockSpec((k, d), lambda b, t: (0, 0)),
            pl.BlockSpec((1, tm, d), lambda b, t: (b, t + skip_tiles, 0)),
            pl.BlockSpec((1, N_MOD, d), lambda b, t: row_map(b, t + skip_tiles)),
            pl.BlockSpec((1, d), lambda b, t: (0, 0)),
            pl.BlockSpec((1, d), lambda b, t: (0, 0)),
        ],
        out_specs=pl.BlockSpec((1, tm, d), lambda b, t: (b, t, 0)),
        out_shape=jax.ShapeDtypeStruct((n_batch, t_tot - skip_tiles * tm, d), F32),
        compiler_params=_params(("parallel", "parallel")),
        name="mix_out",
    )(y, w, xs, mods, ln_g, ln_b)


def _ffn_body(x_ref, mod_ref, win_ref, wout_ref, lng_ref, lnb_ref, o_ref, *, n_chunk, alpha):
    x = x_ref[0]
    hb = (x * (1.0 + mod_ref[0, 4:5, :]) + mod_ref[0, 3:4, :]).astype(BF16)
    d_ff = wout_ref.shape[0]
    tc = d_ff // n_chunk
    acc = None
    for c in range(n_chunk):
        gate = _dot(hb, win_ref[:, c * tc:(c + 1) * tc])
        up = _dot(hb, win_ref[:, d_ff + c * tc:d_ff + (c + 1) * tc])
        part = _dot((_silu(gate) * up).astype(BF16), wout_ref[c * tc:(c + 1) * tc, :])
        acc = part if acc is None else acc + part
    r = alpha * x + mod_ref[0, 5:6, :] * acc
    o_ref[0] = _layer_norm(r, lng_ref[...], lnb_ref[...])


def _ffn(xs, mods, w_in, w_out, ln_g, ln_b, n_ctx_tiles, alpha):
    n_batch, t_tot, d = xs.shape
    d_ff = w_out.shape[0]
    tm = TOKEN_TILE
    n_chunk = 2 if d_ff % (2 * LANES) == 0 else 1
    return pl.pallas_call(
        functools.partial(_ffn_body, n_chunk=n_chunk, alpha=alpha),
        grid=(n_batch, t_tot // tm),
        in_specs=[
            pl.BlockSpec((1, tm, d), lambda b, t: (b, t, 0)),
            pl.BlockSpec((1, N_MOD, d), _mod_row_map(n_batch, n_ctx_tiles)),
            pl.BlockSpec((d, 2 * d_ff), lambda b, t: (0, 0)),
            pl.BlockSpec((d_ff, d), lambda b, t: (0, 0)),
            pl.BlockSpec((1, d), lambda b, t: (0, 0)),
            pl.BlockSpec((1, d), lambda b, t: (0, 0)),
        ],
        out_specs=pl.BlockSpec((1, tm, d), lambda b, t: (b, t, 0)),
        out_shape=jax.ShapeDtypeStruct(xs.shape, F32),
        compiler_params=_params(("parallel", "parallel")),
        name="ffn",
    )(xs, mods, w_in, w_out, ln_g, ln_b)


def kernel(x, c, ctx, c_ctx, w_mod, b_mod, ln_g, ln_b, w_ffn_in, w_ffn_out, gdn_w_in, gdn_conv, gdn_a_log,
           gdn_dt_bias, gdn_norm_g, gdn_w_out, attn_w_qkv, attn_q_norm, attn_k_norm, attn_w_out):
    n_batch, seq_len, d = x.shape
    ctx_len = ctx.shape[1]
    depth = w_mod.shape[0]
    alpha = (2 * depth) ** 0.25
    n_hv = gdn_a_log.shape[-1]
    dv = gdn_norm_g.shape[-1]
    v_w = n_hv * dv
    qkv_w = gdn_conv.shape[-1]
    n_hk = (qkv_w - v_w) // (2 * GDN_DK)
    n_hq = attn_w_out.shape[1] // ATT_DH
    n_hkv = (attn_w_qkv.shape[-1] // ATT_DH - n_hq) // 2
    assert ctx_len % TOKEN_TILE == 0 and seq_len % TOKEN_TILE == 0 and seq_len % GRID_W == 0
    assert 4 * n_hv <= LANES and d % LANES == 0
    n_ctx_tiles = ctx_len // TOKEN_TILE

    xs = jnp.concatenate([ctx, x], axis=1)

    mp = -(-(n_batch + 1) // SUBLANES) * SUBLANES
    cond = jnp.concatenate([c, c_ctx[None, :], jnp.zeros((mp - n_batch - 1, d), F32)], axis=0)
    mods = _modulation(cond, w_mod, b_mod).reshape(depth, mp, N_MOD, d)
    tables = _rope_tables(ctx_len, seq_len)

    for i in range(depth):
        j = i // 2
        m = mods[i]
        skip = n_ctx_tiles if i == depth - 1 else 0
        if i % 2 == 0:
            w_in = gdn_w_in[j]
            gate_w = jnp.pad(w_in[:, qkv_w + v_w:], ((0, 0), (0, LANES - 4 * n_hv)))
            p_main, raw = _gdn_in_proj(xs, m, w_in[:, :qkv_w + v_w].astype(BF16), gate_w, gdn_conv[j], n_ctx_tiles,
                                       qk_w=n_hk * GDN_DK, n_chunk=6)
            lanes_pad = (0, LANES - 4 * n_hv)
            a_log = jnp.pad(jnp.concatenate([jnp.zeros((2 * n_hv,), F32), gdn_a_log[j].reshape(-1)]), lanes_pad)
            dt_b = jnp.pad(jnp.concatenate([jnp.zeros((2 * n_hv,), F32), gdn_dt_bias[j].reshape(-1)]), lanes_pad)
            colg, rowg = _gdn_gates(raw, a_log[None, :], dt_b[None, :], n_hv)
            y = _gdn_delta(p_main, colg, rowg, gdn_norm_g[j][None, :], ctx_len, n_hk, n_hv)
            w_o = gdn_w_out[j]
        else:
            p_qkv = _in_proj(xs, m, attn_w_qkv[j].astype(BF16), n_ctx_tiles, n_chunk=3)
            y = _attention(p_qkv, tables, attn_q_norm[j][None, :], attn_k_norm[j][None, :], ctx_len, n_hq, n_hkv,
                           skip_tiles=skip)
            w_o = attn_w_out[j]
        xs = _mix_out(y, w_o.astype(BF16), xs, m, ln_g[i, 0][None, :], ln_b[i, 0][None, :], n_ctx_tiles, alpha,
                      skip_tiles=skip)
        xs = _ffn(xs, m, w_ffn_in[i].astype(BF16), w_ffn_out[i].astype(BF16),
                  ln_g[i, 1][None, :], ln_b[i, 1][None, :], n_ctx_tiles - skip, alpha)
    return xs
```

```python
import functools
import math

import jax
import jax.numpy as jnp
from jax import lax
from jax.experimental import pallas as pl
from jax.experimental.pallas import tpu as pltpu

F32 = jnp.float32
BF16 = jnp.bfloat16

N_MOD = 6
EPS = 1e-6
GDN_DK = 128
CHUNK = 64
ATT_DH = 128
GRID_W = 64
ROPE_THETA = 10000.0

LANES = 128
SUBLANES = 8
PAIR = 2 * CHUNK
VMEM_LIMIT_BYTES = 56 * 1024 * 1024
TOKEN_TILE = 256
NEG_INF = -1e30
LOG2E = math.log2(math.e)

assert PAIR == LANES


def _params(sem):
    return pltpu.CompilerParams(dimension_semantics=sem, vmem_limit_bytes=VMEM_LIMIT_BYTES)


def _dot(a, b):
    return jnp.dot(a, b, preferred_element_type=F32)


def _split2(a):
    hi = a.astype(BF16)
    return hi, (a - hi.astype(F32)).astype(BF16)


def _dot_hi(a, b):
    a1, a2 = _split2(a)
    b1, b2 = _split2(b)
    return _dot(a1, b1) + (_dot(a1, b2) + _dot(a2, b1))


def _sigmoid(x):
    return 1.0 / (1.0 + jnp.exp(-x))


def _silu(x):
    return x * _sigmoid(x)


def _layer_norm(r, g, b):
    mu = jnp.mean(r, axis=-1, keepdims=True)
    rc = r - mu
    var = jnp.mean(rc * rc, axis=-1, keepdims=True)
    return rc * lax.rsqrt(var + EPS) * g + b


def _mod_body(cond_ref, w_ref, b_ref, o_ref):
    o_ref[0] = _dot_hi(_silu(cond_ref[...]), w_ref[0]) + b_ref[0]


def _modulation(cond, w_mod, b_mod):
    n_layer, d, n = w_mod.shape
    mp = cond.shape[0]
    tn = n // 4
    return pl.pallas_call(
        _mod_body,
        grid=(n_layer, n // tn),
        in_specs=[
            pl.BlockSpec((mp, d), lambda l, j: (0, 0)),
            pl.BlockSpec((1, d, tn), lambda l, j: (l, 0, j)),
            pl.BlockSpec((1, 1, tn), lambda l, j: (l, 0, j)),
        ],
        out_specs=pl.BlockSpec((1, mp, tn), lambda l, j: (l, 0, j)),
        out_shape=jax.ShapeDtypeStruct((n_layer, mp, n), F32),
        compiler_params=_params(("parallel", "parallel")),
        name="modulation",
    )(cond, w_mod, b_mod.reshape(n_layer, 1, n))


def _mod_row_map(n_batch, n_ctx_tiles):
    return lambda b, t: (jnp.where(t < n_ctx_tiles, n_batch, b), 0, 0)


def _in_proj_body(x_ref, mod_ref, w_ref, o_ref, *, n_chunk):
    hb = (x_ref[0] * (1.0 + mod_ref[0, 1:2, :]) + mod_ref[0, 0:1, :]).astype(BF16)
    tn = w_ref.shape[1] // n_chunk
    for j in range(n_chunk):
        o_ref[0, :, j * tn:(j + 1) * tn] = _dot(hb, w_ref[:, j * tn:(j + 1) * tn]).astype(o_ref.dtype)


def _in_proj(xs, mods, w, n_ctx_tiles, n_chunk):
    n_batch, t_tot, d = xs.shape
    n = w.shape[1]
    tm = TOKEN_TILE
    return pl.pallas_call(
        functools.partial(_in_proj_body, n_chunk=n_chunk),
        grid=(n_batch, t_tot // tm),
        in_specs=[
            pl.BlockSpec((1, tm, d), lambda b, t: (b, t, 0)),
            pl.BlockSpec((1, N_MOD, d), _mod_row_map(n_batch, n_ctx_tiles)),
            pl.BlockSpec((d, n), lambda b, t: (0, 0)),
        ],
        out_specs=pl.BlockSpec((1, tm, n), lambda b, t: (b, t, 0)),
        out_shape=jax.ShapeDtypeStruct((n_batch, t_tot, n), BF16),
        compiler_params=_params(("parallel", "parallel")),
        name="in_proj",
    )(xs, mods, w)


CONV_HALO = SUBLANES


def _l2_norm(x):
    return x * lax.rsqrt(jnp.sum(x * x, axis=-1, keepdims=True) + EPS)


def _gdn_proj_body(x_ref, xp_ref, xn_ref, mod_ref, w_ref, wxh_ref, wxl_ref, conv_ref, alog_ref, dtb_ref,
                   o_ref, col_ref, row_ref, *, ctx_len, qk_w, qkv_w, n_chunk, dk, n_hv):
    tm = x_ref.shape[1]
    r0 = pl.program_id(1) * tm
    t_tot = pl.num_programs(1) * tm
    scale = 1.0 + mod_ref[0, 1:2, :]
    shift = mod_ref[0, 0:1, :]
    h = x_ref[0] * scale + shift
    hb = h.astype(BF16)
    hb_ext = jnp.concatenate([xp_ref[0] * scale + shift, h, xn_ref[0] * scale + shift], axis=0).astype(BF16)
    keep_lo = jnp.where(jnp.logical_or(r0 == 0, r0 == ctx_len), 0.0, 1.0)
    keep_hi = jnp.where(jnp.logical_or(r0 + tm == ctx_len, r0 + tm == t_tot), 0.0, 1.0)
    conv_w = conv_ref.shape[0]
    pad = conv_w // 2
    n_ext = tm + 2 * CONV_HALO
    mid = slice(CONV_HALO, CONV_HALO + tm)
    tn = w_ref.shape[1] // n_chunk
    for j in range(n_chunk):
        c0 = j * tn
        if c0 >= qkv_w:
            o_ref[0, :, c0:c0 + tn] = _dot(hb, w_ref[:, c0:c0 + tn]).astype(o_ref.dtype)
            continue
        p = _dot(hb_ext, w_ref[:, c0:c0 + tn])
        ext = jnp.concatenate([p[:CONV_HALO] * keep_lo, p[mid], p[CONV_HALO + tm:] * keep_hi], axis=0)
        y = None
        for tap in range(conv_w):
            off = tap - pad
            xs = ext if off == 0 else pltpu.roll(ext, (-off) % n_ext, axis=0)
            term = xs[mid] * conv_ref[tap:tap + 1, c0:c0 + tn]
            y = term if y is None else y + term
        y = _silu(y)
        if c0 >= 2 * qk_w:
            o_ref[0, :, c0:c0 + tn] = y.astype(o_ref.dtype)
            continue
        for hh in range(tn // dk):
            yh = _l2_norm(y[:, hh * dk:(hh + 1) * dk])
            if c0 < qk_w:
                yh = yh * (dk ** -0.5)
            o_ref[0, :, c0 + hh * dk:c0 + (hh + 1) * dk] = yh.astype(o_ref.dtype)
    h_lo = (h - hb.astype(F32)).astype(BF16)
    raw = _dot(hb, wxh_ref[...]) + (_dot(hb, wxl_ref[...]) + _dot(h_lo, wxh_ref[...]))
    _gate_forms(raw, alog_ref[...], dtb_ref[...], col_ref, row_ref, n_hv)


def _gate_forms(raw, a_log, dt_bias, col_ref, row_ref, n_hv):
    beta = _sigmoid(raw)
    xs = raw + dt_bias
    g = -jnp.exp(a_log) * (jnp.maximum(xs, 0.0) + jnp.log1p(jnp.exp(-jnp.abs(xs))))
    r = lax.broadcasted_iota(jnp.int32, (PAIR, PAIR), 0)
    c = lax.broadcasted_iota(jnp.int32, (PAIR, PAIR), 1)
    same = (r < CHUNK) == (c < CHUNK)
    tri_f = jnp.where(jnp.logical_and(same, c <= r), 1.0, 0.0).astype(BF16)
    tri_r = jnp.where(jnp.logical_and(same, c >= r), 1.0, 0.0).astype(BF16)
    lane = lax.broadcasted_iota(jnp.int32, (PAIR, LANES), 1)
    for p in range(raw.shape[0] // PAIR):
        gp = g[p * PAIR:(p + 1) * PAIR]
        g1 = gp.astype(BF16)
        r1 = gp - g1.astype(F32)
        g2 = r1.astype(BF16)
        g3 = (r1 - g2.astype(F32)).astype(BF16)
        cum_f = _dot(tri_f, g1) + (_dot(tri_f, g2) + _dot(tri_f, g3))
        cum_r = _dot(tri_r, g1) + (_dot(tri_r, g2) + _dot(tri_r, g3))
        col = jnp.where(lane < 2 * n_hv, beta[p * PAIR:(p + 1) * PAIR],
                        jnp.where(lane < 3 * n_hv, cum_f, cum_r))
        col_ref[0, p * PAIR:(p + 1) * PAIR, :] = col
        row_ref[0, p] = col.T


def _gdn_in_proj(xs, mods, w, w_gate, conv, a_log_lanes, dt_bias_lanes, n_ctx_tiles, qk_w, n_hv, n_chunk):
    n_batch, t_tot, d = xs.shape
    n = w.shape[1]
    conv_w, qkv_w = conv.shape
    tm = TOKEN_TILE
    tn = n // n_chunk
    assert qk_w % tn == 0 and qkv_w % tn == 0 and tn % GDN_DK == 0 and tm % PAIR == 0
    halo_per_tile = tm // CONV_HALO
    last_halo = t_tot // CONV_HALO - 1
    w_gate_hi, w_gate_lo = _split2(w_gate)
    const = lambda shape: pl.BlockSpec(shape, lambda b, t: (0,) * len(shape))
    return pl.pallas_call(
        functools.partial(_gdn_proj_body, ctx_len=n_ctx_tiles * tm, qk_w=qk_w, qkv_w=qkv_w, n_chunk=n_chunk,
                          dk=GDN_DK, n_hv=n_hv),
        grid=(n_batch, t_tot // tm),
        in_specs=[
            pl.BlockSpec((1, tm, d), lambda b, t: (b, t, 0)),
            pl.BlockSpec((1, CONV_HALO, d), lambda b, t: (b, jnp.maximum(t * halo_per_tile - 1, 0), 0)),
            pl.BlockSpec((1, CONV_HALO, d), lambda b, t: (b, jnp.minimum((t + 1) * halo_per_tile, last_halo), 0)),
            pl.BlockSpec((1, N_MOD, d), _mod_row_map(n_batch, n_ctx_tiles)),
            const((d, n)), const((d, LANES)), const((d, LANES)), const((conv_w, qkv_w)),
            const((1, LANES)), const((1, LANES)),
        ],
        out_specs=[pl.BlockSpec((1, tm, n), lambda b, t: (b, t, 0)),
                   pl.BlockSpec((1, tm, LANES), lambda b, t: (b, t, 0)),
                   pl.BlockSpec((1, tm // PAIR, LANES, PAIR), lambda b, t: (b, t, 0, 0))],
        out_shape=[jax.ShapeDtypeStruct((n_batch, t_tot, n), BF16),
                   jax.ShapeDtypeStruct((n_batch, t_tot, LANES), F32),
                   jax.ShapeDtypeStruct((n_batch, t_tot // PAIR, LANES, PAIR), F32)],
        compiler_params=_params(("parallel", "parallel")),
        name="gdn_in_proj",
    )(xs, xs, xs, mods, w, w_gate_hi, w_gate_lo, conv, a_log_lanes, dt_bias_lanes)


STAGE_TILE = 256
HEADS_PER_STEP = 2
CHUNKS_PER_STEP = 4
SUPER = CHUNKS_PER_STEP * CHUNK


def _delta_body(q_ref, k_ref, v_ref, z_ref, colg_ref, rowg_ref, ng_ref,
                y_ref, o_s, st_s, cg_s, uw_s, qkd_s, kt2_s, *, ctx_len, n_hv, n_m):
    g_idx = pl.program_id(1)
    t_tot = q_ref.shape[1]
    dk = q_ref.shape[2] // n_m
    dv = v_ref.shape[2] // (2 * n_m)
    n_sup = t_tot // SUPER
    n_ctx_sup = ctx_len // SUPER

    lane_shift = (LANES - 2 * n_m * g_idx) % LANES

    def prep(i, carry):
        rows = pl.ds(pl.multiple_of(i * STAGE_TILE, STAGE_TILE), STAGE_TILE)
        for m in range(n_m):
            km = k_ref[0, rows, m * dk:(m + 1) * dk].astype(F32)
            for cc in range(STAGE_TILE // CHUNK):
                kc = km[cc * CHUNK:(cc + 1) * CHUNK]
                kt2_s[i * (STAGE_TILE // CHUNK) + cc, m] = jnp.concatenate([kc, kc], axis=0).T
        cg_s[rows, :] = pltpu.roll(colg_ref[0, rows, :], lane_shift, axis=1)
        return carry

    lax.fori_loop(0, t_tot // STAGE_TILE, prep, 0)
    o_s[...] = jnp.zeros_like(o_s)
    st_s[...] = jnp.zeros_like(st_s)
    uw_s[...] = jnp.zeros_like(uw_s)
    qkd_s[...] = jnp.zeros_like(qkd_s)

    r = lax.broadcasted_iota(jnp.int32, (CHUNK, LANES), 0)
    lane = lax.broadcasted_iota(jnp.int32, (CHUNK, LANES), 1)
    c = lane & (CHUNK - 1)
    left = lane < CHUNK
    left_row = lax.broadcasted_iota(jnp.int32, (1, LANES), 1) < CHUNK
    keep_left = jnp.where(left, 1.0, 0.0).astype(BF16)
    keep_right = jnp.where(left, 0.0, 1.0).astype(BF16)
    eye2 = jnp.where(c == r, 1.0, 0.0)
    n_lvl = CHUNK.bit_length() - 1
    lvl_mask = [jnp.logical_and((r >> (lg + 1)) == (c >> (lg + 1)), (r >> lg) != (c >> lg)) for lg in range(n_lvl)]
    incl = (c <= r, c >= r)
    strict = (c < r, c > r)
    zero_b = jnp.zeros((CHUNK, dv), BF16)

    def block_diag(y):
        return jnp.concatenate([y * keep_left, y * keep_right], axis=0)

    def reverse_super(i):
        return jnp.where(i < n_ctx_sup, n_ctx_sup - 1 - i, n_sup + n_ctx_sup - 1 - i)

    def make_units(i):
        sup = (i, reverse_super(i))
        units = []
        for m in range(n_m):
            for d in (0, 1):
                for seq in range(CHUNKS_PER_STEP):
                    cidx = seq if d == 0 else CHUNKS_PER_STEP - 1 - seq
                    units.append(dict(
                        m=m, d=d, seq=seq, half=cidx % 2, uid=(m * 2 + d) * CHUNKS_PER_STEP + seq,
                        pair=sup[d] * (SUPER // PAIR) + cidx // 2, chunk=sup[d] * CHUNKS_PER_STEP + cidx,
                        rows=pl.ds(pl.multiple_of(sup[d] * SUPER + cidx * CHUNK, CHUNK), CHUNK)))
        return units

    def load_gates(u):
        m, d, half = u["m"], u["d"], u["half"]
        cg = cg_s[u["rows"], :]
        b0, g0 = d * n_hv + 2 * m, (2 + d) * n_hv + 2 * m
        u["beta_c"] = (cg[:, b0:b0 + 1], cg[:, b0 + 1:b0 + 2])
        u["gc_c"] = (cg[:, g0:g0 + 1], cg[:, g0 + 1:g0 + 2])
        head0 = 2 * (n_m * g_idx + m)

        def packed_rows(base):
            rows = rowg_ref[0, u["pair"], pl.ds(base + head0, 2), :]
            a, b = rows[0:1], rows[1:2]
            if half == 0:
                return jnp.where(left_row, a, pltpu.roll(b, CHUNK, axis=1))
            return jnp.where(left_row, pltpu.roll(a, CHUNK, axis=1), b)

        u["beta_r"] = packed_rows(d * n_hv)
        u["gc_r"] = packed_rows((2 + d) * n_hv)

    def state_free_part(i, slot):
        units = make_units(i)
        for u in units:
            m = u["m"]
            load_gates(u)
            qb, kb = q_ref[0, u["rows"], m * dk:(m + 1) * dk], k_ref[0, u["rows"], m * dk:(m + 1) * dk]
            u["kb"] = kb
            u["gram"] = lax.dot_general(jnp.concatenate([kb, qb], axis=0), jnp.concatenate([kb, kb], axis=0),
                                        (((1,), (1,)), ((), ())), preferred_element_type=F32)
        yield
        for u in units:
            d = u["d"]
            gc_c2 = jnp.where(left, u["gc_c"][0], u["gc_c"][1])
            beta_c2 = jnp.where(left, u["beta_c"][0], u["beta_c"][1])
            decay = jnp.where(incl[d], jnp.exp(jnp.where(incl[d], gc_c2 - u["gc_r"], 0.0)), 0.0)
            a = jnp.where(strict[d], u["gram"][0:CHUNK] * decay, 0.0) * beta_c2
            qkd_s[slot, u["uid"]] = (u["gram"][CHUNK:2 * CHUNK] * decay).astype(BF16)
            u["a"] = a
            u["x"] = eye2 - jnp.where(lvl_mask[0], a, 0.0)
        yield
        for lg in range(1, n_lvl):
            for u in units:
                u["xb"] = u["x"].astype(BF16)
                off = jnp.where(lvl_mask[lg], u["a"], 0.0).astype(BF16)
                u["z"] = _dot(off, block_diag(u["xb"]))
            yield
            for u in units:
                u["x"] = u["x"] - _dot(u["xb"], block_diag(u["z"].astype(BF16)))
            yield
        for u in units:
            m = u["m"]
            tb = u["x"] * u["beta_r"]
            tbe = tb * jnp.exp(u["gc_r"])
            v2 = v_ref[0, u["rows"], 2 * m * dv:2 * (m + 1) * dv]
            v_bd = jnp.concatenate([jnp.concatenate([v2[:, :dv], zero_b], axis=1),
                                    jnp.concatenate([zero_b, v2[:, dv:]], axis=1)], axis=0)
            k_bd = jnp.concatenate([jnp.concatenate([u["kb"], zero_b], axis=1),
                                    jnp.concatenate([zero_b, u["kb"]], axis=1)], axis=0)
            uw_s[slot, u["uid"], :, :2 * dv] = _dot(tb.astype(BF16), v_bd)
            uw_s[slot, u["uid"], :, 2 * dv:] = _dot(tbe.astype(BF16), k_bd)
        yield

    def recurrence_part(i, slot):
        units = make_units(i)
        for seq in range(CHUNKS_PER_STEP):
            now = [u for u in units if u["seq"] == seq]
            for u in now:
                m = u["m"]
                load_gates(u)
                uw = uw_s[slot, u["uid"]]
                u["u"] = uw[:, :2 * dv]
                u["s2"] = st_s[2 * m + u["d"]]
                lhs = jnp.concatenate([uw[:, 2 * dv:3 * dv].astype(BF16), uw[:, 3 * dv:].astype(BF16),
                                       q_ref[0, u["rows"], m * dk:(m + 1) * dk]], axis=0)
                u["ws"] = _dot(lhs, u["s2"].astype(BF16))
            yield
            for u in now:
                d, ws = u["d"], u["ws"]
                vn0 = (u["u"][:, :dv] - ws[0:CHUNK, :dv]).astype(BF16)
                vn1 = (u["u"][:, dv:] - ws[CHUNK:2 * CHUNK, dv:]).astype(BF16)
                vn_bd = jnp.concatenate([jnp.concatenate([vn0, zero_b], axis=1),
                                         jnp.concatenate([zero_b, vn1], axis=1)], axis=0)
                last = (CHUNK - 1, 2 * CHUNK - 1) if d == 0 else (0, CHUNK)
                u["gl"] = [u["gc_r"][:, l:l + 1] for l in last]
                gl2 = jnp.where(left_row, u["gl"][0], u["gl"][1])
                kdt = kt2_s[u["chunk"], u["m"]] * jnp.exp(gl2 - u["gc_r"])
                lhs = jnp.concatenate([qkd_s[slot, u["uid"]], kdt.astype(BF16)], axis=0)
                u["ov"] = _dot(lhs, vn_bd)
            yield
            for u in now:
                m, ws, ov = u["m"], u["ws"], u["ov"]
                qs = jnp.concatenate([ws[2 * CHUNK:, :dv] * jnp.exp(u["gc_c"][0]),
                                      ws[2 * CHUNK:, dv:] * jnp.exp(u["gc_c"][1])], axis=1)
                o_s[u["rows"], 2 * m * dv:2 * (m + 1) * dv] += qs + ov[0:CHUNK]
                egl = jnp.concatenate([jnp.broadcast_to(jnp.exp(u["gl"][0]), (1, dv)),
                                       jnp.broadcast_to(jnp.exp(u["gl"][1]), (1, dv))], axis=1)
                st_s[2 * m + u["d"]] = u["s2"] * egl + ov[CHUNK:]
            yield

    def step(i, carry):
        slot = i & 1
        ahead = state_free_part(jnp.minimum(i, n_sup - 1), slot)
        behind = recurrence_part(jnp.maximum(i - 1, 0), 1 - slot)
        live = [ahead, behind]
        while live:
            live = [g for g in live if next(g, "done") != "done"]
        return carry

    lax.fori_loop(0, n_sup + 1, step, 0)

    def finish(i, carry):
        rows = pl.ds(pl.multiple_of(i * STAGE_TILE, STAGE_TILE), STAGE_TILE)
        for j in range(2 * n_m):
            o = o_s[rows, j * dv:(j + 1) * dv]
            z = z_ref[0, rows, j * dv:(j + 1) * dv].astype(F32)
            y = o * lax.rsqrt(jnp.mean(o * o, axis=-1, keepdims=True) + EPS) * ng_ref[...]
            y_ref[0, rows, j * dv:(j + 1) * dv] = (y * _silu(z)).astype(y_ref.dtype)
        return carry

    lax.fori_loop(0, t_tot // STAGE_TILE, finish, 0)


def _gdn_delta(p_main, colg, rowg, norm_g, ctx_len, n_hk, n_hv):
    n_batch, t_tot, _ = p_main.shape
    dk = GDN_DK
    dv = norm_g.shape[-1]
    assert n_hv == 2 * n_hk and dk == dv == LANES and n_hk % HEADS_PER_STEP == 0
    assert ctx_len % SUPER == 0 and t_tot % SUPER == 0 and SUPER % STAGE_TILE == 0
    n_m = HEADS_PER_STEP
    qb = n_m * dk
    vb = 2 * n_m * dv
    k_blk0 = n_hk // n_m
    v_blk0 = (2 * n_hk * dk) // vb
    z_blk0 = (2 * n_hk * dk + n_hv * dv) // vb
    n_pairs = t_tot // PAIR
    n_units = 2 * n_m * CHUNKS_PER_STEP
    return pl.pallas_call(
        functools.partial(_delta_body, ctx_len=ctx_len, n_hv=n_hv, n_m=n_m),
        grid=(n_batch, n_hk // n_m),
        in_specs=[
            pl.BlockSpec((1, t_tot, qb), lambda b, h: (b, 0, h)),
            pl.BlockSpec((1, t_tot, qb), lambda b, h: (b, 0, k_blk0 + h)),
            pl.BlockSpec((1, t_tot, vb), lambda b, h: (b, 0, v_blk0 + h)),
            pl.BlockSpec((1, t_tot, vb), lambda b, h: (b, 0, z_blk0 + h)),
            pl.BlockSpec((1, t_tot, LANES), lambda b, h: (b, 0, 0)),
            pl.BlockSpec((1, n_pairs, LANES, PAIR), lambda b, h: (b, 0, 0, 0)),
            pl.BlockSpec((1, dv), lambda b, h: (0, 0)),
        ],
        out_specs=pl.BlockSpec((1, t_tot, vb), lambda b, h: (b, 0, h)),
        out_shape=jax.ShapeDtypeStruct((n_batch, t_tot, n_hv * dv), BF16),
        scratch_shapes=[
            pltpu.VMEM((t_tot, vb), F32),
            pltpu.VMEM((2 * n_m, dk, 2 * dv), F32),
            pltpu.VMEM((t_tot, LANES), F32),
            pltpu.VMEM((2, n_units, CHUNK, 2 * dv + 2 * dk), F32),
            pltpu.VMEM((2, n_units, CHUNK, 2 * CHUNK), BF16),
            pltpu.VMEM((t_tot // CHUNK, n_m, dk, 2 * CHUNK), F32),
        ],
        compiler_params=_params(("parallel", "parallel")),
        name="gdn_delta",
    )(p_main, p_main, p_main, p_main, colg, rowg, norm_g)


def _rope(x, cos, sin_a, sin_b):
    quarter = ATT_DH // 4
    return x * cos + pltpu.roll(x, ATT_DH - quarter, axis=1) * sin_a + pltpu.roll(x, quarter, axis=1) * sin_b


def _rms_norm(x, g):
    return x * lax.rsqrt(jnp.mean(x * x, axis=-1, keepdims=True) + EPS) * g


def _attn_body(q_ref, k_ref, v_ref, cos_ref, sa_ref, sb_ref, qn_ref, kn_ref, o_ref, kt_s, v1_s,
               *, ctx_len, n_group, skip_tiles):
    qi = pl.program_id(2)
    tq = q_ref.shape[1]
    t_tot = k_ref.shape[1]
    dh = k_ref.shape[2]

    @pl.when(qi == 0)
    def _():
        k = _rms_norm(k_ref[0].astype(F32), kn_ref[...])
        k = _rope(k, cos_ref[...], sa_ref[...], sb_ref[...])
        kt_s[...] = k.T.astype(kt_s.dtype)
        v1_s[:, :dh] = v_ref[0]
        v1_s[:, dh:] = jnp.ones((t_tot, dh), v1_s.dtype)

    r0 = pl.multiple_of((qi + skip_tiles) * tq, tq)
    cos = cos_ref[pl.ds(r0, tq), :]
    sin_a = sa_ref[pl.ds(r0, tq), :]
    sin_b = sb_ref[pl.ds(r0, tq), :]

    def attend(n_keys):
        def scores(g):
            q = _rms_norm(q_ref[0, :, g * dh:(g + 1) * dh].astype(F32), qn_ref[...])
            q = _rope(q, cos, sin_a, sin_b) * (dh ** -0.5 * LOG2E)
            return _dot(q.astype(BF16), kt_s[:, :n_keys])

        s_next = scores(0)
        for g in range(n_group):
            s = s_next
            if g + 1 < n_group:
                s_next = scores(g + 1)
            p = jnp.exp2(s - jnp.max(s, axis=-1, keepdims=True))
            num_den = _dot(p.astype(BF16), v1_s[:n_keys, :])
            o_ref[0, :, g * dh:(g + 1) * dh] = (num_den[:, :dh] / num_den[:, dh:]).astype(o_ref.dtype)

    @pl.when(r0 < ctx_len)
    def _():
        attend(ctx_len)

    @pl.when(r0 >= ctx_len)
    def _():
        attend(t_tot)


def _attention(p_qkv, tables, q_norm, k_norm, ctx_len, n_hq, n_hkv, skip_tiles=0):
    n_batch, t_tot, _ = p_qkv.shape
    dh = ATT_DH
    n_group = n_hq // n_hkv
    tq = TOKEN_TILE
    cos, sin_a, sin_b = tables
    full = lambda shape: pl.BlockSpec(shape, lambda b, h, q: (0,) * len(shape))
    return pl.pallas_call(
        functools.partial(_attn_body, ctx_len=ctx_len, n_group=n_group, skip_tiles=skip_tiles),
        grid=(n_batch, n_hkv, t_tot // tq - skip_tiles),
        in_specs=[
            pl.BlockSpec((1, tq, n_group * dh), lambda b, h, q: (b, q + skip_tiles, h)),
            pl.BlockSpec((1, t_tot, dh), lambda b, h, q: (b, 0, n_hq + h)),
            pl.BlockSpec((1, t_tot, dh), lambda b, h, q: (b, 0, n_hq + n_hkv + h)),
            full((t_tot, dh)), full((t_tot, dh)), full((t_tot, dh)),
            full((1, dh)), full((1, dh)),
        ],
        out_specs=pl.BlockSpec((1, tq, n_group * dh), lambda b, h, q: (b, q, h)),
        out_shape=jax.ShapeDtypeStruct((n_batch, t_tot - skip_tiles * tq, n_hq * dh), BF16),
        scratch_shapes=[pltpu.VMEM((dh, t_tot), BF16), pltpu.VMEM((t_tot, 2 * dh), BF16)],
        compiler_params=_params(("parallel", "parallel", "arbitrary")),
        name="attention",
    )(p_qkv, p_qkv, p_qkv, cos, sin_a, sin_b, q_norm, k_norm)


def _rope_tables(ctx_len, seq_len):
    rows = seq_len // GRID_W
    row = jnp.repeat(jnp.arange(rows), GRID_W).astype(F32)
    col = jnp.tile(jnp.arange(GRID_W), rows).astype(F32)
    n_freq = ATT_DH // 4
    freqs = ROPE_THETA ** (-jnp.arange(n_freq, dtype=F32) / n_freq)
    ang_r = row[:, None] * freqs
    ang_c = col[:, None] * freqs
    ang = jnp.concatenate([ang_r, ang_r, ang_c, ang_c], axis=-1)
    cos = jnp.concatenate([jnp.ones((ctx_len, ATT_DH), F32), jnp.cos(ang)], axis=0)
    sin = jnp.concatenate([jnp.zeros((ctx_len, ATT_DH), F32), jnp.sin(ang)], axis=0)
    first_half = (jnp.arange(ATT_DH) % (2 * n_freq)) < n_freq
    return cos, jnp.where(first_half, -sin, 0.0), jnp.where(first_half, 0.0, sin)


def _mix_ffn_body(y_ref, wo_ref, x_ref, mod_ref, ln_ref, win_ref, wout_ref, o_ref, *, n_chunk, alpha):
    delta = _dot(y_ref[0], wo_ref[...])
    x1 = _layer_norm(alpha * x_ref[0] + mod_ref[0, 2:3, :] * delta, ln_ref[0:1, :], ln_ref[1:2, :])
    hb = (x1 * (1.0 + mod_ref[0, 4:5, :]) + mod_ref[0, 3:4, :]).astype(BF16)
    d_ff = wout_ref.shape[0]
    tc = d_ff // n_chunk
    acc = None
    for c in range(n_chunk):
        gate = _dot(hb, win_ref[:, c * tc:(c + 1) * tc])
        up = _dot(hb, win_ref[:, d_ff + c * tc:d_ff + (c + 1) * tc])
        part = _dot((_silu(gate) * up).astype(BF16), wout_ref[c * tc:(c + 1) * tc, :])
        acc = part if acc is None else acc + part
    o_ref[0] = _layer_norm(alpha * x1 + mod_ref[0, 5:6, :] * acc, ln_ref[2:3, :], ln_ref[3:4, :])


def _mix_ffn(y, w_o, xs, mods, ln, w_in, w_out, n_ctx_tiles, alpha, skip_tiles=0):
    n_batch, t_tot, d = xs.shape
    k = y.shape[-1]
    d_ff = w_out.shape[0]
    tm = TOKEN_TILE
    n_chunk = 2 if d_ff % (2 * LANES) == 0 else 1
    row_map = _mod_row_map(n_batch, n_ctx_tiles)
    y_skip = skip_tiles - (t_tot - y.shape[1]) // tm
    resident = lambda shape: pl.BlockSpec(shape, lambda b, t: (0,) * len(shape), pipeline_mode=pl.Buffered(1))
    return pl.pallas_call(
        functools.partial(_mix_ffn_body, n_chunk=n_chunk, alpha=alpha),
        grid=(n_batch, t_tot // tm - skip_tiles),
        in_specs=[
            pl.BlockSpec((1, tm, k), lambda b, t: (b, t + y_skip, 0)),
            resident((k, d)),
            pl.BlockSpec((1, tm, d), lambda b, t: (b, t + skip_tiles, 0)),
            pl.BlockSpec((1, N_MOD, d), lambda b, t: row_map(b, t + skip_tiles)),
            resident(ln.shape),
            resident((d, 2 * d_ff)),
            resident((d_ff, d)),
        ],
        out_specs=pl.BlockSpec((1, tm, d), lambda b, t: (b, t, 0)),
        out_shape=jax.ShapeDtypeStruct((n_batch, t_tot - skip_tiles * tm, d), F32),
        compiler_params=_params(("parallel", "parallel")),
        name="mix_ffn",
    )(y, w_o, xs, mods, ln, w_in, w_out)


def kernel(x, c, ctx, c_ctx, w_mod, b_mod, ln_g, ln_b, w_ffn_in, w_ffn_out, gdn_w_in, gdn_conv, gdn_a_log,
           gdn_dt_bias, gdn_norm_g, gdn_w_out, attn_w_qkv, attn_q_norm, attn_k_norm, attn_w_out):
    n_batch, seq_len, d = x.shape
    ctx_len = ctx.shape[1]
    depth = w_mod.shape[0]
    alpha = (2 * depth) ** 0.25
    n_hv = gdn_a_log.shape[-1]
    dv = gdn_norm_g.shape[-1]
    v_w = n_hv * dv
    qkv_w = gdn_conv.shape[-1]
    n_hk = (qkv_w - v_w) // (2 * GDN_DK)
    n_hq = attn_w_out.shape[1] // ATT_DH
    n_hkv = (attn_w_qkv.shape[-1] // ATT_DH - n_hq) // 2
    assert ctx_len % TOKEN_TILE == 0 and seq_len % TOKEN_TILE == 0 and seq_len % GRID_W == 0
    assert 4 * n_hv <= LANES and d % LANES == 0
    n_ctx_tiles = ctx_len // TOKEN_TILE

    xs = jnp.concatenate([ctx, x], axis=1)

    mp = -(-(n_batch + 1) // SUBLANES) * SUBLANES
    cond = jnp.concatenate([c, c_ctx[None, :], jnp.zeros((mp - n_batch - 1, d), F32)], axis=0)
    mods = _modulation(cond, w_mod, b_mod).reshape(depth, mp, N_MOD, d)
    tables = _rope_tables(ctx_len, seq_len)

    for i in range(depth):
        j = i // 2
        m = mods[i]
        skip = n_ctx_tiles if i == depth - 1 else 0
        if i % 2 == 0:
            w_in = gdn_w_in[j]
            gate_w = jnp.pad(w_in[:, qkv_w + v_w:], ((0, 0), (0, LANES - 4 * n_hv)))
            lanes_pad = (0, LANES - 4 * n_hv)
            a_log = jnp.pad(jnp.concatenate([jnp.zeros((2 * n_hv,), F32), gdn_a_log[j].reshape(-1)]), lanes_pad)
            dt_b = jnp.pad(jnp.concatenate([jnp.zeros((2 * n_hv,), F32), gdn_dt_bias[j].reshape(-1)]), lanes_pad)
            p_main, colg, rowg = _gdn_in_proj(xs, m, w_in[:, :qkv_w + v_w].astype(BF16), gate_w, gdn_conv[j],
                                              a_log[None, :], dt_b[None, :], n_ctx_tiles,
                                              qk_w=n_hk * GDN_DK, n_hv=n_hv, n_chunk=6)
            y = _gdn_delta(p_main, colg, rowg, gdn_norm_g[j][None, :], ctx_len, n_hk, n_hv)
            w_o = gdn_w_out[j]
        else:
            p_qkv = _in_proj(xs, m, attn_w_qkv[j].astype(BF16), n_ctx_tiles, n_chunk=3)
            y = _attention(p_qkv, tables, attn_q_norm[j][None, :], attn_k_norm[j][None, :], ctx_len, n_hq, n_hkv,
                           skip_tiles=skip)
            w_o = attn_w_out[j]
        ln = jnp.stack([ln_g[i, 0], ln_b[i, 0], ln_g[i, 1], ln_b[i, 1]])
        xs = _mix_ffn(y, w_o.astype(BF16), xs, m, ln, w_ffn_in[i].astype(BF16), w_ffn_out[i].astype(BF16),
                      n_ctx_tiles, alpha, skip_tiles=skip)
    return xs
```

```python
import functools
import math

import jax
import jax.numpy as jnp
from jax import lax
from jax.experimental import pallas as pl
from jax.experimental.pallas import tpu as pltpu

F32 = jnp.float32
BF16 = jnp.bfloat16

N_MOD = 6
EPS = 1e-6
GDN_DK = 128
CHUNK = 64
ATT_DH = 128
GRID_W = 64
ROPE_THETA = 10000.0

LANES = 128
SUBLANES = 8
PAIR = 2 * CHUNK
VMEM_LIMIT_BYTES = 56 * 1024 * 1024
TOKEN_TILE = 256
NEG_INF = -1e30
LOG2E = math.log2(math.e)

assert PAIR == LANES


def _params(sem):
    return pltpu.CompilerParams(dimension_semantics=sem, vmem_limit_bytes=VMEM_LIMIT_BYTES)


def _dot(a, b):
    return jnp.dot(a, b, preferred_element_type=F32)


def _split2(a):
    hi = a.astype(BF16)
    return hi, (a - hi.astype(F32)).astype(BF16)


def _dot_hi(a, b):
    a1, a2 = _split2(a)
    b1, b2 = _split2(b)
    return _dot(a1, b1) + (_dot(a1, b2) + _dot(a2, b1))


def _sigmoid(x):
    return 1.0 / (1.0 + jnp.exp(-x))


def _silu(x):
    return x * _sigmoid(x)


def _layer_norm(r, g, b):
    mu = jnp.mean(r, axis=-1, keepdims=True)
    rc = r - mu
    var = jnp.mean(rc * rc, axis=-1, keepdims=True)
    return rc * lax.rsqrt(var + EPS) * g + b


def _mod_body(cond_ref, w_ref, b_ref, o_ref):
    o_ref[0] = _dot_hi(_silu(cond_ref[...]), w_ref[0]) + b_ref[0]


def _modulation(cond, w_mod, b_mod):
    n_layer, d, n = w_mod.shape
    mp = cond.shape[0]
    tn = n // 4
    return pl.pallas_call(
        _mod_body,
        grid=(n_layer, n // tn),
        in_specs=[
            pl.BlockSpec((mp, d), lambda l, j: (0, 0)),
            pl.BlockSpec((1, d, tn), lambda l, j: (l, 0, j)),
            pl.BlockSpec((1, 1, tn), lambda l, j: (l, 0, j)),
        ],
        out_specs=pl.BlockSpec((1, mp, tn), lambda l, j: (l, 0, j)),
        out_shape=jax.ShapeDtypeStruct((n_layer, mp, n), F32),
        compiler_params=_params(("parallel", "parallel")),
        name="modulation",
    )(cond, w_mod, b_mod.reshape(n_layer, 1, n))


def _mod_row_map(n_batch, n_ctx_tiles):
    return lambda b, t: (jnp.where(t < n_ctx_tiles, n_batch, b), 0, 0)


def _in_proj_body(x_ref, mod_ref, w_ref, o_ref, *, n_chunk):
    hb = (x_ref[0] * (1.0 + mod_ref[0, 1:2, :]) + mod_ref[0, 0:1, :]).astype(BF16)
    tn = w_ref.shape[1] // n_chunk
    for j in range(n_chunk):
        o_ref[0, :, j * tn:(j + 1) * tn] = _dot(hb, w_ref[:, j * tn:(j + 1) * tn]).astype(o_ref.dtype)


def _in_proj(xs, mods, w, n_ctx_tiles, n_chunk):
    n_batch, t_tot, d = xs.shape
    n = w.shape[1]
    tm = TOKEN_TILE
    return pl.pallas_call(
        functools.partial(_in_proj_body, n_chunk=n_chunk),
        grid=(n_batch, t_tot // tm),
        in_specs=[
            pl.BlockSpec((1, tm, d), lambda b, t: (b, t, 0)),
            pl.BlockSpec((1, N_MOD, d), _mod_row_map(n_batch, n_ctx_tiles)),
            pl.BlockSpec((d, n), lambda b, t: (0, 0)),
        ],
        out_specs=pl.BlockSpec((1, tm, n), lambda b, t: (b, t, 0)),
        out_shape=jax.ShapeDtypeStruct((n_batch, t_tot, n), BF16),
        compiler_params=_params(("parallel", "parallel")),
        name="in_proj",
    )(xs, mods, w)


CONV_HALO = SUBLANES


def _l2_norm(x):
    return x * lax.rsqrt(jnp.sum(x * x, axis=-1, keepdims=True) + EPS)


def _gdn_proj_body(x_ref, xp_ref, xn_ref, mod_ref, w_ref, wxh_ref, wxl_ref, conv_ref, alog_ref, dtb_ref,
                   o_ref, col_ref, row_ref, *, ctx_len, qk_w, qkv_w, n_chunk, dk, n_hv):
    tm = x_ref.shape[1]
    r0 = pl.program_id(1) * tm
    t_tot = pl.num_programs(1) * tm
    scale = 1.0 + mod_ref[0, 1:2, :]
    shift = mod_ref[0, 0:1, :]
    h = x_ref[0] * scale + shift
    hb = h.astype(BF16)
    hb_ext = jnp.concatenate([xp_ref[0] * scale + shift, h, xn_ref[0] * scale + shift], axis=0).astype(BF16)
    keep_lo = jnp.where(jnp.logical_or(r0 == 0, r0 == ctx_len), 0.0, 1.0)
    keep_hi = jnp.where(jnp.logical_or(r0 + tm == ctx_len, r0 + tm == t_tot), 0.0, 1.0)
    conv_w = conv_ref.shape[0]
    pad = conv_w // 2
    n_ext = tm + 2 * CONV_HALO
    mid = slice(CONV_HALO, CONV_HALO + tm)
    tn = w_ref.shape[1] // n_chunk
    for j in range(n_chunk):
        c0 = j * tn
        if c0 >= qkv_w:
            o_ref[0, :, c0:c0 + tn] = _dot(hb, w_ref[:, c0:c0 + tn]).astype(o_ref.dtype)
            continue
        p = _dot(hb_ext, w_ref[:, c0:c0 + tn])
        ext = jnp.concatenate([p[:CONV_HALO] * keep_lo, p[mid], p[CONV_HALO + tm:] * keep_hi], axis=0)
        y = None
        for tap in range(conv_w):
            off = tap - pad
            xs = ext if off == 0 else pltpu.roll(ext, (-off) % n_ext, axis=0)
            term = xs[mid] * conv_ref[tap:tap + 1, c0:c0 + tn]
            y = term if y is None else y + term
        y = _silu(y)
        if c0 >= 2 * qk_w:
            o_ref[0, :, c0:c0 + tn] = y.astype(o_ref.dtype)
            continue
        for hh in range(tn // dk):
            yh = _l2_norm(y[:, hh * dk:(hh + 1) * dk])
            if c0 < qk_w:
                yh = yh * (dk ** -0.5)
            o_ref[0, :, c0 + hh * dk:c0 + (hh + 1) * dk] = yh.astype(o_ref.dtype)
    h_lo = (h - hb.astype(F32)).astype(BF16)
    raw = _dot(hb, wxh_ref[...]) + (_dot(hb, wxl_ref[...]) + _dot(h_lo, wxh_ref[...]))
    _gate_forms(raw, alog_ref[...], dtb_ref[...], col_ref, row_ref, n_hv)


def _gate_forms(raw, a_log, dt_bias, col_ref, row_ref, n_hv):
    beta = _sigmoid(raw)
    xs = raw + dt_bias
    g = -jnp.exp(a_log) * (jnp.maximum(xs, 0.0) + jnp.log1p(jnp.exp(-jnp.abs(xs))))
    r = lax.broadcasted_iota(jnp.int32, (PAIR, PAIR), 0)
    c = lax.broadcasted_iota(jnp.int32, (PAIR, PAIR), 1)
    same = (r < CHUNK) == (c < CHUNK)
    tri_f = jnp.where(jnp.logical_and(same, c <= r), 1.0, 0.0).astype(BF16)
    tri_r = jnp.where(jnp.logical_and(same, c >= r), 1.0, 0.0).astype(BF16)
    lane = lax.broadcasted_iota(jnp.int32, (PAIR, LANES), 1)
    for p in range(raw.shape[0] // PAIR):
        gp = g[p * PAIR:(p + 1) * PAIR]
        g1 = gp.astype(BF16)
        r1 = gp - g1.astype(F32)
        g2 = r1.astype(BF16)
        g3 = (r1 - g2.astype(F32)).astype(BF16)
        cum_f = _dot(tri_f, g1) + (_dot(tri_f, g2) + _dot(tri_f, g3))
        cum_r = _dot(tri_r, g1) + (_dot(tri_r, g2) + _dot(tri_r, g3))
        col = jnp.where(lane < 2 * n_hv, beta[p * PAIR:(p + 1) * PAIR],
                        jnp.where(lane < 3 * n_hv, cum_f, cum_r))
        col_ref[0, p * PAIR:(p + 1) * PAIR, :] = col
        row_ref[0, p] = col.T


def _gdn_in_proj(xs, mods, w, w_gate, conv, a_log_lanes, dt_bias_lanes, n_ctx_tiles, qk_w, n_hv, n_chunk):
    n_batch, t_tot, d = xs.shape
    n = w.shape[1]
    conv_w, qkv_w = conv.shape
    tm = TOKEN_TILE
    tn = n // n_chunk
    assert qk_w % tn == 0 and qkv_w % tn == 0 and tn % GDN_DK == 0 and tm % PAIR == 0
    halo_per_tile = tm // CONV_HALO
    last_halo = t_tot // CONV_HALO - 1
    w_gate_hi, w_gate_lo = _split2(w_gate)
    const = lambda shape: pl.BlockSpec(shape, lambda b, t: (0,) * len(shape))
    return pl.pallas_call(
        functools.partial(_gdn_proj_body, ctx_len=n_ctx_tiles * tm, qk_w=qk_w, qkv_w=qkv_w, n_chunk=n_chunk,
                          dk=GDN_DK, n_hv=n_hv),
        grid=(n_batch, t_tot // tm),
        in_specs=[
            pl.BlockSpec((1, tm, d), lambda b, t: (b, t, 0)),
            pl.BlockSpec((1, CONV_HALO, d), lambda b, t: (b, jnp.maximum(t * halo_per_tile - 1, 0), 0)),
            pl.BlockSpec((1, CONV_HALO, d), lambda b, t: (b, jnp.minimum((t + 1) * halo_per_tile, last_halo), 0)),
            pl.BlockSpec((1, N_MOD, d), _mod_row_map(n_batch, n_ctx_tiles)),
            const((d, n)), const((d, LANES)), const((d, LANES)), const((conv_w, qkv_w)),
            const((1, LANES)), const((1, LANES)),
        ],
        out_specs=[pl.BlockSpec((1, tm, n), lambda b, t: (b, t, 0)),
                   pl.BlockSpec((1, tm, LANES), lambda b, t: (b, t, 0)),
                   pl.BlockSpec((1, tm // PAIR, LANES, PAIR), lambda b, t: (b, t, 0, 0))],
        out_shape=[jax.ShapeDtypeStruct((n_batch, t_tot, n), BF16),
                   jax.ShapeDtypeStruct((n_batch, t_tot, LANES), F32),
                   jax.ShapeDtypeStruct((n_batch, t_tot // PAIR, LANES, PAIR), F32)],
        compiler_params=_params(("parallel", "parallel")),
        name="gdn_in_proj",
    )(xs, xs, xs, mods, w, w_gate_hi, w_gate_lo, conv, a_log_lanes, dt_bias_lanes)


STAGE_TILE = 256
HEADS_PER_STEP = 2
CHUNKS_PER_STEP = 4
SUPER = CHUNKS_PER_STEP * CHUNK


def _delta_body(q_ref, k_ref, v_ref, colg_ref, rowg_ref,
                y_ref, o_s, st_s, cg_s, uw_s, qkd_s, kt2_s, *, ctx_len, n_hv, n_m):
    g_idx = pl.program_id(1)
    t_tot = q_ref.shape[1]
    dk = q_ref.shape[2] // n_m
    dv = v_ref.shape[2] // (2 * n_m)
    n_sup = t_tot // SUPER
    n_ctx_sup = ctx_len // SUPER

    lane_shift = (LANES - 2 * n_m * g_idx) % LANES

    def prep(i, carry):
        rows = pl.ds(pl.multiple_of(i * STAGE_TILE, STAGE_TILE), STAGE_TILE)
        for m in range(n_m):
            km = k_ref[0, rows, m * dk:(m + 1) * dk].astype(F32)
            for cc in range(STAGE_TILE // CHUNK):
                kc = km[cc * CHUNK:(cc + 1) * CHUNK]
                kt2_s[i * (STAGE_TILE // CHUNK) + cc, m] = jnp.concatenate([kc, kc], axis=0).T
        cg_s[rows, :] = pltpu.roll(colg_ref[0, rows, :], lane_shift, axis=1)
        return carry

    lax.fori_loop(0, t_tot // STAGE_TILE, prep, 0)
    o_s[...] = jnp.zeros_like(o_s)
    st_s[...] = jnp.zeros_like(st_s)
    uw_s[...] = jnp.zeros_like(uw_s)
    qkd_s[...] = jnp.zeros_like(qkd_s)

    r = lax.broadcasted_iota(jnp.int32, (CHUNK, LANES), 0)
    lane = lax.broadcasted_iota(jnp.int32, (CHUNK, LANES), 1)
    c = lane & (CHUNK - 1)
    left = lane < CHUNK
    left_row = lax.broadcasted_iota(jnp.int32, (1, LANES), 1) < CHUNK
    keep_left = jnp.where(left, 1.0, 0.0).astype(BF16)
    keep_right = jnp.where(left, 0.0, 1.0).astype(BF16)
    eye2 = jnp.where(c == r, 1.0, 0.0)
    n_lvl = CHUNK.bit_length() - 1
    lvl_mask = [jnp.logical_and((r >> (lg + 1)) == (c >> (lg + 1)), (r >> lg) != (c >> lg)) for lg in range(n_lvl)]
    incl = (c <= r, c >= r)
    strict = (c < r, c > r)
    zero_b = jnp.zeros((CHUNK, dv), BF16)

    def block_diag(y):
        return jnp.concatenate([y * keep_left, y * keep_right], axis=0)

    def reverse_super(i):
        return jnp.where(i < n_ctx_sup, n_ctx_sup - 1 - i, n_sup + n_ctx_sup - 1 - i)

    def make_units(i):
        sup = (i, reverse_super(i))
        units = []
        for m in range(n_m):
            for d in (0, 1):
                for seq in range(CHUNKS_PER_STEP):
                    cidx = seq if d == 0 else CHUNKS_PER_STEP - 1 - seq
                    units.append(dict(
                        m=m, d=d, seq=seq, half=cidx % 2, uid=(m * 2 + d) * CHUNKS_PER_STEP + seq,
                        pair=sup[d] * (SUPER // PAIR) + cidx // 2, chunk=sup[d] * CHUNKS_PER_STEP + cidx,
                        rows=pl.ds(pl.multiple_of(sup[d] * SUPER + cidx * CHUNK, CHUNK), CHUNK)))
        return units

    def load_gates(u):
        m, d, half = u["m"], u["d"], u["half"]
        cg = cg_s[u["rows"], :]
        b0, g0 = d * n_hv + 2 * m, (2 + d) * n_hv + 2 * m
        u["beta_c"] = (cg[:, b0:b0 + 1], cg[:, b0 + 1:b0 + 2])
        u["gc_c"] = (cg[:, g0:g0 + 1], cg[:, g0 + 1:g0 + 2])
        head0 = 2 * (n_m * g_idx + m)

        def packed_rows(base):
            rows = rowg_ref[0, u["pair"], pl.ds(base + head0, 2), :]
            a, b = rows[0:1], rows[1:2]
            if half == 0:
                return jnp.where(left_row, a, pltpu.roll(b, CHUNK, axis=1))
            return jnp.where(left_row, pltpu.roll(a, CHUNK, axis=1), b)

        u["beta_r"] = packed_rows(d * n_hv)
        u["gc_r"] = packed_rows((2 + d) * n_hv)

    def state_free_part(i, slot):
        units = make_units(i)
        for u in units:
            m = u["m"]
            load_gates(u)
            qb, kb = q_ref[0, u["rows"], m * dk:(m + 1) * dk], k_ref[0, u["rows"], m * dk:(m + 1) * dk]
            u["kb"] = kb
            u["gram"] = lax.dot_general(jnp.concatenate([kb, qb], axis=0), jnp.concatenate([kb, kb], axis=0),
                                        (((1,), (1,)), ((), ())), preferred_element_type=F32)
        yield
        for u in units:
            d = u["d"]
            gc_c2 = jnp.where(left, u["gc_c"][0], u["gc_c"][1])
            beta_c2 = jnp.where(left, u["beta_c"][0], u["beta_c"][1])
            decay = jnp.where(incl[d], jnp.exp(jnp.where(incl[d], gc_c2 - u["gc_r"], 0.0)), 0.0)
            a = jnp.where(strict[d], u["gram"][0:CHUNK] * decay, 0.0) * beta_c2
            qkd_s[slot, u["uid"]] = (u["gram"][CHUNK:2 * CHUNK] * decay).astype(BF16)
            u["a"] = a
            u["x"] = eye2 - jnp.where(lvl_mask[0], a, 0.0)
        yield
        for lg in range(1, n_lvl):
            for u in units:
                u["xb"] = u["x"].astype(BF16)
                off = jnp.where(lvl_mask[lg], u["a"], 0.0).astype(BF16)
                u["z"] = _dot(off, block_diag(u["xb"]))
            yield
            for u in units:
                u["x"] = u["x"] - _dot(u["xb"], block_diag(u["z"].astype(BF16)))
            yield
        for u in units:
            m = u["m"]
            tb = u["x"] * u["beta_r"]
            tbe = tb * jnp.exp(u["gc_r"])
            v2 = v_ref[0, u["rows"], 2 * m * dv:2 * (m + 1) * dv]
            v_bd = jnp.concatenate([jnp.concatenate([v2[:, :dv], zero_b], axis=1),
                                    jnp.concatenate([zero_b, v2[:, dv:]], axis=1)], axis=0)
            k_bd = jnp.concatenate([jnp.concatenate([u["kb"], zero_b], axis=1),
                                    jnp.concatenate([zero_b, u["kb"]], axis=1)], axis=0)
            uw_s[slot, u["uid"], :, :2 * dv] = _dot(tb.astype(BF16), v_bd)
            uw_s[slot, u["uid"], :, 2 * dv:] = _dot(tbe.astype(BF16), k_bd)
        yield

    def recurrence_part(i, slot):
        units = make_units(i)
        for seq in range(CHUNKS_PER_STEP):
            now = [u for u in units if u["seq"] == seq]
            for u in now:
                m = u["m"]
                load_gates(u)
                uw = uw_s[slot, u["uid"]]
                u["u"] = uw[:, :2 * dv]
                u["s2"] = st_s[2 * m + u["d"]]
                lhs = jnp.concatenate([uw[:, 2 * dv:3 * dv].astype(BF16), uw[:, 3 * dv:].astype(BF16),
                                       q_ref[0, u["rows"], m * dk:(m + 1) * dk]], axis=0)
                u["ws"] = _dot(lhs, u["s2"].astype(BF16))
            yield
            for u in now:
                d, ws = u["d"], u["ws"]
                vn0 = (u["u"][:, :dv] - ws[0:CHUNK, :dv]).astype(BF16)
                vn1 = (u["u"][:, dv:] - ws[CHUNK:2 * CHUNK, dv:]).astype(BF16)
                vn_bd = jnp.concatenate([jnp.concatenate([vn0, zero_b], axis=1),
                                         jnp.concatenate([zero_b, vn1], axis=1)], axis=0)
                last = (CHUNK - 1, 2 * CHUNK - 1) if d == 0 else (0, CHUNK)
                u["gl"] = [u["gc_r"][:, l:l + 1] for l in last]
                gl2 = jnp.where(left_row, u["gl"][0], u["gl"][1])
                kdt = kt2_s[u["chunk"], u["m"]] * jnp.exp(gl2 - u["gc_r"])
                lhs = jnp.concatenate([qkd_s[slot, u["uid"]], kdt.astype(BF16)], axis=0)
                u["ov"] = _dot(lhs, vn_bd)
            yield
            for u in now:
                m, ws, ov = u["m"], u["ws"], u["ov"]
                qs = jnp.concatenate([ws[2 * CHUNK:, :dv] * jnp.exp(u["gc_c"][0]),
                                      ws[2 * CHUNK:, dv:] * jnp.exp(u["gc_c"][1])], axis=1)
                o_s[u["rows"], 2 * m * dv:2 * (m + 1) * dv] += qs + ov[0:CHUNK]
                egl = jnp.concatenate([jnp.broadcast_to(jnp.exp(u["gl"][0]), (1, dv)),
                                       jnp.broadcast_to(jnp.exp(u["gl"][1]), (1, dv))], axis=1)
                st_s[2 * m + u["d"]] = u["s2"] * egl + ov[CHUNK:]
            yield

    def step(i, carry):
        slot = i & 1
        ahead = state_free_part(jnp.minimum(i, n_sup - 1), slot)
        behind = recurrence_part(jnp.maximum(i - 1, 0), 1 - slot)
        live = [ahead, behind]
        while live:
            live = [g for g in live if next(g, "done") != "done"]
        return carry

    lax.fori_loop(0, n_sup + 1, step, 0)

    def finish(i, carry):
        rows = pl.ds(pl.multiple_of(i * STAGE_TILE, STAGE_TILE), STAGE_TILE)
        y_ref[0, rows, :] = o_s[rows, :].astype(y_ref.dtype)
        return carry

    lax.fori_loop(0, t_tot // STAGE_TILE, finish, 0)


def _gdn_delta(p_main, colg, rowg, dv, ctx_len, n_hk, n_hv):
    n_batch, t_tot, _ = p_main.shape
    dk = GDN_DK
    assert n_hv == 2 * n_hk and dk == dv == LANES and n_hk % HEADS_PER_STEP == 0
    assert ctx_len % SUPER == 0 and t_tot % SUPER == 0 and SUPER % STAGE_TILE == 0
    n_m = HEADS_PER_STEP
    qb = n_m * dk
    vb = 2 * n_m * dv
    k_blk0 = n_hk // n_m
    v_blk0 = (2 * n_hk * dk) // vb
    n_pairs = t_tot // PAIR
    n_units = 2 * n_m * CHUNKS_PER_STEP
    return pl.pallas_call(
        functools.partial(_delta_body, ctx_len=ctx_len, n_hv=n_hv, n_m=n_m),
        grid=(n_batch, n_hk // n_m),
        in_specs=[
            pl.BlockSpec((1, t_tot, qb), lambda b, h: (b, 0, h)),
            pl.BlockSpec((1, t_tot, qb), lambda b, h: (b, 0, k_blk0 + h)),
            pl.BlockSpec((1, t_tot, vb), lambda b, h: (b, 0, v_blk0 + h)),
            pl.BlockSpec((1, t_tot, LANES), lambda b, h: (b, 0, 0)),
            pl.BlockSpec((1, n_pairs, LANES, PAIR), lambda b, h: (b, 0, 0, 0)),
        ],
        out_specs=pl.BlockSpec((1, t_tot, vb), lambda b, h: (b, 0, h)),
        out_shape=jax.ShapeDtypeStruct((n_batch, t_tot, n_hv * dv), BF16),
        scratch_shapes=[
            pltpu.VMEM((t_tot, vb), F32),
            pltpu.VMEM((2 * n_m, dk, 2 * dv), F32),
            pltpu.VMEM((t_tot, LANES), F32),
            pltpu.VMEM((2, n_units, CHUNK, 2 * dv + 2 * dk), F32),
            pltpu.VMEM((2, n_units, CHUNK, 2 * CHUNK), BF16),
            pltpu.VMEM((t_tot // CHUNK, n_m, dk, 2 * CHUNK), F32),
        ],
        compiler_params=_params(("parallel", "parallel")),
        name="gdn_delta",
    )(p_main, p_main, p_main, colg, rowg)


def _rope(x, cos, sin_a, sin_b):
    quarter = ATT_DH // 4
    return x * cos + pltpu.roll(x, ATT_DH - quarter, axis=1) * sin_a + pltpu.roll(x, quarter, axis=1) * sin_b


def _rms_norm(x, g):
    return x * lax.rsqrt(jnp.mean(x * x, axis=-1, keepdims=True) + EPS) * g


def _attn_body(q_ref, k_ref, v_ref, cos_ref, sa_ref, sb_ref, qn_ref, kn_ref, o_ref, kt_s, v1_s,
               *, ctx_len, n_group, skip_tiles):
    qi = pl.program_id(2)
    tq = q_ref.shape[1]
    t_tot = k_ref.shape[1]
    dh = k_ref.shape[2]

    @pl.when(qi == 0)
    def _():
        k = _rms_norm(k_ref[0].astype(F32), kn_ref[...])
        k = _rope(k, cos_ref[...], sa_ref[...], sb_ref[...])
        kt_s[...] = k.T.astype(kt_s.dtype)
        v1_s[:, :dh] = v_ref[0]
        v1_s[:, dh:] = jnp.ones((t_tot, dh), v1_s.dtype)

    r0 = pl.multiple_of((qi + skip_tiles) * tq, tq)
    cos = cos_ref[pl.ds(r0, tq), :]
    sin_a = sa_ref[pl.ds(r0, tq), :]
    sin_b = sb_ref[pl.ds(r0, tq), :]

    def attend(n_keys):
        def scores(g):
            q = _rms_norm(q_ref[0, :, g * dh:(g + 1) * dh].astype(F32), qn_ref[...])
            q = _rope(q, cos, sin_a, sin_b) * (dh ** -0.5 * LOG2E)
            return _dot(q.astype(BF16), kt_s[:, :n_keys])

        s_next = scores(0)
        for g in range(n_group):
            s = s_next
            if g + 1 < n_group:
                s_next = scores(g + 1)
            p = jnp.exp2(s - jnp.max(s, axis=-1, keepdims=True))
            num_den = _dot(p.astype(BF16), v1_s[:n_keys, :])
            o_ref[0, :, g * dh:(g + 1) * dh] = (num_den[:, :dh] / num_den[:, dh:]).astype(o_ref.dtype)

    @pl.when(r0 < ctx_len)
    def _():
        attend(ctx_len)

    @pl.when(r0 >= ctx_len)
    def _():
        attend(t_tot)


def _attention(p_qkv, tables, q_norm, k_norm, ctx_len, n_hq, n_hkv, skip_tiles=0):
    n_batch, t_tot, _ = p_qkv.shape
    dh = ATT_DH
    n_group = n_hq // n_hkv
    tq = TOKEN_TILE
    cos, sin_a, sin_b = tables
    full = lambda shape: pl.BlockSpec(shape, lambda b, h, q: (0,) * len(shape))
    return pl.pallas_call(
        functools.partial(_attn_body, ctx_len=ctx_len, n_group=n_group, skip_tiles=skip_tiles),
        grid=(n_batch, n_hkv, t_tot // tq - skip_tiles),
        in_specs=[
            pl.BlockSpec((1, tq, n_group * dh), lambda b, h, q: (b, q + skip_tiles, h)),
            pl.BlockSpec((1, t_tot, dh), lambda b, h, q: (b, 0, n_hq + h)),
            pl.BlockSpec((1, t_tot, dh), lambda b, h, q: (b, 0, n_hq + n_hkv + h)),
            full((t_tot, dh)), full((t_tot, dh)), full((t_tot, dh)),
            full((1, dh)), full((1, dh)),
        ],
        out_specs=pl.BlockSpec((1, tq, n_group * dh), lambda b, h, q: (b, q, h)),
        out_shape=jax.ShapeDtypeStruct((n_batch, t_tot - skip_tiles * tq, n_hq * dh), BF16),
        scratch_shapes=[pltpu.VMEM((dh, t_tot), BF16), pltpu.VMEM((t_tot, 2 * dh), BF16)],
        compiler_params=_params(("parallel", "parallel", "arbitrary")),
        name="attention",
    )(p_qkv, p_qkv, p_qkv, cos, sin_a, sin_b, q_norm, k_norm)


def _rope_tables(ctx_len, seq_len):
    rows = seq_len // GRID_W
    row = jnp.repeat(jnp.arange(rows), GRID_W).astype(F32)
    col = jnp.tile(jnp.arange(GRID_W), rows).astype(F32)
    n_freq = ATT_DH // 4
    freqs = ROPE_THETA ** (-jnp.arange(n_freq, dtype=F32) / n_freq)
    ang_r = row[:, None] * freqs
    ang_c = col[:, None] * freqs
    ang = jnp.concatenate([ang_r, ang_r, ang_c, ang_c], axis=-1)
    cos = jnp.concatenate([jnp.ones((ctx_len, ATT_DH), F32), jnp.cos(ang)], axis=0)
    sin = jnp.concatenate([jnp.zeros((ctx_len, ATT_DH), F32), jnp.sin(ang)], axis=0)
    first_half = (jnp.arange(ATT_DH) % (2 * n_freq)) < n_freq
    return cos, jnp.where(first_half, -sin, 0.0), jnp.where(first_half, 0.0, sin)


def _mix_ffn_body(y_ref, *refs, n_chunk, alpha, gated):
    if gated:
        z_ref, ng_ref, wo_ref, x_ref, mod_ref, ln_ref, win_ref, wout_ref, o_ref = refs
        dv = ng_ref.shape[1]
        heads = []
        for hh in range(y_ref.shape[2] // dv):
            o = y_ref[0, :, hh * dv:(hh + 1) * dv].astype(F32)
            z = z_ref[0, :, hh * dv:(hh + 1) * dv].astype(F32)
            heads.append((_rms_norm(o, ng_ref[...]) * _silu(z)).astype(BF16))
        y = jnp.concatenate(heads, axis=1)
    else:
        wo_ref, x_ref, mod_ref, ln_ref, win_ref, wout_ref, o_ref = refs
        y = y_ref[0]
    delta = _dot(y, wo_ref[...])
    x1 = _layer_norm(alpha * x_ref[0] + mod_ref[0, 2:3, :] * delta, ln_ref[0:1, :], ln_ref[1:2, :])
    hb = (x1 * (1.0 + mod_ref[0, 4:5, :]) + mod_ref[0, 3:4, :]).astype(BF16)
    d_ff = wout_ref.shape[0]
    tc = d_ff // n_chunk
    acc = None
    for c in range(n_chunk):
        gate = _dot(hb, win_ref[:, c * tc:(c + 1) * tc])
        up = _dot(hb, win_ref[:, d_ff + c * tc:d_ff + (c + 1) * tc])
        part = _dot((_silu(gate) * up).astype(BF16), wout_ref[c * tc:(c + 1) * tc, :])
        acc = part if acc is None else acc + part
    o_ref[0] = _layer_norm(alpha * x1 + mod_ref[0, 5:6, :] * acc, ln_ref[2:3, :], ln_ref[3:4, :])


def _mix_ffn(y, w_o, xs, mods, ln, w_in, w_out, n_ctx_tiles, alpha, skip_tiles=0, gate=None):
    n_batch, t_tot, d = xs.shape
    k = y.shape[-1]
    d_ff = w_out.shape[0]
    tm = TOKEN_TILE
    n_chunk = 2 if d_ff % (2 * LANES) == 0 else 1
    row_map = _mod_row_map(n_batch, n_ctx_tiles)
    y_skip = skip_tiles - (t_tot - y.shape[1]) // tm
    resident = lambda shape: pl.BlockSpec(shape, lambda b, t: (0,) * len(shape), pipeline_mode=pl.Buffered(1))
    gate_specs, gate_args = [], []
    if gate is not None:
        z_src, z_blk, norm_g = gate
        gate_specs = [pl.BlockSpec((1, tm, k), lambda b, t: (b, t + skip_tiles, z_blk)), resident(norm_g.shape)]
        gate_args = [z_src, norm_g]
    return pl.pallas_call(
        functools.partial(_mix_ffn_body, n_chunk=n_chunk, alpha=alpha, gated=gate is not None),
        grid=(n_batch, t_tot // tm - skip_tiles),
        in_specs=[
            pl.BlockSpec((1, tm, k), lambda b, t: (b, t + y_skip, 0)),
            *gate_specs,
            resident((k, d)),
            pl.BlockSpec((1, tm, d), lambda b, t: (b, t + skip_tiles, 0)),
            pl.BlockSpec((1, N_MOD, d), lambda b, t: row_map(b, t + skip_tiles)),
            resident(ln.shape),
            resident((d, 2 * d_ff)),
            resident((d_ff, d)),
        ],
        out_specs=pl.BlockSpec((1, tm, d), lambda b, t: (b, t, 0)),
        out_shape=jax.ShapeDtypeStruct((n_batch, t_tot - skip_tiles * tm, d), F32),
        compiler_params=_params(("parallel", "parallel")),
        name="mix_ffn",
    )(y, *gate_args, w_o, xs, mods, ln, w_in, w_out)


def kernel(x, c, ctx, c_ctx, w_mod, b_mod, ln_g, ln_b, w_ffn_in, w_ffn_out, gdn_w_in, gdn_conv, gdn_a_log,
           gdn_dt_bias, gdn_norm_g, gdn_w_out, attn_w_qkv, attn_q_norm, attn_k_norm, attn_w_out):
    n_batch, seq_len, d = x.shape
    ctx_len = ctx.shape[1]
    depth = w_mod.shape[0]
    alpha = (2 * depth) ** 0.25
    n_hv = gdn_a_log.shape[-1]
    dv = gdn_norm_g.shape[-1]
    v_w = n_hv * dv
    qkv_w = gdn_conv.shape[-1]
    n_hk = (qkv_w - v_w) // (2 * GDN_DK)
    n_hq = attn_w_out.shape[1] // ATT_DH
    n_hkv = (attn_w_qkv.shape[-1] // ATT_DH - n_hq) // 2
    assert ctx_len % TOKEN_TILE == 0 and seq_len % TOKEN_TILE == 0 and seq_len % GRID_W == 0
    assert 4 * n_hv <= LANES and d % LANES == 0
    n_ctx_tiles = ctx_len // TOKEN_TILE

    xs = jnp.concatenate([ctx, x], axis=1)

    mp = -(-(n_batch + 1) // SUBLANES) * SUBLANES
    cond = jnp.concatenate([c, c_ctx[None, :], jnp.zeros((mp - n_batch - 1, d), F32)], axis=0)
    mods = _modulation(cond, w_mod, b_mod).reshape(depth, mp, N_MOD, d)
    tables = _rope_tables(ctx_len, seq_len)

    for i in range(depth):
        j = i // 2
        m = mods[i]
        skip = n_ctx_tiles if i == depth - 1 else 0
        if i % 2 == 0:
            w_in = gdn_w_in[j]
            gate_w = jnp.pad(w_in[:, qkv_w + v_w:], ((0, 0), (0, LANES - 4 * n_hv)))
            lanes_pad = (0, LANES - 4 * n_hv)
            a_log = jnp.pad(jnp.concatenate([jnp.zeros((2 * n_hv,), F32), gdn_a_log[j].reshape(-1)]), lanes_pad)
            dt_b = jnp.pad(jnp.concatenate([jnp.zeros((2 * n_hv,), F32), gdn_dt_bias[j].reshape(-1)]), lanes_pad)
            p_main, colg, rowg = _gdn_in_proj(xs, m, w_in[:, :qkv_w + v_w].astype(BF16), gate_w, gdn_conv[j],
                                              a_log[None, :], dt_b[None, :], n_ctx_tiles,
                                              qk_w=n_hk * GDN_DK, n_hv=n_hv, n_chunk=6)
            y = _gdn_delta(p_main, colg, rowg, dv, ctx_len, n_hk, n_hv)
            w_o = gdn_w_out[j]
            assert qkv_w % v_w == 0
            gate = (p_main, qkv_w // v_w, gdn_norm_g[j][None, :])
        else:
            p_qkv = _in_proj(xs, m, attn_w_qkv[j].astype(BF16), n_ctx_tiles, n_chunk=3)
            y = _attention(p_qkv, tables, attn_q_norm[j][None, :], attn_k_norm[j][None, :], ctx_len, n_hq, n_hkv,
                           skip_tiles=skip)
            w_o = attn_w_out[j]
            gate = None
        ln = jnp.stack([ln_g[i, 0], ln_b[i, 0], ln_g[i, 1], ln_b[i, 1]])
        xs = _mix_ffn(y, w_o.astype(BF16), xs, m, ln, w_ffn_in[i].astype(BF16), w_ffn_out[i].astype(BF16),
                      n_ctx_tiles, alpha, skip_tiles=skip, gate=gate)
    return xs
```

```python
import functools
import math

import jax
import jax.numpy as jnp
from jax import lax
from jax.experimental import pallas as pl
from jax.experimental.pallas import tpu as pltpu

F32 = jnp.float32
BF16 = jnp.bfloat16

N_MOD = 6
EPS = 1e-6
GDN_DK = 128
CHUNK = 64
ATT_DH = 128
GRID_W = 64
ROPE_THETA = 10000.0

LANES = 128
SUBLANES = 8
PAIR = 2 * CHUNK
VMEM_LIMIT_BYTES = 56 * 1024 * 1024
TOKEN_TILE = 256
NEG_INF = -1e30
LOG2E = math.log2(math.e)

assert PAIR == LANES


def _params(sem):
    return pltpu.CompilerParams(dimension_semantics=sem, vmem_limit_bytes=VMEM_LIMIT_BYTES)


def _dot(a, b):
    return jnp.dot(a, b, preferred_element_type=F32)


def _split2(a):
    hi = a.astype(BF16)
    return hi, (a - hi.astype(F32)).astype(BF16)


def _dot_hi(a, b):
    a1, a2 = _split2(a)
    b1, b2 = _split2(b)
    return _dot(a1, b1) + (_dot(a1, b2) + _dot(a2, b1))


def _sigmoid(x):
    return 1.0 / (1.0 + jnp.exp(-x))


def _silu(x):
    return x * _sigmoid(x)


def _layer_norm(r, g, b):
    mu = jnp.mean(r, axis=-1, keepdims=True)
    rc = r - mu
    var = jnp.mean(rc * rc, axis=-1, keepdims=True)
    return rc * lax.rsqrt(var + EPS) * g + b


def _mod_body(cond_ref, w_ref, b_ref, o_ref):
    o_ref[0] = _dot_hi(_silu(cond_ref[...]), w_ref[0]) + b_ref[0]


def _modulation(cond, w_mod, b_mod):
    n_layer, d, n = w_mod.shape
    mp = cond.shape[0]
    tn = n // 4
    return pl.pallas_call(
        _mod_body,
        grid=(n_layer, n // tn),
        in_specs=[
            pl.BlockSpec((mp, d), lambda l, j: (0, 0)),
            pl.BlockSpec((1, d, tn), lambda l, j: (l, 0, j)),
            pl.BlockSpec((1, 1, tn), lambda l, j: (l, 0, j)),
        ],
        out_specs=pl.BlockSpec((1, mp, tn), lambda l, j: (l, 0, j)),
        out_shape=jax.ShapeDtypeStruct((n_layer, mp, n), F32),
        compiler_params=_params(("parallel", "parallel")),
        name="modulation",
    )(cond, w_mod, b_mod.reshape(n_layer, 1, n))


def _mod_row_map(n_batch, n_ctx_tiles):
    return lambda b, t: (jnp.where(t < n_ctx_tiles, n_batch, b), 0, 0)


def _in_proj_body(x_ref, mod_ref, w_ref, o_ref, *, n_chunk):
    hb = (x_ref[0] * (1.0 + mod_ref[0, 1:2, :]) + mod_ref[0, 0:1, :]).astype(BF16)
    tn = w_ref.shape[1] // n_chunk
    for j in range(n_chunk):
        o_ref[0, :, j * tn:(j + 1) * tn] = _dot(hb, w_ref[:, j * tn:(j + 1) * tn]).astype(o_ref.dtype)


def _in_proj(xs, mods, w, n_ctx_tiles, n_chunk):
    n_batch, t_tot, d = xs.shape
    n = w.shape[1]
    tm = TOKEN_TILE
    return pl.pallas_call(
        functools.partial(_in_proj_body, n_chunk=n_chunk),
        grid=(n_batch, t_tot // tm),
        in_specs=[
            pl.BlockSpec((1, tm, d), lambda b, t: (b, t, 0)),
            pl.BlockSpec((1, N_MOD, d), _mod_row_map(n_batch, n_ctx_tiles)),
            pl.BlockSpec((d, n), lambda b, t: (0, 0)),
        ],
        out_specs=pl.BlockSpec((1, tm, n), lambda b, t: (b, t, 0)),
        out_shape=jax.ShapeDtypeStruct((n_batch, t_tot, n), BF16),
        compiler_params=_params(("parallel", "parallel")),
        name="in_proj",
    )(xs, mods, w)


CONV_HALO = SUBLANES


def _l2_norm(x):
    return x * lax.rsqrt(jnp.sum(x * x, axis=-1, keepdims=True) + EPS)


def _gdn_proj_body(x_ref, xp_ref, xn_ref, mod_ref, w_ref, wxh_ref, wxl_ref, conv_ref, alog_ref, dtb_ref,
                   o_ref, col_ref, row_ref, *, ctx_len, qk_w, qkv_w, n_chunk, dk, n_hv):
    tm = x_ref.shape[1]
    r0 = pl.program_id(1) * tm
    t_tot = pl.num_programs(1) * tm
    scale = 1.0 + mod_ref[0, 1:2, :]
    shift = mod_ref[0, 0:1, :]
    h = x_ref[0] * scale + shift
    hb = h.astype(BF16)
    hb_ext = jnp.concatenate([xp_ref[0] * scale + shift, h, xn_ref[0] * scale + shift], axis=0).astype(BF16)
    keep_lo = jnp.where(jnp.logical_or(r0 == 0, r0 == ctx_len), 0.0, 1.0)
    keep_hi = jnp.where(jnp.logical_or(r0 + tm == ctx_len, r0 + tm == t_tot), 0.0, 1.0)
    conv_w = conv_ref.shape[0]
    pad = conv_w // 2
    n_ext = tm + 2 * CONV_HALO
    mid = slice(CONV_HALO, CONV_HALO + tm)
    tn = w_ref.shape[1] // n_chunk
    for j in range(n_chunk):
        c0 = j * tn
        if c0 >= qkv_w:
            o_ref[0, :, c0:c0 + tn] = _dot(hb, w_ref[:, c0:c0 + tn]).astype(o_ref.dtype)
            continue
        p = _dot(hb_ext, w_ref[:, c0:c0 + tn])
        ext = jnp.concatenate([p[:CONV_HALO] * keep_lo, p[mid], p[CONV_HALO + tm:] * keep_hi], axis=0)
        y = None
        for tap in range(conv_w):
            off = tap - pad
            xs = ext if off == 0 else pltpu.roll(ext, (-off) % n_ext, axis=0)
            term = xs[mid] * conv_ref[tap:tap + 1, c0:c0 + tn]
            y = term if y is None else y + term
        y = _silu(y)
        if c0 >= 2 * qk_w:
            o_ref[0, :, c0:c0 + tn] = y.astype(o_ref.dtype)
            continue
        for hh in range(tn // dk):
            yh = _l2_norm(y[:, hh * dk:(hh + 1) * dk])
            if c0 < qk_w:
                yh = yh * (dk ** -0.5)
            o_ref[0, :, c0 + hh * dk:c0 + (hh + 1) * dk] = yh.astype(o_ref.dtype)
    h_lo = (h - hb.astype(F32)).astype(BF16)
    raw = _dot(hb, wxh_ref[...]) + (_dot(hb, wxl_ref[...]) + _dot(h_lo, wxh_ref[...]))
    _gate_forms(raw, alog_ref[...], dtb_ref[...], col_ref, row_ref, n_hv)


def _gate_forms(raw, a_log, dt_bias, col_ref, row_ref, n_hv):
    beta = _sigmoid(raw)
    xs = raw + dt_bias
    g = -jnp.exp(a_log) * (jnp.maximum(xs, 0.0) + jnp.log1p(jnp.exp(-jnp.abs(xs))))
    r = lax.broadcasted_iota(jnp.int32, (PAIR, PAIR), 0)
    c = lax.broadcasted_iota(jnp.int32, (PAIR, PAIR), 1)
    same = (r < CHUNK) == (c < CHUNK)
    tri_f = jnp.where(jnp.logical_and(same, c <= r), 1.0, 0.0).astype(BF16)
    tri_r = jnp.where(jnp.logical_and(same, c >= r), 1.0, 0.0).astype(BF16)
    lane = lax.broadcasted_iota(jnp.int32, (PAIR, LANES), 1)
    for p in range(raw.shape[0] // PAIR):
        gp = g[p * PAIR:(p + 1) * PAIR]
        g1 = gp.astype(BF16)
        r1 = gp - g1.astype(F32)
        g2 = r1.astype(BF16)
        g3 = (r1 - g2.astype(F32)).astype(BF16)
        cum_f = _dot(tri_f, g1) + (_dot(tri_f, g2) + _dot(tri_f, g3))
        cum_r = _dot(tri_r, g1) + (_dot(tri_r, g2) + _dot(tri_r, g3))
        col = jnp.where(lane < 2 * n_hv, beta[p * PAIR:(p + 1) * PAIR],
                        jnp.where(lane < 3 * n_hv, cum_f, cum_r))
        col_ref[0, p * PAIR:(p + 1) * PAIR, :] = col
        row_ref[0, p] = col.T


def _gdn_in_proj(xs, mods, w, w_gate, conv, a_log_lanes, dt_bias_lanes, n_ctx_tiles, qk_w, n_hv, n_chunk):
    n_batch, t_tot, d = xs.shape
    n = w.shape[1]
    conv_w, qkv_w = conv.shape
    tm = TOKEN_TILE
    tn = n // n_chunk
    assert qk_w % tn == 0 and qkv_w % tn == 0 and tn % GDN_DK == 0 and tm % PAIR == 0
    halo_per_tile = tm // CONV_HALO
    last_halo = t_tot // CONV_HALO - 1
    w_gate_hi, w_gate_lo = _split2(w_gate)
    const = lambda shape: pl.BlockSpec(shape, lambda b, t: (0,) * len(shape))
    return pl.pallas_call(
        functools.partial(_gdn_proj_body, ctx_len=n_ctx_tiles * tm, qk_w=qk_w, qkv_w=qkv_w, n_chunk=n_chunk,
                          dk=GDN_DK, n_hv=n_hv),
        grid=(n_batch, t_tot // tm),
        in_specs=[
            pl.BlockSpec((1, tm, d), lambda b, t: (b, t, 0)),
            pl.BlockSpec((1, CONV_HALO, d), lambda b, t: (b, jnp.maximum(t * halo_per_tile - 1, 0), 0)),
            pl.BlockSpec((1, CONV_HALO, d), lambda b, t: (b, jnp.minimum((t + 1) * halo_per_tile, last_halo), 0)),
            pl.BlockSpec((1, N_MOD, d), _mod_row_map(n_batch, n_ctx_tiles)),
            const((d, n)), const((d, LANES)), const((d, LANES)), const((conv_w, qkv_w)),
            const((1, LANES)), const((1, LANES)),
        ],
        out_specs=[pl.BlockSpec((1, tm, n), lambda b, t: (b, t, 0)),
                   pl.BlockSpec((1, tm, LANES), lambda b, t: (b, t, 0)),
                   pl.BlockSpec((1, tm // PAIR, LANES, PAIR), lambda b, t: (b, t, 0, 0))],
        out_shape=[jax.ShapeDtypeStruct((n_batch, t_tot, n), BF16),
                   jax.ShapeDtypeStruct((n_batch, t_tot, LANES), F32),
                   jax.ShapeDtypeStruct((n_batch, t_tot // PAIR, LANES, PAIR), F32)],
        compiler_params=_params(("parallel", "parallel")),
        name="gdn_in_proj",
    )(xs, xs, xs, mods, w, w_gate_hi, w_gate_lo, conv, a_log_lanes, dt_bias_lanes)


STAGE_TILE = 256
HEADS_PER_STEP = 2
CHUNKS_PER_STEP = 4
SUPER = CHUNKS_PER_STEP * CHUNK


def _delta_body(q_ref, k_ref, v_ref, colg_ref, rowg_ref,
                y_ref, o_s, st_s, cg_s, uw_s, qkd_s, kt2_s, *, ctx_len, n_hv, n_m):
    g_idx = pl.program_id(1)
    t_tot = q_ref.shape[1]
    dk = q_ref.shape[2] // n_m
    dv = v_ref.shape[2] // (2 * n_m)
    n_sup = t_tot // SUPER
    n_ctx_sup = ctx_len // SUPER

    lane_shift = (LANES - 2 * n_m * g_idx) % LANES

    def prep(i, carry):
        rows = pl.ds(pl.multiple_of(i * STAGE_TILE, STAGE_TILE), STAGE_TILE)
        for m in range(n_m):
            km = k_ref[0, rows, m * dk:(m + 1) * dk].astype(F32)
            for cc in range(STAGE_TILE // CHUNK):
                kc = km[cc * CHUNK:(cc + 1) * CHUNK]
                kt2_s[i * (STAGE_TILE // CHUNK) + cc, m] = jnp.concatenate([kc, kc], axis=0).T
        cg_s[rows, :] = pltpu.roll(colg_ref[0, rows, :], lane_shift, axis=1)
        return carry

    lax.fori_loop(0, t_tot // STAGE_TILE, prep, 0)
    o_s[...] = jnp.zeros_like(o_s)
    st_s[...] = jnp.zeros_like(st_s)

    r = lax.broadcasted_iota(jnp.int32, (CHUNK, LANES), 0)
    lane = lax.broadcasted_iota(jnp.int32, (CHUNK, LANES), 1)
    c = lane & (CHUNK - 1)
    left = lane < CHUNK
    left_row = lax.broadcasted_iota(jnp.int32, (1, LANES), 1) < CHUNK
    keep_left = jnp.where(left, 1.0, 0.0).astype(BF16)
    keep_right = jnp.where(left, 0.0, 1.0).astype(BF16)
    eye2 = jnp.where(c == r, 1.0, 0.0)
    n_lvl = CHUNK.bit_length() - 1
    lvl_mask = [jnp.logical_and((r >> (lg + 1)) == (c >> (lg + 1)), (r >> lg) != (c >> lg)) for lg in range(n_lvl)]
    incl = (c <= r, c >= r)
    strict = (c < r, c > r)
    zero_b = jnp.zeros((CHUNK, dv), BF16)

    def block_diag(y):
        return jnp.concatenate([y * keep_left, y * keep_right], axis=0)

    def reverse_super(i):
        return jnp.where(i < n_ctx_sup, n_ctx_sup - 1 - i, n_sup + n_ctx_sup - 1 - i)

    def make_units(i):
        sup = (i, reverse_super(i))
        units = []
        for m in range(n_m):
            for d in (0, 1):
                for seq in range(CHUNKS_PER_STEP):
                    cidx = seq if d == 0 else CHUNKS_PER_STEP - 1 - seq
                    units.append(dict(
                        m=m, d=d, seq=seq, half=cidx % 2, uid=(m * 2 + d) * CHUNKS_PER_STEP + seq,
                        pair=sup[d] * (SUPER // PAIR) + cidx // 2, chunk=sup[d] * CHUNKS_PER_STEP + cidx,
                        rows=pl.ds(pl.multiple_of(sup[d] * SUPER + cidx * CHUNK, CHUNK), CHUNK)))
        return units

    def load_gates(u):
        m, d, half = u["m"], u["d"], u["half"]
        cg = cg_s[u["rows"], :]
        b0, g0 = d * n_hv + 2 * m, (2 + d) * n_hv + 2 * m
        u["beta_c"] = (cg[:, b0:b0 + 1], cg[:, b0 + 1:b0 + 2])
        u["gc_c"] = (cg[:, g0:g0 + 1], cg[:, g0 + 1:g0 + 2])
        head0 = 2 * (n_m * g_idx + m)

        def packed_rows(base):
            rows = rowg_ref[0, u["pair"], pl.ds(base + head0, 2), :]
            a, b = rows[0:1], rows[1:2]
            if half == 0:
                return jnp.where(left_row, a, pltpu.roll(b, CHUNK, axis=1))
            return jnp.where(left_row, pltpu.roll(a, CHUNK, axis=1), b)

        u["beta_r"] = packed_rows(d * n_hv)
        u["gc_r"] = packed_rows((2 + d) * n_hv)

    def state_free_part(i, slot):
        units = make_units(i)
        for u in units:
            m = u["m"]
            load_gates(u)
            qb, kb = q_ref[0, u["rows"], m * dk:(m + 1) * dk], k_ref[0, u["rows"], m * dk:(m + 1) * dk]
            u["kb"] = kb
            u["gram"] = lax.dot_general(jnp.concatenate([kb, qb], axis=0), jnp.concatenate([kb, kb], axis=0),
                                        (((1,), (1,)), ((), ())), preferred_element_type=F32)
        yield
        for u in units:
            d = u["d"]
            gc_c2 = jnp.where(left, u["gc_c"][0], u["gc_c"][1])
            beta_c2 = jnp.where(left, u["beta_c"][0], u["beta_c"][1])
            decay = jnp.where(incl[d], jnp.exp(jnp.where(incl[d], gc_c2 - u["gc_r"], 0.0)), 0.0)
            a = jnp.where(strict[d], u["gram"][0:CHUNK] * decay, 0.0) * beta_c2
            qkd_s[slot, u["uid"]] = (u["gram"][CHUNK:2 * CHUNK] * decay).astype(BF16)
            u["a"] = a
            u["x"] = eye2 - jnp.where(lvl_mask[0], a, 0.0)
        yield
        for lg in range(1, n_lvl):
            for u in units:
                u["xb"] = u["x"].astype(BF16)
                off = jnp.where(lvl_mask[lg], u["a"], 0.0).astype(BF16)
                u["z"] = _dot(off, block_diag(u["xb"]))
            yield
            for u in units:
                u["x"] = u["x"] - _dot(u["xb"], block_diag(u["z"].astype(BF16)))
            yield
        for u in units:
            m = u["m"]
            tb = u["x"] * u["beta_r"]
            tbe = tb * jnp.exp(u["gc_r"])
            v2 = v_ref[0, u["rows"], 2 * m * dv:2 * (m + 1) * dv]
            v_bd = jnp.concatenate([jnp.concatenate([v2[:, :dv], zero_b], axis=1),
                                    jnp.concatenate([zero_b, v2[:, dv:]], axis=1)], axis=0)
            k_bd = jnp.concatenate([jnp.concatenate([u["kb"], zero_b], axis=1),
                                    jnp.concatenate([zero_b, u["kb"]], axis=1)], axis=0)
            uw_s[slot, u["uid"], :, :2 * dv] = _dot(tb.astype(BF16), v_bd)
            uw_s[slot, u["uid"], :, 2 * dv:] = _dot(tbe.astype(BF16), k_bd)
        yield

    def recurrence_part(i, slot):
        units = make_units(i)
        for seq in range(CHUNKS_PER_STEP):
            now = [u for u in units if u["seq"] == seq]
            for u in now:
                m = u["m"]
                load_gates(u)
                uw = uw_s[slot, u["uid"]]
                u["u"] = uw[:, :2 * dv]
                u["s2"] = st_s[2 * m + u["d"]]
                lhs = jnp.concatenate([uw[:, 2 * dv:3 * dv].astype(BF16), uw[:, 3 * dv:].astype(BF16),
                                       q_ref[0, u["rows"], m * dk:(m + 1) * dk]], axis=0)
                u["ws"] = _dot(lhs, u["s2"].astype(BF16))
            yield
            for u in now:
                d, ws = u["d"], u["ws"]
                vn0 = (u["u"][:, :dv] - ws[0:CHUNK, :dv]).astype(BF16)
                vn1 = (u["u"][:, dv:] - ws[CHUNK:2 * CHUNK, dv:]).astype(BF16)
                vn_bd = jnp.concatenate([jnp.concatenate([vn0, zero_b], axis=1),
                                         jnp.concatenate([zero_b, vn1], axis=1)], axis=0)
                last = (CHUNK - 1, 2 * CHUNK - 1) if d == 0 else (0, CHUNK)
                u["gl"] = [u["gc_r"][:, l:l + 1] for l in last]
                gl2 = jnp.where(left_row, u["gl"][0], u["gl"][1])
                kdt = kt2_s[u["chunk"], u["m"]] * jnp.exp(gl2 - u["gc_r"])
                lhs = jnp.concatenate([qkd_s[slot, u["uid"]], kdt.astype(BF16)], axis=0)
                u["ov"] = _dot(lhs, vn_bd)
            yield
            for u in now:
                m, ws, ov = u["m"], u["ws"], u["ov"]
                qs = jnp.concatenate([ws[2 * CHUNK:, :dv] * jnp.exp(u["gc_c"][0]),
                                      ws[2 * CHUNK:, dv:] * jnp.exp(u["gc_c"][1])], axis=1)
                o_s[u["rows"], 2 * m * dv:2 * (m + 1) * dv] += qs + ov[0:CHUNK]
                egl = jnp.concatenate([jnp.broadcast_to(jnp.exp(u["gl"][0]), (1, dv)),
                                       jnp.broadcast_to(jnp.exp(u["gl"][1]), (1, dv))], axis=1)
                st_s[2 * m + u["d"]] = u["s2"] * egl + ov[CHUNK:]
            yield

    def run_interleaved(parts):
        live = list(parts)
        while live:
            live = [g for g in live if next(g, "done") != "done"]

    def step(i, carry):
        slot = i & 1
        run_interleaved([state_free_part(i, slot), recurrence_part(i - 1, 1 - slot)])
        return carry

    run_interleaved([state_free_part(jnp.int32(0), 0)])
    lax.fori_loop(1, n_sup, step, 0)
    run_interleaved([recurrence_part(jnp.int32(n_sup - 1), (n_sup - 1) & 1)])

    def finish(i, carry):
        rows = pl.ds(pl.multiple_of(i * STAGE_TILE, STAGE_TILE), STAGE_TILE)
        y_ref[0, rows, :] = o_s[rows, :].astype(y_ref.dtype)
        return carry

    lax.fori_loop(0, t_tot // STAGE_TILE, finish, 0)


def _gdn_delta(p_main, colg, rowg, dv, ctx_len, n_hk, n_hv):
    n_batch, t_tot, _ = p_main.shape
    dk = GDN_DK
    assert n_hv == 2 * n_hk and dk == dv == LANES and n_hk % HEADS_PER_STEP == 0
    assert ctx_len % SUPER == 0 and t_tot % SUPER == 0 and SUPER % STAGE_TILE == 0
    n_m = HEADS_PER_STEP
    qb = n_m * dk
    vb = 2 * n_m * dv
    k_blk0 = n_hk // n_m
    v_blk0 = (2 * n_hk * dk) // vb
    n_pairs = t_tot // PAIR
    n_units = 2 * n_m * CHUNKS_PER_STEP
    return pl.pallas_call(
        functools.partial(_delta_body, ctx_len=ctx_len, n_hv=n_hv, n_m=n_m),
        grid=(n_batch, n_hk // n_m),
        in_specs=[
            pl.BlockSpec((1, t_tot, qb), lambda b, h: (b, 0, h)),
            pl.BlockSpec((1, t_tot, qb), lambda b, h: (b, 0, k_blk0 + h)),
            pl.BlockSpec((1, t_tot, vb), lambda b, h: (b, 0, v_blk0 + h)),
            pl.BlockSpec((1, t_tot, LANES), lambda b, h: (b, 0, 0)),
            pl.BlockSpec((1, n_pairs, LANES, PAIR), lambda b, h: (b, 0, 0, 0)),
        ],
        out_specs=pl.BlockSpec((1, t_tot, vb), lambda b, h: (b, 0, h)),
        out_shape=jax.ShapeDtypeStruct((n_batch, t_tot, n_hv * dv), BF16),
        scratch_shapes=[
            pltpu.VMEM((t_tot, vb), F32),
            pltpu.VMEM((2 * n_m, dk, 2 * dv), F32),
            pltpu.VMEM((t_tot, LANES), F32),
            pltpu.VMEM((2, n_units, CHUNK, 2 * dv + 2 * dk), F32),
            pltpu.VMEM((2, n_units, CHUNK, 2 * CHUNK), BF16),
            pltpu.VMEM((t_tot // CHUNK, n_m, dk, 2 * CHUNK), F32),
        ],
        compiler_params=_params(("parallel", "parallel")),
        name="gdn_delta",
    )(p_main, p_main, p_main, colg, rowg)


def _rope(x, cos, sin_a, sin_b):
    quarter = ATT_DH // 4
    return x * cos + pltpu.roll(x, ATT_DH - quarter, axis=1) * sin_a + pltpu.roll(x, quarter, axis=1) * sin_b


def _rms_norm(x, g):
    return x * lax.rsqrt(jnp.mean(x * x, axis=-1, keepdims=True) + EPS) * g


def _attn_body(q_ref, k_ref, v_ref, cos_ref, sa_ref, sb_ref, qn_ref, kn_ref, o_ref, kt_s, v1_s,
               *, ctx_len, n_group, skip_tiles):
    qi = pl.program_id(2)
    tq = q_ref.shape[1]
    t_tot = k_ref.shape[1]
    dh = k_ref.shape[2]

    @pl.when(qi == 0)
    def _():
        k = _rms_norm(k_ref[0].astype(F32), kn_ref[...])
        k = _rope(k, cos_ref[...], sa_ref[...], sb_ref[...])
        kt_s[...] = k.T.astype(kt_s.dtype)
        v1_s[:, :dh] = v_ref[0]
        v1_s[:, dh:] = jnp.ones((t_tot, dh), v1_s.dtype)

    r0 = pl.multiple_of((qi + skip_tiles) * tq, tq)
    cos = cos_ref[pl.ds(r0, tq), :]
    sin_a = sa_ref[pl.ds(r0, tq), :]
    sin_b = sb_ref[pl.ds(r0, tq), :]

    def attend(n_keys):
        def scores(g):
            q = _rms_norm(q_ref[0, :, g * dh:(g + 1) * dh].astype(F32), qn_ref[...])
            q = _rope(q, cos, sin_a, sin_b) * (dh ** -0.5 * LOG2E)
            return _dot(q.astype(BF16), kt_s[:, :n_keys])

        s_next = scores(0)
        for g in range(n_group):
            s = s_next
            if g + 1 < n_group:
                s_next = scores(g + 1)
            p = jnp.exp2(s - jnp.max(s, axis=-1, keepdims=True))
            num_den = _dot(p.astype(BF16), v1_s[:n_keys, :])
            o_ref[0, :, g * dh:(g + 1) * dh] = (num_den[:, :dh] / num_den[:, dh:]).astype(o_ref.dtype)

    @pl.when(r0 < ctx_len)
    def _():
        attend(ctx_len)

    @pl.when(r0 >= ctx_len)
    def _():
        attend(t_tot)


def _attention(p_qkv, tables, q_norm, k_norm, ctx_len, n_hq, n_hkv, skip_tiles=0):
    n_batch, t_tot, _ = p_qkv.shape
    dh = ATT_DH
    n_group = n_hq // n_hkv
    tq = TOKEN_TILE
    cos, sin_a, sin_b = tables
    full = lambda shape: pl.BlockSpec(shape, lambda b, h, q: (0,) * len(shape))
    return pl.pallas_call(
        functools.partial(_attn_body, ctx_len=ctx_len, n_group=n_group, skip_tiles=skip_tiles),
        grid=(n_batch, n_hkv, t_tot // tq - skip_tiles),
        in_specs=[
            pl.BlockSpec((1, tq, n_group * dh), lambda b, h, q: (b, q + skip_tiles, h)),
            pl.BlockSpec((1, t_tot, dh), lambda b, h, q: (b, 0, n_hq + h)),
            pl.BlockSpec((1, t_tot, dh), lambda b, h, q: (b, 0, n_hq + n_hkv + h)),
            full((t_tot, dh)), full((t_tot, dh)), full((t_tot, dh)),
            full((1, dh)), full((1, dh)),
        ],
        out_specs=pl.BlockSpec((1, tq, n_group * dh), lambda b, h, q: (b, q, h)),
        out_shape=jax.ShapeDtypeStruct((n_batch, t_tot - skip_tiles * tq, n_hq * dh), BF16),
        scratch_shapes=[pltpu.VMEM((dh, t_tot), BF16), pltpu.VMEM((t_tot, 2 * dh), BF16)],
        compiler_params=_params(("parallel", "parallel", "arbitrary")),
        name="attention",
    )(p_qkv, p_qkv, p_qkv, cos, sin_a, sin_b, q_norm, k_norm)


def _rope_tables(ctx_len, seq_len):
    rows = seq_len // GRID_W
    row = jnp.repeat(jnp.arange(rows), GRID_W).astype(F32)
    col = jnp.tile(jnp.arange(GRID_W), rows).astype(F32)
    n_freq = ATT_DH // 4
    freqs = ROPE_THETA ** (-jnp.arange(n_freq, dtype=F32) / n_freq)
    ang_r = row[:, None] * freqs
    ang_c = col[:, None] * freqs
    ang = jnp.concatenate([ang_r, ang_r, ang_c, ang_c], axis=-1)
    cos = jnp.concatenate([jnp.ones((ctx_len, ATT_DH), F32), jnp.cos(ang)], axis=0)
    sin = jnp.concatenate([jnp.zeros((ctx_len, ATT_DH), F32), jnp.sin(ang)], axis=0)
    first_half = (jnp.arange(ATT_DH) % (2 * n_freq)) < n_freq
    return cos, jnp.where(first_half, -sin, 0.0), jnp.where(first_half, 0.0, sin)


def _mix_ffn_body(y_ref, *refs, n_chunk, alpha, gated):
    if gated:
        z_ref, ng_ref, wo_ref, x_ref, mod_ref, ln_ref, win_ref, wout_ref, o_ref = refs
        dv = ng_ref.shape[1]
        heads = []
        for hh in range(y_ref.shape[2] // dv):
            o = y_ref[0, :, hh * dv:(hh + 1) * dv].astype(F32)
            z = z_ref[0, :, hh * dv:(hh + 1) * dv].astype(F32)
            heads.append((_rms_norm(o, ng_ref[...]) * _silu(z)).astype(BF16))
        y = jnp.concatenate(heads, axis=1)
    else:
        wo_ref, x_ref, mod_ref, ln_ref, win_ref, wout_ref, o_ref = refs
        y = y_ref[0]
    delta = _dot(y, wo_ref[...])
    x1 = _layer_norm(alpha * x_ref[0] + mod_ref[0, 2:3, :] * delta, ln_ref[0:1, :], ln_ref[1:2, :])
    hb = (x1 * (1.0 + mod_ref[0, 4:5, :]) + mod_ref[0, 3:4, :]).astype(BF16)
    d_ff = wout_ref.shape[0]
    tc = d_ff // n_chunk
    acc = None
    for c in range(n_chunk):
        gate = _dot(hb, win_ref[:, c * tc:(c + 1) * tc])
        up = _dot(hb, win_ref[:, d_ff + c * tc:d_ff + (c + 1) * tc])
        part = _dot((_silu(gate) * up).astype(BF16), wout_ref[c * tc:(c + 1) * tc, :])
        acc = part if acc is None else acc + part
    o_ref[0] = _layer_norm(alpha * x1 + mod_ref[0, 5:6, :] * acc, ln_ref[2:3, :], ln_ref[3:4, :])


def _mix_ffn(y, w_o, xs, mods, ln, w_in, w_out, n_ctx_tiles, alpha, skip_tiles=0, gate=None):
    n_batch, t_tot, d = xs.shape
    k = y.shape[-1]
    d_ff = w_out.shape[0]
    tm = TOKEN_TILE
    n_chunk = 2 if d_ff % (2 * LANES) == 0 else 1
    row_map = _mod_row_map(n_batch, n_ctx_tiles)
    y_skip = skip_tiles - (t_tot - y.shape[1]) // tm
    resident = lambda shape: pl.BlockSpec(shape, lambda b, t: (0,) * len(shape), pipeline_mode=pl.Buffered(1))
    gate_specs, gate_args = [], []
    if gate is not None:
        z_src, z_blk, norm_g = gate
        gate_specs = [pl.BlockSpec((1, tm, k), lambda b, t: (b, t + skip_tiles, z_blk)), resident(norm_g.shape)]
        gate_args = [z_src, norm_g]
    return pl.pallas_call(
        functools.partial(_mix_ffn_body, n_chunk=n_chunk, alpha=alpha, gated=gate is not None),
        grid=(n_batch, t_tot // tm - skip_tiles),
        in_specs=[
            pl.BlockSpec((1, tm, k), lambda b, t: (b, t + y_skip, 0)),
            *gate_specs,
            resident((k, d)),
            pl.BlockSpec((1, tm, d), lambda b, t: (b, t + skip_tiles, 0)),
            pl.BlockSpec((1, N_MOD, d), lambda b, t: row_map(b, t + skip_tiles)),
            resident(ln.shape),
            resident((d, 2 * d_ff)),
            resident((d_ff, d)),
        ],
        out_specs=pl.BlockSpec((1, tm, d), lambda b, t: (b, t, 0)),
        out_shape=jax.ShapeDtypeStruct((n_batch, t_tot - skip_tiles * tm, d), F32),
        compiler_params=_params(("parallel", "parallel")),
        name="mix_ffn",
    )(y, *gate_args, w_o, xs, mods, ln, w_in, w_out)


def kernel(x, c, ctx, c_ctx, w_mod, b_mod, ln_g, ln_b, w_ffn_in, w_ffn_out, gdn_w_in, gdn_conv, gdn_a_log,
           gdn_dt_bias, gdn_norm_g, gdn_w_out, attn_w_qkv, attn_q_norm, attn_k_norm, attn_w_out):
    n_batch, seq_len, d = x.shape
    ctx_len = ctx.shape[1]
    depth = w_mod.shape[0]
    alpha = (2 * depth) ** 0.25
    n_hv = gdn_a_log.shape[-1]
    dv = gdn_norm_g.shape[-1]
    v_w = n_hv * dv
    qkv_w = gdn_conv.shape[-1]
    n_hk = (qkv_w - v_w) // (2 * GDN_DK)
    n_hq = attn_w_out.shape[1] // ATT_DH
    n_hkv = (attn_w_qkv.shape[-1] // ATT_DH - n_hq) // 2
    assert ctx_len % TOKEN_TILE == 0 and seq_len % TOKEN_TILE == 0 and seq_len % GRID_W == 0
    assert 4 * n_hv <= LANES and d % LANES == 0
    n_ctx_tiles = ctx_len // TOKEN_TILE

    xs = jnp.concatenate([ctx, x], axis=1)

    mp = -(-(n_batch + 1) // SUBLANES) * SUBLANES
    cond = jnp.concatenate([c, c_ctx[None, :], jnp.zeros((mp - n_batch - 1, d), F32)], axis=0)
    mods = _modulation(cond, w_mod, b_mod).reshape(depth, mp, N_MOD, d)
    tables = _rope_tables(ctx_len, seq_len)

    for i in range(depth):
        j = i // 2
        m = mods[i]
        skip = n_ctx_tiles if i == depth - 1 else 0
        if i % 2 == 0:
            w_in = gdn_w_in[j]
            gate_w = jnp.pad(w_in[:, qkv_w + v_w:], ((0, 0), (0, LANES - 4 * n_hv)))
            lanes_pad = (0, LANES - 4 * n_hv)
            a_log = jnp.pad(jnp.concatenate([jnp.zeros((2 * n_hv,), F32), gdn_a_log[j].reshape(-1)]), lanes_pad)
            dt_b = jnp.pad(jnp.concatenate([jnp.zeros((2 * n_hv,), F32), gdn_dt_bias[j].reshape(-1)]), lanes_pad)
            p_main, colg, rowg = _gdn_in_proj(xs, m, w_in[:, :qkv_w + v_w].astype(BF16), gate_w, gdn_conv[j],
                                              a_log[None, :], dt_b[None, :], n_ctx_tiles,
                                              qk_w=n_hk * GDN_DK, n_hv=n_hv, n_chunk=6)
            y = _gdn_delta(p_main, colg, rowg, dv, ctx_len, n_hk, n_hv)
            w_o = gdn_w_out[j]
            assert qkv_w % v_w == 0
            gate = (p_main, qkv_w // v_w, gdn_norm_g[j][None, :])
        else:
            p_qkv = _in_proj(xs, m, attn_w_qkv[j].astype(BF16), n_ctx_tiles, n_chunk=3)
            y = _attention(p_qkv, tables, attn_q_norm[j][None, :], attn_k_norm[j][None, :], ctx_len, n_hq, n_hkv,
                           skip_tiles=skip)
            w_o = attn_w_out[j]
            gate = None
        ln = jnp.stack([ln_g[i, 0], ln_b[i, 0], ln_g[i, 1], ln_b[i, 1]])
        xs = _mix_ffn(y, w_o.astype(BF16), xs, m, ln, w_ffn_in[i].astype(BF16), w_ffn_out[i].astype(BF16),
                      n_ctx_tiles, alpha, skip_tiles=skip, gate=gate)
    return xs
```

```python
import functools
import math

import jax
import jax.numpy as jnp
from jax import lax
from jax.experimental import pallas as pl
from jax.experimental.pallas import tpu as pltpu

F32 = jnp.float32
BF16 = jnp.bfloat16

N_MOD = 6
EPS = 1e-6
GDN_DK = 128
CHUNK = 64
ATT_DH = 128
GRID_W = 64
ROPE_THETA = 10000.0

LANES = 128
SUBLANES = 8
PAIR = 2 * CHUNK
VMEM_LIMIT_BYTES = 56 * 1024 * 1024
TOKEN_TILE = 256
LOG2E = math.log2(math.e)

assert PAIR == LANES


def _params(sem):
    return pltpu.CompilerParams(dimension_semantics=sem, vmem_limit_bytes=VMEM_LIMIT_BYTES)


def _dot(a, b):
    return jnp.dot(a, b, preferred_element_type=F32)


def _split2(a):
    hi = a.astype(BF16)
    return hi, (a - hi.astype(F32)).astype(BF16)


def _dot_hi(a, b):
    a1, a2 = _split2(a)
    b1, b2 = _split2(b)
    return _dot(a1, b1) + (_dot(a1, b2) + _dot(a2, b1))


def _sigmoid(x):
    return 1.0 / (1.0 + jnp.exp(-x))


def _silu(x):
    return x * _sigmoid(x)


def _layer_norm(r, g, b):
    mu = jnp.mean(r, axis=-1, keepdims=True)
    rc = r - mu
    var = jnp.mean(rc * rc, axis=-1, keepdims=True)
    return rc * lax.rsqrt(var + EPS) * g + b


def _mod_body(cond_ref, w_ref, b_ref, o_ref):
    o_ref[0] = _dot_hi(_silu(cond_ref[...]), w_ref[0]) + b_ref[0]


def _modulation(cond, w_mod, b_mod):
    n_layer, d, n = w_mod.shape
    mp = cond.shape[0]
    tn = n // 4
    return pl.pallas_call(
        _mod_body,
        grid=(n_layer, n // tn),
        in_specs=[
            pl.BlockSpec((mp, d), lambda l, j: (0, 0)),
            pl.BlockSpec((1, d, tn), lambda l, j: (l, 0, j)),
            pl.BlockSpec((1, 1, tn), lambda l, j: (l, 0, j)),
        ],
        out_specs=pl.BlockSpec((1, mp, tn), lambda l, j: (l, 0, j)),
        out_shape=jax.ShapeDtypeStruct((n_layer, mp, n), F32),
        compiler_params=_params(("parallel", "parallel")),
        name="modulation",
    )(cond, w_mod, b_mod.reshape(n_layer, 1, n))


def _mod_row_map(n_batch, n_ctx_tiles):
    return lambda b, t: (jnp.where(t < n_ctx_tiles, n_batch, b), 0, 0)


def _in_proj_body(x_ref, mod_ref, w_ref, o_ref, *, n_chunk):
    hb = (x_ref[0] * (1.0 + mod_ref[0, 1:2, :]) + mod_ref[0, 0:1, :]).astype(BF16)
    tn = w_ref.shape[1] // n_chunk
    for j in range(n_chunk):
        o_ref[0, :, j * tn:(j + 1) * tn] = _dot(hb, w_ref[:, j * tn:(j + 1) * tn]).astype(o_ref.dtype)


def _in_proj(xs, mods, w, n_ctx_tiles, n_chunk):
    n_batch, t_tot, d = xs.shape
    n = w.shape[1]
    tm = TOKEN_TILE
    return pl.pallas_call(
        functools.partial(_in_proj_body, n_chunk=n_chunk),
        grid=(n_batch, t_tot // tm),
        in_specs=[
            pl.BlockSpec((1, tm, d), lambda b, t: (b, t, 0)),
            pl.BlockSpec((1, N_MOD, d), _mod_row_map(n_batch, n_ctx_tiles)),
            pl.BlockSpec((d, n), lambda b, t: (0, 0)),
        ],
        out_specs=pl.BlockSpec((1, tm, n), lambda b, t: (b, t, 0)),
        out_shape=jax.ShapeDtypeStruct((n_batch, t_tot, n), BF16),
        compiler_params=_params(("parallel", "parallel")),
        name="in_proj",
    )(xs, mods, w)


CONV_HALO = SUBLANES


def _l2_norm(x):
    return x * lax.rsqrt(jnp.sum(x * x, axis=-1, keepdims=True) + EPS)


def _gdn_proj_body(x_ref, xp_ref, xn_ref, mod_ref, w_ref, wxh_ref, wxl_ref, conv_ref, alog_ref, dtb_ref,
                   o_ref, kt_ref, col_ref, row_ref, *, ctx_len, qk_w, qkv_w, n_chunk, dk, n_hv):
    tm = x_ref.shape[1]
    r0 = pl.program_id(1) * tm
    t_tot = pl.num_programs(1) * tm
    scale = 1.0 + mod_ref[0, 1:2, :]
    shift = mod_ref[0, 0:1, :]
    h = x_ref[0] * scale + shift
    hb = h.astype(BF16)
    hb_ext = jnp.concatenate([xp_ref[0] * scale + shift, h, xn_ref[0] * scale + shift], axis=0).astype(BF16)
    keep_lo = jnp.where(jnp.logical_or(r0 == 0, r0 == ctx_len), 0.0, 1.0)
    keep_hi = jnp.where(jnp.logical_or(r0 + tm == ctx_len, r0 + tm == t_tot), 0.0, 1.0)
    conv_w = conv_ref.shape[0]
    pad = conv_w // 2
    n_ext = tm + 2 * CONV_HALO
    mid = slice(CONV_HALO, CONV_HALO + tm)
    tn = w_ref.shape[1] // n_chunk
    for j in range(n_chunk):
        c0 = j * tn
        if c0 >= qkv_w:
            o_ref[0, :, c0:c0 + tn] = _dot(hb, w_ref[:, c0:c0 + tn]).astype(o_ref.dtype)
            continue
        p = _dot(hb_ext, w_ref[:, c0:c0 + tn])
        ext = jnp.concatenate([p[:CONV_HALO] * keep_lo, p[mid], p[CONV_HALO + tm:] * keep_hi], axis=0)
        y = None
        for tap in range(conv_w):
            off = tap - pad
            xs = ext if off == 0 else pltpu.roll(ext, (-off) % n_ext, axis=0)
            term = xs[mid] * conv_ref[tap:tap + 1, c0:c0 + tn]
            y = term if y is None else y + term
        y = _silu(y)
        if c0 >= 2 * qk_w:
            o_ref[0, :, c0:c0 + tn] = y.astype(o_ref.dtype)
            continue
        for hh in range(tn // dk):
            yh = _l2_norm(y[:, hh * dk:(hh + 1) * dk])
            if c0 < qk_w:
                yh = yh * (dk ** -0.5)
            o_ref[0, :, c0 + hh * dk:c0 + (hh + 1) * dk] = yh.astype(o_ref.dtype)
            if c0 >= qk_w:
                for cc in range(tm // CHUNK):
                    kc = yh[cc * CHUNK:(cc + 1) * CHUNK]
                    kt_ref[0, (c0 - qk_w) // dk + hh, cc] = jnp.concatenate([kc, kc], axis=0).T.astype(kt_ref.dtype)
    h_lo = (h - hb.astype(F32)).astype(BF16)
    raw = _dot(hb, wxh_ref[...]) + (_dot(hb, wxl_ref[...]) + _dot(h_lo, wxh_ref[...]))
    _gate_forms(raw, alog_ref[...], dtb_ref[...], col_ref, row_ref, n_hv)


def _gate_forms(raw, a_log, dt_bias, col_ref, row_ref, n_hv):
    beta = _sigmoid(raw)
    xs = raw + dt_bias
    g = -jnp.exp(a_log) * (jnp.maximum(xs, 0.0) + jnp.log1p(jnp.exp(-jnp.abs(xs))))
    r = lax.broadcasted_iota(jnp.int32, (PAIR, PAIR), 0)
    c = lax.broadcasted_iota(jnp.int32, (PAIR, PAIR), 1)
    same = (r < CHUNK) == (c < CHUNK)
    tri_f = jnp.where(jnp.logical_and(same, c <= r), 1.0, 0.0).astype(BF16)
    tri_r = jnp.where(jnp.logical_and(same, c >= r), 1.0, 0.0).astype(BF16)
    lane = lax.broadcasted_iota(jnp.int32, (PAIR, LANES), 1)
    for p in range(raw.shape[0] // PAIR):
        gp = g[p * PAIR:(p + 1) * PAIR]
        g1 = gp.astype(BF16)
        r1 = gp - g1.astype(F32)
        g2 = r1.astype(BF16)
        g3 = (r1 - g2.astype(F32)).astype(BF16)
        cum_f = _dot(tri_f, g1) + (_dot(tri_f, g2) + _dot(tri_f, g3))
        cum_r = _dot(tri_r, g1) + (_dot(tri_r, g2) + _dot(tri_r, g3))
        col = jnp.where(lane < 2 * n_hv, beta[p * PAIR:(p + 1) * PAIR],
                        jnp.where(lane < 3 * n_hv, cum_f, cum_r))
        col_ref[0, p * PAIR:(p + 1) * PAIR, :] = col
        row_ref[0, p] = col.T


def _gdn_in_proj(xs, mods, w, w_gate, conv, a_log_lanes, dt_bias_lanes, n_ctx_tiles, qk_w, n_hv, n_chunk):
    n_batch, t_tot, d = xs.shape
    n = w.shape[1]
    conv_w, qkv_w = conv.shape
    tm = TOKEN_TILE
    tn = n // n_chunk
    assert qk_w % tn == 0 and qkv_w % tn == 0 and tn % GDN_DK == 0 and tm % PAIR == 0
    halo_per_tile = tm // CONV_HALO
    last_halo = t_tot // CONV_HALO - 1
    w_gate_hi, w_gate_lo = _split2(w_gate)
    n_hk = qk_w // GDN_DK
    const = lambda shape: pl.BlockSpec(shape, lambda b, t: (0,) * len(shape))
    return pl.pallas_call(
        functools.partial(_gdn_proj_body, ctx_len=n_ctx_tiles * tm, qk_w=qk_w, qkv_w=qkv_w, n_chunk=n_chunk,
                          dk=GDN_DK, n_hv=n_hv),
        grid=(n_batch, t_tot // tm),
        in_specs=[
            pl.BlockSpec((1, tm, d), lambda b, t: (b, t, 0)),
            pl.BlockSpec((1, CONV_HALO, d), lambda b, t: (b, jnp.maximum(t * halo_per_tile - 1, 0), 0)),
            pl.BlockSpec((1, CONV_HALO, d), lambda b, t: (b, jnp.minimum((t + 1) * halo_per_tile, last_halo), 0)),
            pl.BlockSpec((1, N_MOD, d), _mod_row_map(n_batch, n_ctx_tiles)),
            const((d, n)), const((d, LANES)), const((d, LANES)), const((conv_w, qkv_w)),
            const((1, LANES)), const((1, LANES)),
        ],
        out_specs=[pl.BlockSpec((1, tm, n), lambda b, t: (b, t, 0)),
                   pl.BlockSpec((1, n_hk, tm // CHUNK, GDN_DK, 2 * CHUNK), lambda b, t: (b, 0, t, 0, 0)),
                   pl.BlockSpec((1, tm, LANES), lambda b, t: (b, t, 0)),
                   pl.BlockSpec((1, tm // PAIR, LANES, PAIR), lambda b, t: (b, t, 0, 0))],
        out_shape=[jax.ShapeDtypeStruct((n_batch, t_tot, n), BF16),
                   jax.ShapeDtypeStruct((n_batch, n_hk, t_tot // CHUNK, GDN_DK, 2 * CHUNK), BF16),
                   jax.ShapeDtypeStruct((n_batch, t_tot, LANES), F32),
                   jax.ShapeDtypeStruct((n_batch, t_tot // PAIR, LANES, PAIR), F32)],
        compiler_params=_params(("parallel", "parallel")),
        name="gdn_in_proj",
    )(xs, xs, xs, mods, w, w_gate_hi, w_gate_lo, conv, a_log_lanes, dt_bias_lanes)


STAGE_TILE = 256
HEADS_PER_STEP = 2
CHUNKS_PER_STEP = 4
SUPER = CHUNKS_PER_STEP * CHUNK


def _delta_body(q_ref, k_ref, v_ref, kt_ref, colg_ref, rowg_ref,
                y_ref, o_s, st_s, cg_s, uw_s, qkd_s, *, ctx_len, n_hv, n_m):
    g_idx = pl.program_id(1)
    t_tot = q_ref.shape[1]
    dk = q_ref.shape[2] // n_m
    dv = v_ref.shape[2] // (2 * n_m)
    n_sup = t_tot // SUPER
    n_ctx_sup = ctx_len // SUPER

    lane_shift = (LANES - 2 * n_m * g_idx) % LANES

    def prep(i, carry):
        rows = pl.ds(pl.multiple_of(i * STAGE_TILE, STAGE_TILE), STAGE_TILE)
        cg_s[rows, :] = pltpu.roll(colg_ref[0, rows, :], lane_shift, axis=1)
        return carry

    lax.fori_loop(0, t_tot // STAGE_TILE, prep, 0)
    o_s[...] = jnp.zeros_like(o_s)
    st_s[...] = jnp.zeros_like(st_s)

    r = lax.broadcasted_iota(jnp.int32, (CHUNK, LANES), 0)
    lane = lax.broadcasted_iota(jnp.int32, (CHUNK, LANES), 1)
    c = lane & (CHUNK - 1)
    left = lane < CHUNK
    left_row = lax.broadcasted_iota(jnp.int32, (1, LANES), 1) < CHUNK
    keep_left = jnp.where(left, 1.0, 0.0).astype(BF16)
    keep_right = jnp.where(left, 0.0, 1.0).astype(BF16)
    eye2 = jnp.where(c == r, 1.0, 0.0)
    n_lvl = CHUNK.bit_length() - 1
    lvl_mask = [jnp.logical_and((r >> (lg + 1)) == (c >> (lg + 1)), (r >> lg) != (c >> lg)) for lg in range(n_lvl)]
    incl = (c <= r, c >= r)
    strict = (c < r, c > r)
    zero_b = jnp.zeros((CHUNK, dv), BF16)

    def block_diag(y):
        return jnp.concatenate([y * keep_left, y * keep_right], axis=0)

    def reverse_super(i):
        return jnp.where(i < n_ctx_sup, n_ctx_sup - 1 - i, n_sup + n_ctx_sup - 1 - i)

    def make_units(i):
        sup = (i, reverse_super(i))
        units = []
        for m in range(n_m):
            for d in (0, 1):
                for seq in range(CHUNKS_PER_STEP):
                    cidx = seq if d == 0 else CHUNKS_PER_STEP - 1 - seq
                    units.append(dict(
                        m=m, d=d, seq=seq, half=cidx % 2, uid=(m * 2 + d) * CHUNKS_PER_STEP + seq,
                        pair=sup[d] * (SUPER // PAIR) + cidx // 2, chunk=sup[d] * CHUNKS_PER_STEP + cidx,
                        rows=pl.ds(pl.multiple_of(sup[d] * SUPER + cidx * CHUNK, CHUNK), CHUNK)))
        return units

    def load_gates(u):
        m, d, half = u["m"], u["d"], u["half"]
        cg = cg_s[u["rows"], :]
        b0, g0 = d * n_hv + 2 * m, (2 + d) * n_hv + 2 * m
        u["beta_c"] = (cg[:, b0:b0 + 1], cg[:, b0 + 1:b0 + 2])
        u["gc_c"] = (cg[:, g0:g0 + 1], cg[:, g0 + 1:g0 + 2])
        head0 = 2 * (n_m * g_idx + m)

        def packed_rows(base):
            rows = rowg_ref[0, u["pair"], pl.ds(base + head0, 2), :]
            a, b = rows[0:1], rows[1:2]
            if half == 0:
                return jnp.where(left_row, a, pltpu.roll(b, CHUNK, axis=1))
            return jnp.where(left_row, pltpu.roll(a, CHUNK, axis=1), b)

        u["beta_r"] = packed_rows(d * n_hv)
        u["gc_r"] = packed_rows((2 + d) * n_hv)

    def state_free_part(i, slot):
        units = make_units(i)
        for u in units:
            m = u["m"]
            load_gates(u)
            qb, kb = q_ref[0, u["rows"], m * dk:(m + 1) * dk], k_ref[0, u["rows"], m * dk:(m + 1) * dk]
            u["kb"] = kb
            u["gram"] = lax.dot_general(jnp.concatenate([kb, qb], axis=0), jnp.concatenate([kb, kb], axis=0),
                                        (((1,), (1,)), ((), ())), preferred_element_type=F32)
        yield
        for u in units:
            d = u["d"]
            gc_c2 = jnp.where(left, u["gc_c"][0], u["gc_c"][1])
            beta_c2 = jnp.where(left, u["beta_c"][0], u["beta_c"][1])
            decay = jnp.where(incl[d], jnp.exp(jnp.where(incl[d], gc_c2 - u["gc_r"], 0.0)), 0.0)
            a = jnp.where(strict[d], u["gram"][0:CHUNK] * decay, 0.0) * beta_c2
            qkd_s[slot, u["uid"]] = (u["gram"][CHUNK:2 * CHUNK] * decay).astype(BF16)
            u["a"] = a
            u["x"] = eye2 - jnp.where(lvl_mask[0], a, 0.0)
        yield
        for lg in range(1, n_lvl):
            for u in units:
                u["xb"] = u["x"].astype(BF16)
                off = jnp.where(lvl_mask[lg], u["a"], 0.0).astype(BF16)
                u["z"] = _dot(off, block_diag(u["xb"]))
            yield
            for u in units:
                u["x"] = u["x"] - _dot(u["xb"], block_diag(u["z"].astype(BF16)))
            yield
        for u in units:
            m = u["m"]
            tb = u["x"] * u["beta_r"]
            tbe = tb * jnp.exp(u["gc_r"])
            v2 = v_ref[0, u["rows"], 2 * m * dv:2 * (m + 1) * dv]
            v_bd = jnp.concatenate([jnp.concatenate([v2[:, :dv], zero_b], axis=1),
                                    jnp.concatenate([zero_b, v2[:, dv:]], axis=1)], axis=0)
            k_bd = jnp.concatenate([jnp.concatenate([u["kb"], zero_b], axis=1),
                                    jnp.concatenate([zero_b, u["kb"]], axis=1)], axis=0)
            uw_s[slot, u["uid"], :, :2 * dv] = _dot(tb.astype(BF16), v_bd)
            uw_s[slot, u["uid"], :, 2 * dv:] = _dot(tbe.astype(BF16), k_bd)
        yield

    def recurrence_part(i, slot):
        units = make_units(i)
        for seq in range(CHUNKS_PER_STEP):
            now = [u for u in units if u["seq"] == seq]
            for u in now:
                m = u["m"]
                load_gates(u)
                uw = uw_s[slot, u["uid"]]
                u["u"] = uw[:, :2 * dv]
                u["s2"] = st_s[2 * m + u["d"]]
                lhs = jnp.concatenate([uw[:, 2 * dv:3 * dv].astype(BF16), uw[:, 3 * dv:].astype(BF16),
                                       q_ref[0, u["rows"], m * dk:(m + 1) * dk]], axis=0)
                u["ws"] = _dot(lhs, u["s2"].astype(BF16))
            yield
            for u in now:
                d, ws = u["d"], u["ws"]
                vn0 = (u["u"][:, :dv] - ws[0:CHUNK, :dv]).astype(BF16)
                vn1 = (u["u"][:, dv:] - ws[CHUNK:2 * CHUNK, dv:]).astype(BF16)
                vn_bd = jnp.concatenate([jnp.concatenate([vn0, zero_b], axis=1),
                                         jnp.concatenate([zero_b, vn1], axis=1)], axis=0)
                last = (CHUNK - 1, 2 * CHUNK - 1) if d == 0 else (0, CHUNK)
                u["gl"] = [u["gc_r"][:, l:l + 1] for l in last]
                gl2 = jnp.where(left_row, u["gl"][0], u["gl"][1])
                kdt = kt_ref[0, u["m"], u["chunk"]].astype(F32) * jnp.exp(gl2 - u["gc_r"])
                lhs = jnp.concatenate([qkd_s[slot, u["uid"]], kdt.astype(BF16)], axis=0)
                u["ov"] = _dot(lhs, vn_bd)
            yield
            for u in now:
                m, ws, ov = u["m"], u["ws"], u["ov"]
                qs = jnp.concatenate([ws[2 * CHUNK:, :dv] * jnp.exp(u["gc_c"][0]),
                                      ws[2 * CHUNK:, dv:] * jnp.exp(u["gc_c"][1])], axis=1)
                o_s[u["rows"], 2 * m * dv:2 * (m + 1) * dv] += qs + ov[0:CHUNK]
                egl = jnp.concatenate([jnp.broadcast_to(jnp.exp(u["gl"][0]), (1, dv)),
                                       jnp.broadcast_to(jnp.exp(u["gl"][1]), (1, dv))], axis=1)
                st_s[2 * m + u["d"]] = u["s2"] * egl + ov[CHUNK:]
            yield

    def run_interleaved(parts):
        live = list(parts)
        while live:
            live = [g for g in live if next(g, "done") != "done"]

    def step(i, carry):
        slot = i & 1
        run_interleaved([state_free_part(i, slot), recurrence_part(i - 1, 1 - slot)])
        return carry

    run_interleaved([state_free_part(jnp.int32(0), 0)])
    lax.fori_loop(1, n_sup, step, 0)
    run_interleaved([recurrence_part(jnp.int32(n_sup - 1), (n_sup - 1) & 1)])

    def finish(i, carry):
        rows = pl.ds(pl.multiple_of(i * STAGE_TILE, STAGE_TILE), STAGE_TILE)
        y_ref[0, rows, :] = o_s[rows, :].astype(y_ref.dtype)
        return carry

    lax.fori_loop(0, t_tot // STAGE_TILE, finish, 0)


def _gdn_delta(p_main, k_t, colg, rowg, dv, ctx_len, n_hk, n_hv):
    n_batch, t_tot, _ = p_main.shape
    dk = GDN_DK
    assert n_hv == 2 * n_hk and dk == dv == LANES and n_hk % HEADS_PER_STEP == 0
    assert ctx_len % SUPER == 0 and t_tot % SUPER == 0 and SUPER % STAGE_TILE == 0
    n_m = HEADS_PER_STEP
    qb = n_m * dk
    vb = 2 * n_m * dv
    k_blk0 = n_hk // n_m
    v_blk0 = (2 * n_hk * dk) // vb
    n_pairs = t_tot // PAIR
    n_units = 2 * n_m * CHUNKS_PER_STEP
    return pl.pallas_call(
        functools.partial(_delta_body, ctx_len=ctx_len, n_hv=n_hv, n_m=n_m),
        grid=(n_batch, n_hk // n_m),
        in_specs=[
            pl.BlockSpec((1, t_tot, qb), lambda b, h: (b, 0, h)),
            pl.BlockSpec((1, t_tot, qb), lambda b, h: (b, 0, k_blk0 + h)),
            pl.BlockSpec((1, t_tot, vb), lambda b, h: (b, 0, v_blk0 + h)),
            pl.BlockSpec((1, n_m, t_tot // CHUNK, dk, 2 * CHUNK), lambda b, h: (b, h, 0, 0, 0)),
            pl.BlockSpec((1, t_tot, LANES), lambda b, h: (b, 0, 0)),
            pl.BlockSpec((1, n_pairs, LANES, PAIR), lambda b, h: (b, 0, 0, 0)),
        ],
        out_specs=pl.BlockSpec((1, t_tot, vb), lambda b, h: (b, 0, h)),
        out_shape=jax.ShapeDtypeStruct((n_batch, t_tot, n_hv * dv), BF16),
        scratch_shapes=[
            pltpu.VMEM((t_tot, vb), F32),
            pltpu.VMEM((2 * n_m, dk, 2 * dv), F32),
            pltpu.VMEM((t_tot, LANES), F32),
            pltpu.VMEM((2, n_units, CHUNK, 2 * dv + 2 * dk), F32),
            pltpu.VMEM((2, n_units, CHUNK, 2 * CHUNK), BF16),
        ],
        compiler_params=_params(("parallel", "parallel")),
        name="gdn_delta",
    )(p_main, p_main, p_main, k_t, colg, rowg)


def _rope(x, cos, sin_a, sin_b):
    quarter = ATT_DH // 4
    return x * cos + pltpu.roll(x, ATT_DH - quarter, axis=1) * sin_a + pltpu.roll(x, quarter, axis=1) * sin_b


def _rms_norm(x, g):
    return x * lax.rsqrt(jnp.mean(x * x, axis=-1, keepdims=True) + EPS) * g


def _attn_body(q_ref, k_ref, v_ref, cos_ref, sa_ref, sb_ref, qn_ref, kn_ref, o_ref, kt_s, v1_s,
               *, ctx_len, n_group, skip_tiles):
    qi = pl.program_id(2)
    tq = q_ref.shape[1]
    t_tot = k_ref.shape[1]
    dh = k_ref.shape[2]

    @pl.when(qi == 0)
    def _():
        k = _rms_norm(k_ref[0].astype(F32), kn_ref[...])
        k = _rope(k, cos_ref[...], sa_ref[...], sb_ref[...])
        kt_s[...] = k.T.astype(kt_s.dtype)
        v1_s[:, :dh] = v_ref[0]
        v1_s[:, dh:] = jnp.ones((t_tot, dh), v1_s.dtype)

    r0 = pl.multiple_of((qi + skip_tiles) * tq, tq)
    cos = cos_ref[pl.ds(r0, tq), :]
    sin_a = sa_ref[pl.ds(r0, tq), :]
    sin_b = sb_ref[pl.ds(r0, tq), :]

    def attend(n_keys):
        def scores(g):
            q = _rms_norm(q_ref[0, :, g * dh:(g + 1) * dh].astype(F32), qn_ref[...])
            q = _rope(q, cos, sin_a, sin_b) * (dh ** -0.5 * LOG2E)
            return _dot(q.astype(BF16), kt_s[:, :n_keys])

        s_next = scores(0)
        for g in range(n_group):
            s = s_next
            if g + 1 < n_group:
                s_next = scores(g + 1)
            p = jnp.exp2(s - jnp.max(s, axis=-1, keepdims=True))
            num_den = _dot(p.astype(BF16), v1_s[:n_keys, :])
            o_ref[0, :, g * dh:(g + 1) * dh] = (num_den[:, :dh] / num_den[:, dh:]).astype(o_ref.dtype)

    @pl.when(r0 < ctx_len)
    def _():
        attend(ctx_len)

    @pl.when(r0 >= ctx_len)
    def _():
        attend(t_tot)


def _attention(p_qkv, tables, q_norm, k_norm, ctx_len, n_hq, n_hkv, skip_tiles=0):
    n_batch, t_tot, _ = p_qkv.shape
    dh = ATT_DH
    n_group = n_hq // n_hkv
    tq = TOKEN_TILE
    cos, sin_a, sin_b = tables
    full = lambda shape: pl.BlockSpec(shape, lambda b, h, q: (0,) * len(shape))
    return pl.pallas_call(
        functools.partial(_attn_body, ctx_len=ctx_len, n_group=n_group, skip_tiles=skip_tiles),
        grid=(n_batch, n_hkv, t_tot // tq - skip_tiles),
        in_specs=[
            pl.BlockSpec((1, tq, n_group * dh), lambda b, h, q: (b, q + skip_tiles, h)),
            pl.BlockSpec((1, t_tot, dh), lambda b, h, q: (b, 0, n_hq + h)),
            pl.BlockSpec((1, t_tot, dh), lambda b, h, q: (b, 0, n_hq + n_hkv + h)),
            full((t_tot, dh)), full((t_tot, dh)), full((t_tot, dh)),
            full((1, dh)), full((1, dh)),
        ],
        out_specs=pl.BlockSpec((1, tq, n_group * dh), lambda b, h, q: (b, q, h)),
        out_shape=jax.ShapeDtypeStruct((n_batch, t_tot - skip_tiles * tq, n_hq * dh), BF16),
        scratch_shapes=[pltpu.VMEM((dh, t_tot), BF16), pltpu.VMEM((t_tot, 2 * dh), BF16)],
        compiler_params=_params(("parallel", "parallel", "arbitrary")),
        name="attention",
    )(p_qkv, p_qkv, p_qkv, cos, sin_a, sin_b, q_norm, k_norm)


def _rope_tables(ctx_len, seq_len):
    rows = seq_len // GRID_W
    row = jnp.repeat(jnp.arange(rows), GRID_W).astype(F32)
    col = jnp.tile(jnp.arange(GRID_W), rows).astype(F32)
    n_freq = ATT_DH // 4
    freqs = ROPE_THETA ** (-jnp.arange(n_freq, dtype=F32) / n_freq)
    ang_r = row[:, None] * freqs
    ang_c = col[:, None] * freqs
    ang = jnp.concatenate([ang_r, ang_r, ang_c, ang_c], axis=-1)
    cos = jnp.concatenate([jnp.ones((ctx_len, ATT_DH), F32), jnp.cos(ang)], axis=0)
    sin = jnp.concatenate([jnp.zeros((ctx_len, ATT_DH), F32), jnp.sin(ang)], axis=0)
    first_half = (jnp.arange(ATT_DH) % (2 * n_freq)) < n_freq
    return cos, jnp.where(first_half, -sin, 0.0), jnp.where(first_half, 0.0, sin)


def _mix_ffn_body(y_ref, *refs, n_chunk, alpha, gated):
    if gated:
        z_ref, ng_ref, wo_ref, x_ref, mod_ref, ln_ref, win_ref, wout_ref, o_ref = refs
        dv = ng_ref.shape[1]
        heads = []
        for hh in range(y_ref.shape[2] // dv):
            o = y_ref[0, :, hh * dv:(hh + 1) * dv].astype(F32)
            z = z_ref[0, :, hh * dv:(hh + 1) * dv].astype(F32)
            heads.append((_rms_norm(o, ng_ref[...]) * _silu(z)).astype(BF16))
        y = jnp.concatenate(heads, axis=1)
    else:
        wo_ref, x_ref, mod_ref, ln_ref, win_ref, wout_ref, o_ref = refs
        y = y_ref[0]
    delta = _dot(y, wo_ref[...])
    x1 = _layer_norm(alpha * x_ref[0] + mod_ref[0, 2:3, :] * delta, ln_ref[0:1, :], ln_ref[1:2, :])
    hb = (x1 * (1.0 + mod_ref[0, 4:5, :]) + mod_ref[0, 3:4, :]).astype(BF16)
    d_ff = wout_ref.shape[0]
    tc = d_ff // n_chunk
    acc = None
    for c in range(n_chunk):
        gate = _dot(hb, win_ref[:, c * tc:(c + 1) * tc])
        up = _dot(hb, win_ref[:, d_ff + c * tc:d_ff + (c + 1) * tc])
        part = _dot((_silu(gate) * up).astype(BF16), wout_ref[c * tc:(c + 1) * tc, :])
        acc = part if acc is None else acc + part
    o_ref[0] = _layer_norm(alpha * x1 + mod_ref[0, 5:6, :] * acc, ln_ref[2:3, :], ln_ref[3:4, :])


def _mix_ffn(y, w_o, xs, mods, ln, w_in, w_out, n_ctx_tiles, alpha, skip_tiles=0, gate=None):
    n_batch, t_tot, d = xs.shape
    k = y.shape[-1]
    d_ff = w_out.shape[0]
    tm = TOKEN_TILE
    n_chunk = 2 if d_ff % (2 * LANES) == 0 else 1
    row_map = _mod_row_map(n_batch, n_ctx_tiles)
    y_skip = skip_tiles - (t_tot - y.shape[1]) // tm
    resident = lambda shape: pl.BlockSpec(shape, lambda b, t: (0,) * len(shape), pipeline_mode=pl.Buffered(1))
    gate_specs, gate_args = [], []
    if gate is not None:
        z_src, z_blk, norm_g = gate
        gate_specs = [pl.BlockSpec((1, tm, k), lambda b, t: (b, t + skip_tiles, z_blk)), resident(norm_g.shape)]
        gate_args = [z_src, norm_g]
    return pl.pallas_call(
        functools.partial(_mix_ffn_body, n_chunk=n_chunk, alpha=alpha, gated=gate is not None),
        grid=(n_batch, t_tot // tm - skip_tiles),
        in_specs=[
            pl.BlockSpec((1, tm, k), lambda b, t: (b, t + y_skip, 0)),
            *gate_specs,
            resident((k, d)),
            pl.BlockSpec((1, tm, d), lambda b, t: (b, t + skip_tiles, 0)),
            pl.BlockSpec((1, N_MOD, d), lambda b, t: row_map(b, t + skip_tiles)),
            resident(ln.shape),
            resident((d, 2 * d_ff)),
            resident((d_ff, d)),
        ],
        out_specs=pl.BlockSpec((1, tm, d), lambda b, t: (b, t, 0)),
        out_shape=jax.ShapeDtypeStruct((n_batch, t_tot - skip_tiles * tm, d), F32),
        compiler_params=_params(("parallel", "parallel")),
        name="mix_ffn",
    )(y, *gate_args, w_o, xs, mods, ln, w_in, w_out)


def kernel(x, c, ctx, c_ctx, w_mod, b_mod, ln_g, ln_b, w_ffn_in, w_ffn_out, gdn_w_in, gdn_conv, gdn_a_log,
           gdn_dt_bias, gdn_norm_g, gdn_w_out, attn_w_qkv, attn_q_norm, attn_k_norm, attn_w_out):
    n_batch, seq_len, d = x.shape
    ctx_len = ctx.shape[1]
    depth = w_mod.shape[0]
    alpha = (2 * depth) ** 0.25
    n_hv = gdn_a_log.shape[-1]
    dv = gdn_norm_g.shape[-1]
    v_w = n_hv * dv
    qkv_w = gdn_conv.shape[-1]
    n_hk = (qkv_w - v_w) // (2 * GDN_DK)
    n_hq = attn_w_out.shape[1] // ATT_DH
    n_hkv = (attn_w_qkv.shape[-1] // ATT_DH - n_hq) // 2
    assert ctx_len % TOKEN_TILE == 0 and seq_len % TOKEN_TILE == 0 and seq_len % GRID_W == 0
    assert 4 * n_hv <= LANES and d % LANES == 0
    n_ctx_tiles = ctx_len // TOKEN_TILE

    xs = jnp.concatenate([ctx, x], axis=1)

    mp = -(-(n_batch + 1) // SUBLANES) * SUBLANES
    cond = jnp.concatenate([c, c_ctx[None, :], jnp.zeros((mp - n_batch - 1, d), F32)], axis=0)
    mods = _modulation(cond, w_mod, b_mod).reshape(depth, mp, N_MOD, d)
    tables = _rope_tables(ctx_len, seq_len)

    for i in range(depth):
        j = i // 2
        m = mods[i]
        skip = n_ctx_tiles if i == depth - 1 else 0
        if i % 2 == 0:
            w_in = gdn_w_in[j]
            gate_w = jnp.pad(w_in[:, qkv_w + v_w:], ((0, 0), (0, LANES - 4 * n_hv)))
            lanes_pad = (0, LANES - 4 * n_hv)
            a_log = jnp.pad(jnp.concatenate([jnp.zeros((2 * n_hv,), F32), gdn_a_log[j].reshape(-1)]), lanes_pad)
            dt_b = jnp.pad(jnp.concatenate([jnp.zeros((2 * n_hv,), F32), gdn_dt_bias[j].reshape(-1)]), lanes_pad)
            p_main, k_t, colg, rowg = _gdn_in_proj(xs, m, w_in[:, :qkv_w + v_w].astype(BF16), gate_w, gdn_conv[j],
                                                   a_log[None, :], dt_b[None, :], n_ctx_tiles,
                                                   qk_w=n_hk * GDN_DK, n_hv=n_hv, n_chunk=6)
            y = _gdn_delta(p_main, k_t, colg, rowg, dv, ctx_len, n_hk, n_hv)
            w_o = gdn_w_out[j]
            assert qkv_w % v_w == 0
            gate = (p_main, qkv_w // v_w, gdn_norm_g[j][None, :])
        else:
            p_qkv = _in_proj(xs, m, attn_w_qkv[j].astype(BF16), n_ctx_tiles, n_chunk=3)
            y = _attention(p_qkv, tables, attn_q_norm[j][None, :], attn_k_norm[j][None, :], ctx_len, n_hq, n_hkv,
                           skip_tiles=skip)
            w_o = attn_w_out[j]
            gate = None
        ln = jnp.stack([ln_g[i, 0], ln_b[i, 0], ln_g[i, 1], ln_b[i, 1]])
        xs = _mix_ffn(y, w_o.astype(BF16), xs, m, ln, w_ffn_in[i].astype(BF16), w_ffn_out[i].astype(BF16),
                      n_ctx_tiles, alpha, skip_tiles=skip, gate=gate)
    return xs
```

```python
import functools
import math

import jax
import jax.numpy as jnp
from jax import lax
from jax.experimental import pallas as pl
from jax.experimental.pallas import tpu as pltpu

F32 = jnp.float32
BF16 = jnp.bfloat16

N_MOD = 6
EPS = 1e-6
GDN_DK = 128
CHUNK = 64
ATT_DH = 128
GRID_W = 64
ROPE_THETA = 10000.0

LANES = 128
SUBLANES = 8
PAIR = 2 * CHUNK
VMEM_LIMIT_BYTES = 56 * 1024 * 1024
TOKEN_TILE = 256
LOG2E = math.log2(math.e)

assert PAIR == LANES


def _params(sem):
    return pltpu.CompilerParams(dimension_semantics=sem, vmem_limit_bytes=VMEM_LIMIT_BYTES)


def _dot(a, b):
    return jnp.dot(a, b, preferred_element_type=F32)


def _split2(a):
    hi = a.astype(BF16)
    return hi, (a - hi.astype(F32)).astype(BF16)


def _dot_hi(a, b):
    a1, a2 = _split2(a)
    b1, b2 = _split2(b)
    return _dot(a1, b1) + (_dot(a1, b2) + _dot(a2, b1))


def _sigmoid(x):
    return 1.0 / (1.0 + jnp.exp(-x))


def _silu(x):
    return x * _sigmoid(x)


def _layer_norm(r, g, b):
    mu = jnp.mean(r, axis=-1, keepdims=True)
    rc = r - mu
    var = jnp.mean(rc * rc, axis=-1, keepdims=True)
    return rc * lax.rsqrt(var + EPS) * g + b


def _mod_body(cond_ref, w_ref, b_ref, o_ref):
    o_ref[0] = _dot_hi(_silu(cond_ref[...]), w_ref[0]) + b_ref[0]


def _modulation(cond, w_mod, b_mod):
    n_layer, d, n = w_mod.shape
    mp = cond.shape[0]
    tn = n // 4
    return pl.pallas_call(
        _mod_body,
        grid=(n_layer, n // tn),
        in_specs=[
            pl.BlockSpec((mp, d), lambda l, j: (0, 0)),
            pl.BlockSpec((1, d, tn), lambda l, j: (l, 0, j)),
            pl.BlockSpec((1, 1, tn), lambda l, j: (l, 0, j)),
        ],
        out_specs=pl.BlockSpec((1, mp, tn), lambda l, j: (l, 0, j)),
        out_shape=jax.ShapeDtypeStruct((n_layer, mp, n), F32),
        compiler_params=_params(("parallel", "parallel")),
        name="modulation",
    )(cond, w_mod, b_mod.reshape(n_layer, 1, n))


def _mod_row_map(n_batch, n_ctx_tiles):
    return lambda b, t: (jnp.where(t < n_ctx_tiles, n_batch, b), 0, 0)


def _in_proj_body(x_ref, mod_ref, w_ref, o_ref, *, n_chunk):
    hb = (x_ref[0] * (1.0 + mod_ref[0, 1:2, :]) + mod_ref[0, 0:1, :]).astype(BF16)
    tn = w_ref.shape[1] // n_chunk
    for j in range(n_chunk):
        o_ref[0, :, j * tn:(j + 1) * tn] = _dot(hb, w_ref[:, j * tn:(j + 1) * tn]).astype(o_ref.dtype)


def _in_proj(xs, mods, w, n_ctx_tiles, n_chunk):
    n_batch, t_tot, d = xs.shape
    n = w.shape[1]
    tm = TOKEN_TILE
    return pl.pallas_call(
        functools.partial(_in_proj_body, n_chunk=n_chunk),
        grid=(n_batch, t_tot // tm),
        in_specs=[
            pl.BlockSpec((1, tm, d), lambda b, t: (b, t, 0)),
            pl.BlockSpec((1, N_MOD, d), _mod_row_map(n_batch, n_ctx_tiles)),
            pl.BlockSpec((d, n), lambda b, t: (0, 0)),
        ],
        out_specs=pl.BlockSpec((1, tm, n), lambda b, t: (b, t, 0)),
        out_shape=jax.ShapeDtypeStruct((n_batch, t_tot, n), BF16),
        compiler_params=_params(("parallel", "parallel")),
        name="in_proj",
    )(xs, mods, w)


CONV_HALO = SUBLANES


def _l2_norm(x):
    return x * lax.rsqrt(jnp.sum(x * x, axis=-1, keepdims=True) + EPS)


def _gdn_proj_body(x_ref, xp_ref, xn_ref, mod_ref, w_ref, wxh_ref, wxl_ref, conv_ref, alog_ref, dtb_ref,
                   o_ref, kt_ref, col_ref, row_ref, *, ctx_len, qk_w, qkv_w, n_chunk, dk, n_hv):
    tm = x_ref.shape[1]
    r0 = pl.program_id(1) * tm
    t_tot = pl.num_programs(1) * tm
    scale = 1.0 + mod_ref[0, 1:2, :]
    shift = mod_ref[0, 0:1, :]
    h = x_ref[0] * scale + shift
    hb = h.astype(BF16)
    hb_ext = jnp.concatenate([xp_ref[0] * scale + shift, h, xn_ref[0] * scale + shift], axis=0).astype(BF16)
    keep_lo = jnp.where(jnp.logical_or(r0 == 0, r0 == ctx_len), 0.0, 1.0)
    keep_hi = jnp.where(jnp.logical_or(r0 + tm == ctx_len, r0 + tm == t_tot), 0.0, 1.0)
    conv_w = conv_ref.shape[0]
    pad = conv_w // 2
    n_ext = tm + 2 * CONV_HALO
    mid = slice(CONV_HALO, CONV_HALO + tm)
    tn = w_ref.shape[1] // n_chunk
    for j in range(n_chunk):
        c0 = j * tn
        if c0 >= qkv_w:
            o_ref[0, :, c0:c0 + tn] = _dot(hb, w_ref[:, c0:c0 + tn]).astype(o_ref.dtype)
            continue
        p = _dot(hb_ext, w_ref[:, c0:c0 + tn])
        ext = jnp.concatenate([p[:CONV_HALO] * keep_lo, p[mid], p[CONV_HALO + tm:] * keep_hi], axis=0)
        y = None
        for tap in range(conv_w):
            off = tap - pad
            xs = ext if off == 0 else pltpu.roll(ext, (-off) % n_ext, axis=0)
            term = xs[mid] * conv_ref[tap:tap + 1, c0:c0 + tn]
            y = term if y is None else y + term
        y = _silu(y)
        if c0 >= 2 * qk_w:
            o_ref[0, :, c0:c0 + tn] = y.astype(o_ref.dtype)
            continue
        for hh in range(tn // dk):
            yh = _l2_norm(y[:, hh * dk:(hh + 1) * dk])
            if c0 < qk_w:
                yh = yh * (dk ** -0.5)
            o_ref[0, :, c0 + hh * dk:c0 + (hh + 1) * dk] = yh.astype(o_ref.dtype)
            if c0 >= qk_w:
                for cc in range(tm // CHUNK):
                    kc = yh[cc * CHUNK:(cc + 1) * CHUNK]
                    kt_ref[0, (c0 - qk_w) // dk + hh, cc] = jnp.concatenate([kc, kc], axis=0).T.astype(kt_ref.dtype)
    h_lo = (h - hb.astype(F32)).astype(BF16)
    raw = _dot(hb, wxh_ref[...]) + (_dot(hb, wxl_ref[...]) + _dot(h_lo, wxh_ref[...]))
    _gate_forms(raw, alog_ref[...], dtb_ref[...], col_ref, row_ref, n_hv)


def _gate_forms(raw, a_log, dt_bias, col_ref, row_ref, n_hv):
    beta = _sigmoid(raw)
    xs = raw + dt_bias
    g = -jnp.exp(a_log) * (jnp.maximum(xs, 0.0) + jnp.log1p(jnp.exp(-jnp.abs(xs))))
    r = lax.broadcasted_iota(jnp.int32, (PAIR, PAIR), 0)
    c = lax.broadcasted_iota(jnp.int32, (PAIR, PAIR), 1)
    same = (r < CHUNK) == (c < CHUNK)
    tri_f = jnp.where(jnp.logical_and(same, c <= r), 1.0, 0.0).astype(BF16)
    tri_r = jnp.where(jnp.logical_and(same, c >= r), 1.0, 0.0).astype(BF16)
    lane = lax.broadcasted_iota(jnp.int32, (PAIR, LANES), 1)
    for p in range(raw.shape[0] // PAIR):
        gp = g[p * PAIR:(p + 1) * PAIR]
        g1 = gp.astype(BF16)
        r1 = gp - g1.astype(F32)
        g2 = r1.astype(BF16)
        g3 = (r1 - g2.astype(F32)).astype(BF16)
        cum_f = _dot(tri_f, g1) + (_dot(tri_f, g2) + _dot(tri_f, g3))
        cum_r = _dot(tri_r, g1) + (_dot(tri_r, g2) + _dot(tri_r, g3))
        col = jnp.where(lane < 2 * n_hv, beta[p * PAIR:(p + 1) * PAIR],
                        jnp.where(lane < 3 * n_hv, cum_f, cum_r))
        for grp in range(col_ref.shape[1]):
            shift = (LANES - 2 * HEADS_PER_STEP * grp) % LANES
            col_ref[0, grp, p * PAIR:(p + 1) * PAIR, :] = col if shift == 0 else pltpu.roll(col, shift, axis=1)
        row_ref[0, p] = col.T


def _gdn_in_proj(xs, mods, w, w_gate, conv, a_log_lanes, dt_bias_lanes, n_ctx_tiles, qk_w, n_hv, n_chunk):
    n_batch, t_tot, d = xs.shape
    n = w.shape[1]
    conv_w, qkv_w = conv.shape
    tm = TOKEN_TILE
    tn = n // n_chunk
    assert qk_w % tn == 0 and qkv_w % tn == 0 and tn % GDN_DK == 0 and tm % PAIR == 0
    halo_per_tile = tm // CONV_HALO
    last_halo = t_tot // CONV_HALO - 1
    w_gate_hi, w_gate_lo = _split2(w_gate)
    n_hk = qk_w // GDN_DK
    const = lambda shape: pl.BlockSpec(shape, lambda b, t: (0,) * len(shape))
    return pl.pallas_call(
        functools.partial(_gdn_proj_body, ctx_len=n_ctx_tiles * tm, qk_w=qk_w, qkv_w=qkv_w, n_chunk=n_chunk,
                          dk=GDN_DK, n_hv=n_hv),
        grid=(n_batch, t_tot // tm),
        in_specs=[
            pl.BlockSpec((1, tm, d), lambda b, t: (b, t, 0)),
            pl.BlockSpec((1, CONV_HALO, d), lambda b, t: (b, jnp.maximum(t * halo_per_tile - 1, 0), 0)),
            pl.BlockSpec((1, CONV_HALO, d), lambda b, t: (b, jnp.minimum((t + 1) * halo_per_tile, last_halo), 0)),
            pl.BlockSpec((1, N_MOD, d), _mod_row_map(n_batch, n_ctx_tiles)),
            const((d, n)), const((d, LANES)), const((d, LANES)), const((conv_w, qkv_w)),
            const((1, LANES)), const((1, LANES)),
        ],
        out_specs=[pl.BlockSpec((1, tm, n), lambda b, t: (b, t, 0)),
                   pl.BlockSpec((1, n_hk, tm // CHUNK, GDN_DK, 2 * CHUNK), lambda b, t: (b, 0, t, 0, 0)),
                   pl.BlockSpec((1, n_hk // HEADS_PER_STEP, tm, LANES), lambda b, t: (b, 0, t, 0)),
                   pl.BlockSpec((1, tm // PAIR, LANES, PAIR), lambda b, t: (b, t, 0, 0))],
        out_shape=[jax.ShapeDtypeStruct((n_batch, t_tot, n), BF16),
                   jax.ShapeDtypeStruct((n_batch, n_hk, t_tot // CHUNK, GDN_DK, 2 * CHUNK), BF16),
                   jax.ShapeDtypeStruct((n_batch, n_hk // HEADS_PER_STEP, t_tot, LANES), F32),
                   jax.ShapeDtypeStruct((n_batch, t_tot // PAIR, LANES, PAIR), F32)],
        compiler_params=_params(("parallel", "parallel")),
        name="gdn_in_proj",
    )(xs, xs, xs, mods, w, w_gate_hi, w_gate_lo, conv, a_log_lanes, dt_bias_lanes)


STAGE_TILE = 256
HEADS_PER_STEP = 2
CHUNKS_PER_STEP = 4
SUPER = CHUNKS_PER_STEP * CHUNK


def _delta_body(q_ref, k_ref, v_ref, kt_ref, colg_ref, rowg_ref,
                y_ref, o_s, st_s, uw_s, qkd_s, *, ctx_len, n_hv, n_m):
    g_idx = pl.program_id(1)
    t_tot = q_ref.shape[1]
    dk = q_ref.shape[2] // n_m
    dv = v_ref.shape[2] // (2 * n_m)
    n_sup = t_tot // SUPER
    n_ctx_sup = ctx_len // SUPER

    st_s[...] = jnp.zeros_like(st_s)

    r = lax.broadcasted_iota(jnp.int32, (CHUNK, LANES), 0)
    lane = lax.broadcasted_iota(jnp.int32, (CHUNK, LANES), 1)
    c = lane & (CHUNK - 1)
    left = lane < CHUNK
    left_row = lax.broadcasted_iota(jnp.int32, (1, LANES), 1) < CHUNK
    keep_left = jnp.where(left, 1.0, 0.0).astype(BF16)
    keep_right = jnp.where(left, 0.0, 1.0).astype(BF16)
    eye2 = jnp.where(c == r, 1.0, 0.0)
    n_lvl = CHUNK.bit_length() - 1
    lvl_mask = [jnp.logical_and((r >> (lg + 1)) == (c >> (lg + 1)), (r >> lg) != (c >> lg)) for lg in range(n_lvl)]
    incl = (c <= r, c >= r)
    strict = (c < r, c > r)
    zero_b = jnp.zeros((CHUNK, dv), BF16)

    def block_diag(y):
        return jnp.concatenate([y * keep_left, y * keep_right], axis=0)

    def reverse_super(i):
        return jnp.where(i < n_ctx_sup, n_ctx_sup - 1 - i, n_sup + n_ctx_sup - 1 - i)

    def make_units(i):
        sup = (i, reverse_super(i))
        units = []
        for m in range(n_m):
            for d in (0, 1):
                for seq in range(CHUNKS_PER_STEP):
                    cidx = seq if d == 0 else CHUNKS_PER_STEP - 1 - seq
                    units.append(dict(
                        m=m, d=d, seq=seq, half=cidx % 2, uid=(m * 2 + d) * CHUNKS_PER_STEP + seq,
                        pair=sup[d] * (SUPER // PAIR) + cidx // 2, chunk=sup[d] * CHUNKS_PER_STEP + cidx,
                        rows=pl.ds(pl.multiple_of(sup[d] * SUPER + cidx * CHUNK, CHUNK), CHUNK)))
        return units

    def load_gates(u):
        m, d, half = u["m"], u["d"], u["half"]
        cg = colg_ref[0, 0, u["rows"], :]
        b0, g0 = d * n_hv + 2 * m, (2 + d) * n_hv + 2 * m
        u["beta_c"] = (cg[:, b0:b0 + 1], cg[:, b0 + 1:b0 + 2])
        u["gc_c"] = (cg[:, g0:g0 + 1], cg[:, g0 + 1:g0 + 2])
        head0 = 2 * (n_m * g_idx + m)

        def packed_rows(base):
            rows = rowg_ref[0, u["pair"], pl.ds(base + head0, 2), :]
            a, b = rows[0:1], rows[1:2]
            if half == 0:
                return jnp.where(left_row, a, pltpu.roll(b, CHUNK, axis=1))
            return jnp.where(left_row, pltpu.roll(a, CHUNK, axis=1), b)

        u["beta_r"] = packed_rows(d * n_hv)
        u["gc_r"] = packed_rows((2 + d) * n_hv)

    def state_free_part(i, slot):
        units = make_units(i)
        for u in units:
            m = u["m"]
            load_gates(u)
            qb, kb = q_ref[0, u["rows"], m * dk:(m + 1) * dk], k_ref[0, u["rows"], m * dk:(m + 1) * dk]
            u["kb"] = kb
            u["gram"] = lax.dot_general(jnp.concatenate([kb, qb], axis=0), jnp.concatenate([kb, kb], axis=0),
                                        (((1,), (1,)), ((), ())), preferred_element_type=F32)
        yield
        for u in units:
            d = u["d"]
            gc_c2 = jnp.where(left, u["gc_c"][0], u["gc_c"][1])
            beta_c2 = jnp.where(left, u["beta_c"][0], u["beta_c"][1])
            decay = jnp.where(incl[d], jnp.exp(jnp.where(incl[d], gc_c2 - u["gc_r"], 0.0)), 0.0)
            a = jnp.where(strict[d], u["gram"][0:CHUNK] * decay, 0.0) * beta_c2
            qkd_s[slot, u["uid"]] = (u["gram"][CHUNK:2 * CHUNK] * decay).astype(BF16)
            u["a"] = a
            u["x"] = eye2 - jnp.where(lvl_mask[0], a, 0.0)
        yield
        for lg in range(1, n_lvl):
            for u in units:
                u["xb"] = u["x"].astype(BF16)
                off = jnp.where(lvl_mask[lg], u["a"], 0.0).astype(BF16)
                u["z"] = _dot(off, block_diag(u["xb"]))
            yield
            for u in units:
                u["x"] = u["x"] - _dot(u["xb"], block_diag(u["z"].astype(BF16)))
            yield
        for u in units:
            m = u["m"]
            tb = u["x"] * u["beta_r"]
            tbe = tb * jnp.exp(u["gc_r"])
            v2 = v_ref[0, u["rows"], 2 * m * dv:2 * (m + 1) * dv]
            v_bd = jnp.concatenate([jnp.concatenate([v2[:, :dv], zero_b], axis=1),
                                    jnp.concatenate([zero_b, v2[:, dv:]], axis=1)], axis=0)
            k_bd = jnp.concatenate([jnp.concatenate([u["kb"], zero_b], axis=1),
                                    jnp.concatenate([zero_b, u["kb"]], axis=1)], axis=0)
            uw_s[slot, u["uid"], :, :2 * dv] = _dot(tb.astype(BF16), v_bd)
            uw_s[slot, u["uid"], :, 2 * dv:] = _dot(tbe.astype(BF16), k_bd)
        yield

    def recurrence_part(i, slot):
        units = make_units(i)
        for seq in range(CHUNKS_PER_STEP):
            now = [u for u in units if u["seq"] == seq]
            for u in now:
                m = u["m"]
                load_gates(u)
                uw = uw_s[slot, u["uid"]]
                u["u"] = uw[:, :2 * dv]
                u["s2"] = st_s[2 * m + u["d"]]
                lhs = jnp.concatenate([uw[:, 2 * dv:3 * dv].astype(BF16), uw[:, 3 * dv:].astype(BF16),
                                       q_ref[0, u["rows"], m * dk:(m + 1) * dk]], axis=0)
                u["ws"] = _dot(lhs, u["s2"].astype(BF16))
            yield
            for u in now:
                d, ws = u["d"], u["ws"]
                vn0 = (u["u"][:, :dv] - ws[0:CHUNK, :dv]).astype(BF16)
                vn1 = (u["u"][:, dv:] - ws[CHUNK:2 * CHUNK, dv:]).astype(BF16)
                vn_bd = jnp.concatenate([jnp.concatenate([vn0, zero_b], axis=1),
                                         jnp.concatenate([zero_b, vn1], axis=1)], axis=0)
                last = (CHUNK - 1, 2 * CHUNK - 1) if d == 0 else (0, CHUNK)
                u["gl"] = [u["gc_r"][:, l:l + 1] for l in last]
                gl2 = jnp.where(left_row, u["gl"][0], u["gl"][1])
                kdt = kt_ref[0, u["m"], u["chunk"]].astype(F32) * jnp.exp(gl2 - u["gc_r"])
                lhs = jnp.concatenate([qkd_s[slot, u["uid"]], kdt.astype(BF16)], axis=0)
                u["ov"] = _dot(lhs, vn_bd)
            yield
            for u in now:
                m, ws, ov = u["m"], u["ws"], u["ov"]
                qs = jnp.concatenate([ws[2 * CHUNK:, :dv] * jnp.exp(u["gc_c"][0]),
                                      ws[2 * CHUNK:, dv:] * jnp.exp(u["gc_c"][1])], axis=1)
                o_s[u["d"], u["rows"], 2 * m * dv:2 * (m + 1) * dv] = qs + ov[0:CHUNK]
                egl = jnp.concatenate([jnp.broadcast_to(jnp.exp(u["gl"][0]), (1, dv)),
                                       jnp.broadcast_to(jnp.exp(u["gl"][1]), (1, dv))], axis=1)
                st_s[2 * m + u["d"]] = u["s2"] * egl + ov[CHUNK:]
            yield

    def run_interleaved(parts):
        live = list(parts)
        while live:
            live = [g for g in live if next(g, "done") != "done"]

    def step(i, carry):
        slot = i & 1
        run_interleaved([state_free_part(i, slot), recurrence_part(i - 1, 1 - slot)])
        return carry

    run_interleaved([state_free_part(jnp.int32(0), 0)])
    lax.fori_loop(1, n_sup, step, 0)
    run_interleaved([recurrence_part(jnp.int32(n_sup - 1), (n_sup - 1) & 1)])

    def finish(i, carry):
        rows = pl.ds(pl.multiple_of(i * STAGE_TILE, STAGE_TILE), STAGE_TILE)
        y_ref[0, rows, :] = (o_s[0, rows, :] + o_s[1, rows, :]).astype(y_ref.dtype)
        return carry

    lax.fori_loop(0, t_tot // STAGE_TILE, finish, 0)


def _gdn_delta(p_main, k_t, colg, rowg, dv, ctx_len, n_hk, n_hv):
    n_batch, t_tot, _ = p_main.shape
    dk = GDN_DK
    assert n_hv == 2 * n_hk and dk == dv == LANES and n_hk % HEADS_PER_STEP == 0
    assert ctx_len % SUPER == 0 and t_tot % SUPER == 0 and SUPER % STAGE_TILE == 0
    n_m = HEADS_PER_STEP
    qb = n_m * dk
    vb = 2 * n_m * dv
    k_blk0 = n_hk // n_m
    v_blk0 = (2 * n_hk * dk) // vb
    n_pairs = t_tot // PAIR
    n_units = 2 * n_m * CHUNKS_PER_STEP
    return pl.pallas_call(
        functools.partial(_delta_body, ctx_len=ctx_len, n_hv=n_hv, n_m=n_m),
        grid=(n_batch, n_hk // n_m),
        in_specs=[
            pl.BlockSpec((1, t_tot, qb), lambda b, h: (b, 0, h)),
            pl.BlockSpec((1, t_tot, qb), lambda b, h: (b, 0, k_blk0 + h)),
            pl.BlockSpec((1, t_tot, vb), lambda b, h: (b, 0, v_blk0 + h)),
            pl.BlockSpec((1, n_m, t_tot // CHUNK, dk, 2 * CHUNK), lambda b, h: (b, h, 0, 0, 0)),
            pl.BlockSpec((1, 1, t_tot, LANES), lambda b, h: (b, h, 0, 0)),
            pl.BlockSpec((1, n_pairs, LANES, PAIR), lambda b, h: (b, 0, 0, 0)),
        ],
        out_specs=pl.BlockSpec((1, t_tot, vb), lambda b, h: (b, 0, h)),
        out_shape=jax.ShapeDtypeStruct((n_batch, t_tot, n_hv * dv), BF16),
        scratch_shapes=[
            pltpu.VMEM((2, t_tot, vb), F32),
            pltpu.VMEM((2 * n_m, dk, 2 * dv), F32),
            pltpu.VMEM((2, n_units, CHUNK, 2 * dv + 2 * dk), F32),
            pltpu.VMEM((2, n_units, CHUNK, 2 * CHUNK), BF16),
        ],
        compiler_params=_params(("parallel", "parallel")),
        name="gdn_delta",
    )(p_main, p_main, p_main, k_t, colg, rowg)


def _rope(x, cos, sin_a, sin_b):
    quarter = ATT_DH // 4
    return x * cos + pltpu.roll(x, ATT_DH - quarter, axis=1) * sin_a + pltpu.roll(x, quarter, axis=1) * sin_b


def _rms_norm(x, g):
    return x * lax.rsqrt(jnp.mean(x * x, axis=-1, keepdims=True) + EPS) * g


def _attn_body(q_ref, k_ref, v_ref, cos_ref, sa_ref, sb_ref, qn_ref, kn_ref, o_ref, kt_s, v1_s,
               *, ctx_len, n_group, skip_tiles):
    qi = pl.program_id(2)
    tq = q_ref.shape[1]
    t_tot = k_ref.shape[1]
    dh = k_ref.shape[2]

    @pl.when(qi == 0)
    def _():
        k = _rms_norm(k_ref[0].astype(F32), kn_ref[...])
        k = _rope(k, cos_ref[...], sa_ref[...], sb_ref[...])
        kt_s[...] = k.T.astype(kt_s.dtype)
        v1_s[:, :dh] = v_ref[0]
        v1_s[:, dh:] = jnp.ones((t_tot, dh), v1_s.dtype)

    r0 = pl.multiple_of((qi + skip_tiles) * tq, tq)
    cos = cos_ref[pl.ds(r0, tq), :]
    sin_a = sa_ref[pl.ds(r0, tq), :]
    sin_b = sb_ref[pl.ds(r0, tq), :]

    def attend(n_keys):
        def scores(g):
            q = _rms_norm(q_ref[0, :, g * dh:(g + 1) * dh].astype(F32), qn_ref[...])
            q = _rope(q, cos, sin_a, sin_b) * (dh ** -0.5 * LOG2E)
            return _dot(q.astype(BF16), kt_s[:, :n_keys])

        s_next = scores(0)
        for g in range(n_group):
            s = s_next
            if g + 1 < n_group:
                s_next = scores(g + 1)
            p = jnp.exp2(s - jnp.max(s, axis=-1, keepdims=True))
            num_den = _dot(p.astype(BF16), v1_s[:n_keys, :])
            o_ref[0, :, g * dh:(g + 1) * dh] = (num_den[:, :dh] / num_den[:, dh:]).astype(o_ref.dtype)

    @pl.when(r0 < ctx_len)
    def _():
        attend(ctx_len)

    @pl.when(r0 >= ctx_len)
    def _():
        attend(t_tot)


def _attention(p_qkv, tables, q_norm, k_norm, ctx_len, n_hq, n_hkv, skip_tiles=0):
    n_batch, t_tot, _ = p_qkv.shape
    dh = ATT_DH
    n_group = n_hq // n_hkv
    tq = TOKEN_TILE
    cos, sin_a, sin_b = tables
    full = lambda shape: pl.BlockSpec(shape, lambda b, h, q: (0,) * len(shape))
    return pl.pallas_call(
        functools.partial(_attn_body, ctx_len=ctx_len, n_group=n_group, skip_tiles=skip_tiles),
        grid=(n_batch, n_hkv, t_tot // tq - skip_tiles),
        in_specs=[
            pl.BlockSpec((1, tq, n_group * dh), lambda b, h, q: (b, q + skip_tiles, h)),
            pl.BlockSpec((1, t_tot, dh), lambda b, h, q: (b, 0, n_hq + h)),
            pl.BlockSpec((1, t_tot, dh), lambda b, h, q: (b, 0, n_hq + n_hkv + h)),
            full((t_tot, dh)), full((t_tot, dh)), full((t_tot, dh)),
            full((1, dh)), full((1, dh)),
        ],
        out_specs=pl.BlockSpec((1, tq, n_group * dh), lambda b, h, q: (b, q, h)),
        out_shape=jax.ShapeDtypeStruct((n_batch, t_tot - skip_tiles * tq, n_hq * dh), BF16),
        scratch_shapes=[pltpu.VMEM((dh, t_tot), BF16), pltpu.VMEM((t_tot, 2 * dh), BF16)],
        compiler_params=_params(("parallel", "parallel", "arbitrary")),
        name="attention",
    )(p_qkv, p_qkv, p_qkv, cos, sin_a, sin_b, q_norm, k_norm)


def _rope_tables(ctx_len, seq_len):
    rows = seq_len // GRID_W
    row = jnp.repeat(jnp.arange(rows), GRID_W).astype(F32)
    col = jnp.tile(jnp.arange(GRID_W), rows).astype(F32)
    n_freq = ATT_DH // 4
    freqs = ROPE_THETA ** (-jnp.arange(n_freq, dtype=F32) / n_freq)
    ang_r = row[:, None] * freqs
    ang_c = col[:, None] * freqs
    ang = jnp.concatenate([ang_r, ang_r, ang_c, ang_c], axis=-1)
    cos = jnp.concatenate([jnp.ones((ctx_len, ATT_DH), F32), jnp.cos(ang)], axis=0)
    sin = jnp.concatenate([jnp.zeros((ctx_len, ATT_DH), F32), jnp.sin(ang)], axis=0)
    first_half = (jnp.arange(ATT_DH) % (2 * n_freq)) < n_freq
    return cos, jnp.where(first_half, -sin, 0.0), jnp.where(first_half, 0.0, sin)


def _mix_ffn_body(y_ref, *refs, n_chunk, alpha, gated):
    if gated:
        z_ref, ng_ref, wo_ref, x_ref, mod_ref, ln_ref, win_ref, wout_ref, o_ref = refs
        dv = ng_ref.shape[1]
        heads = []
        for hh in range(y_ref.shape[2] // dv):
            o = y_ref[0, :, hh * dv:(hh + 1) * dv].astype(F32)
            z = z_ref[0, :, hh * dv:(hh + 1) * dv].astype(F32)
            heads.append((_rms_norm(o, ng_ref[...]) * _silu(z)).astype(BF16))
        y = jnp.concatenate(heads, axis=1)
    else:
        wo_ref, x_ref, mod_ref, ln_ref, win_ref, wout_ref, o_ref = refs
        y = y_ref[0]
    delta = _dot(y, wo_ref[...])
    x1 = _layer_norm(alpha * x_ref[0] + mod_ref[0, 2:3, :] * delta, ln_ref[0:1, :], ln_ref[1:2, :])
    hb = (x1 * (1.0 + mod_ref[0, 4:5, :]) + mod_ref[0, 3:4, :]).astype(BF16)
    d_ff = wout_ref.shape[0]
    tc = d_ff // n_chunk
    acc = None
    for c in range(n_chunk):
        gate = _dot(hb, win_ref[:, c * tc:(c + 1) * tc])
        up = _dot(hb, win_ref[:, d_ff + c * tc:d_ff + (c + 1) * tc])
        part = _dot((_silu(gate) * up).astype(BF16), wout_ref[c * tc:(c + 1) * tc, :])
        acc = part if acc is None else acc + part
    o_ref[0] = _layer_norm(alpha * x1 + mod_ref[0, 5:6, :] * acc, ln_ref[2:3, :], ln_ref[3:4, :])


def _mix_ffn(y, w_o, xs, mods, ln, w_in, w_out, n_ctx_tiles, alpha, skip_tiles=0, gate=None):
    n_batch, t_tot, d = xs.shape
    k = y.shape[-1]
    d_ff = w_out.shape[0]
    tm = TOKEN_TILE
    n_chunk = 2 if d_ff % (2 * LANES) == 0 else 1
    row_map = _mod_row_map(n_batch, n_ctx_tiles)
    y_skip = skip_tiles - (t_tot - y.shape[1]) // tm
    resident = lambda shape: pl.BlockSpec(shape, lambda b, t: (0,) * len(shape), pipeline_mode=pl.Buffered(1))
    gate_specs, gate_args = [], []
    if gate is not None:
        z_src, z_blk, norm_g = gate
        gate_specs = [pl.BlockSpec((1, tm, k), lambda b, t: (b, t + skip_tiles, z_blk)), resident(norm_g.shape)]
        gate_args = [z_src, norm_g]
    return pl.pallas_call(
        functools.partial(_mix_ffn_body, n_chunk=n_chunk, alpha=alpha, gated=gate is not None),
        grid=(n_batch, t_tot // tm - skip_tiles),
        in_specs=[
            pl.BlockSpec((1, tm, k), lambda b, t: (b, t + y_skip, 0)),
            *gate_specs,
            resident((k, d)),
            pl.BlockSpec((1, tm, d), lambda b, t: (b, t + skip_tiles, 0)),
            pl.BlockSpec((1, N_MOD, d), lambda b, t: row_map(b, t + skip_tiles)),
            resident(ln.shape),
            resident((d, 2 * d_ff)),
            resident((d_ff, d)),
        ],
        out_specs=pl.BlockSpec((1, tm, d), lambda b, t: (b, t, 0)),
        out_shape=jax.ShapeDtypeStruct((n_batch, t_tot - skip_tiles * tm, d), F32),
        compiler_params=_params(("parallel", "parallel")),
        name="mix_ffn",
    )(y, *gate_args, w_o, xs, mods, ln, w_in, w_out)


def kernel(x, c, ctx, c_ctx, w_mod, b_mod, ln_g, ln_b, w_ffn_in, w_ffn_out, gdn_w_in, gdn_conv, gdn_a_log,
           gdn_dt_bias, gdn_norm_g, gdn_w_out, attn_w_qkv, attn_q_norm, attn_k_norm, attn_w_out):
    n_batch, seq_len, d = x.shape
    ctx_len = ctx.shape[1]
    depth = w_mod.shape[0]
    alpha = (2 * depth) ** 0.25
    n_hv = gdn_a_log.shape[-1]
    dv = gdn_norm_g.shape[-1]
    v_w = n_hv * dv
    qkv_w = gdn_conv.shape[-1]
    n_hk = (qkv_w - v_w) // (2 * GDN_DK)
    n_hq = attn_w_out.shape[1] // ATT_DH
    n_hkv = (attn_w_qkv.shape[-1] // ATT_DH - n_hq) // 2
    assert ctx_len % TOKEN_TILE == 0 and seq_len % TOKEN_TILE == 0 and seq_len % GRID_W == 0
    assert 4 * n_hv <= LANES and d % LANES == 0
    n_ctx_tiles = ctx_len // TOKEN_TILE

    xs = jnp.concatenate([ctx, x], axis=1)

    mp = -(-(n_batch + 1) // SUBLANES) * SUBLANES
    cond = jnp.concatenate([c, c_ctx[None, :], jnp.zeros((mp - n_batch - 1, d), F32)], axis=0)
    mods = _modulation(cond, w_mod, b_mod).reshape(depth, mp, N_MOD, d)
    tables = _rope_tables(ctx_len, seq_len)

    for i in range(depth):
        j = i // 2
        m = mods[i]
        skip = n_ctx_tiles if i == depth - 1 else 0
        if i % 2 == 0:
            w_in = gdn_w_in[j]
            gate_w = jnp.pad(w_in[:, qkv_w + v_w:], ((0, 0), (0, LANES - 4 * n_hv)))
            lanes_pad = (0, LANES - 4 * n_hv)
            a_log = jnp.pad(jnp.concatenate([jnp.zeros((2 * n_hv,), F32), gdn_a_log[j].reshape(-1)]), lanes_pad)
            dt_b = jnp.pad(jnp.concatenate([jnp.zeros((2 * n_hv,), F32), gdn_dt_bias[j].reshape(-1)]), lanes_pad)
            p_main, k_t, colg, rowg = _gdn_in_proj(xs, m, w_in[:, :qkv_w + v_w].astype(BF16), gate_w, gdn_conv[j],
                                                   a_log[None, :], dt_b[None, :], n_ctx_tiles,
                                                   qk_w=n_hk * GDN_DK, n_hv=n_hv, n_chunk=6)
            y = _gdn_delta(p_main, k_t, colg, rowg, dv, ctx_len, n_hk, n_hv)
            w_o = gdn_w_out[j]
            assert qkv_w % v_w == 0
            gate = (p_main, qkv_w // v_w, gdn_norm_g[j][None, :])
        else:
            p_qkv = _in_proj(xs, m, attn_w_qkv[j].astype(BF16), n_ctx_tiles, n_chunk=3)
            y = _attention(p_qkv, tables, attn_q_norm[j][None, :], attn_k_norm[j][None, :], ctx_len, n_hq, n_hkv,
                           skip_tiles=skip)
            w_o = attn_w_out[j]
            gate = None
        ln = jnp.stack([ln_g[i, 0], ln_b[i, 0], ln_g[i, 1], ln_b[i, 1]])
        xs = _mix_ffn(y, w_o.astype(BF16), xs, m, ln, w_ffn_in[i].astype(BF16), w_ffn_out[i].astype(BF16),
                      n_ctx_tiles, alpha, skip_tiles=skip, gate=gate)
    return xs
```

```python
import functools
import math

import jax
import jax.numpy as jnp
from jax import lax
from jax.experimental import pallas as pl
from jax.experimental.pallas import tpu as pltpu

F32 = jnp.float32
BF16 = jnp.bfloat16

N_MOD = 6
EPS = 1e-6
GDN_DK = 128
CHUNK = 64
ATT_DH = 128
GRID_W = 64
ROPE_THETA = 10000.0

LANES = 128
SUBLANES = 8
PAIR = 2 * CHUNK
VMEM_LIMIT_BYTES = 56 * 1024 * 1024
TOKEN_TILE = 256
LOG2E = math.log2(math.e)

assert PAIR == LANES


def _params(sem):
    return pltpu.CompilerParams(dimension_semantics=sem, vmem_limit_bytes=VMEM_LIMIT_BYTES)


def _dot(a, b):
    return jnp.dot(a, b, preferred_element_type=F32)


def _split2(a):
    hi = a.astype(BF16)
    return hi, (a - hi.astype(F32)).astype(BF16)


def _dot_hi(a, b):
    a1, a2 = _split2(a)
    b1, b2 = _split2(b)
    return _dot(a1, b1) + (_dot(a1, b2) + _dot(a2, b1))


def _sigmoid(x):
    return 1.0 / (1.0 + jnp.exp(-x))


def _silu(x):
    return x * _sigmoid(x)


def _layer_norm(r, g, b):
    mu = jnp.mean(r, axis=-1, keepdims=True)
    rc = r - mu
    var = jnp.mean(rc * rc, axis=-1, keepdims=True)
    return rc * lax.rsqrt(var + EPS) * g + b


def _mod_body(cond_ref, w_ref, b_ref, o_ref):
    o_ref[0] = _dot_hi(_silu(cond_ref[...]), w_ref[0]) + b_ref[0]


def _modulation(cond, w_mod, b_mod):
    n_layer, d, n = w_mod.shape
    mp = cond.shape[0]
    tn = n // 4
    return pl.pallas_call(
        _mod_body,
        grid=(n_layer, n // tn),
        in_specs=[
            pl.BlockSpec((mp, d), lambda l, j: (0, 0)),
            pl.BlockSpec((1, d, tn), lambda l, j: (l, 0, j)),
            pl.BlockSpec((1, 1, tn), lambda l, j: (l, 0, j)),
        ],
        out_specs=pl.BlockSpec((1, mp, tn), lambda l, j: (l, 0, j)),
        out_shape=jax.ShapeDtypeStruct((n_layer, mp, n), F32),
        compiler_params=_params(("parallel", "parallel")),
        name="modulation",
    )(cond, w_mod, b_mod.reshape(n_layer, 1, n))


def _mod_row_map(n_batch, n_ctx_tiles):
    return lambda b, t: (jnp.where(t < n_ctx_tiles, n_batch, b), 0, 0)


def _rope(x, cos, sin_a, sin_b):
    quarter = ATT_DH // 4
    return x * cos + pltpu.roll(x, ATT_DH - quarter, axis=1) * sin_a + pltpu.roll(x, quarter, axis=1) * sin_b


def _rms_norm(x, g):
    return x * lax.rsqrt(jnp.mean(x * x, axis=-1, keepdims=True) + EPS) * g


def _attn_proj_body(x_ref, mod_ref, w_ref, cos_ref, sa_ref, sb_ref, qn_ref, kn_ref, q_ref, kt_ref, v1_ref,
                    *, n_hq, n_hkv):
    dh = ATT_DH
    hb = (x_ref[0] * (1.0 + mod_ref[0, 1:2, :]) + mod_ref[0, 0:1, :]).astype(BF16)
    cos, sin_a, sin_b = cos_ref[...], sa_ref[...], sb_ref[...]
    q_all = _dot(hb, w_ref[:, :n_hq * dh])
    for h in range(n_hq):
        q = _rope(_rms_norm(q_all[:, h * dh:(h + 1) * dh], qn_ref[...]), cos, sin_a, sin_b)
        q_ref[0, :, h * dh:(h + 1) * dh] = (q * (dh ** -0.5 * LOG2E)).astype(q_ref.dtype)
    kv = _dot(hb, w_ref[:, n_hq * dh:])
    for h in range(n_hkv):
        k = _rope(_rms_norm(kv[:, h * dh:(h + 1) * dh], kn_ref[...]), cos, sin_a, sin_b)
        kt_ref[0, h] = k.T.astype(kt_ref.dtype)
        v1_ref[0, h, :, :dh] = kv[:, (n_hkv + h) * dh:(n_hkv + h + 1) * dh].astype(v1_ref.dtype)
        v1_ref[0, h, :, dh:] = jnp.ones((kv.shape[0], dh), v1_ref.dtype)


def _attn_in_proj(xs, mods, w, tables, q_norm, k_norm, n_ctx_tiles, n_hq, n_hkv):
    n_batch, t_tot, d = xs.shape
    dh = ATT_DH
    tm = TOKEN_TILE
    cos, sin_a, sin_b = tables
    const = lambda shape: pl.BlockSpec(shape, lambda b, t: (0,) * len(shape))
    table = pl.BlockSpec((tm, dh), lambda b, t: (t, 0))
    return pl.pallas_call(
        functools.partial(_attn_proj_body, n_hq=n_hq, n_hkv=n_hkv),
        grid=(n_batch, t_tot // tm),
        in_specs=[
            pl.BlockSpec((1, tm, d), lambda b, t: (b, t, 0)),
            pl.BlockSpec((1, N_MOD, d), _mod_row_map(n_batch, n_ctx_tiles)),
            const(w.shape), table, table, table, const((1, dh)), const((1, dh)),
        ],
        out_specs=[pl.BlockSpec((1, tm, n_hq * dh), lambda b, t: (b, t, 0)),
                   pl.BlockSpec((1, n_hkv, dh, tm), lambda b, t: (b, 0, 0, t)),
                   pl.BlockSpec((1, n_hkv, tm, 2 * dh), lambda b, t: (b, 0, t, 0))],
        out_shape=[jax.ShapeDtypeStruct((n_batch, t_tot, n_hq * dh), BF16),
                   jax.ShapeDtypeStruct((n_batch, n_hkv, dh, t_tot), BF16),
                   jax.ShapeDtypeStruct((n_batch, n_hkv, t_tot, 2 * dh), BF16)],
        compiler_params=_params(("parallel", "parallel")),
        name="attn_in_proj",
    )(xs, mods, w, cos, sin_a, sin_b, q_norm, k_norm)


CONV_HALO = SUBLANES


def _l2_norm(x):
    return x * lax.rsqrt(jnp.sum(x * x, axis=-1, keepdims=True) + EPS)


def _gdn_proj_body(x_ref, xp_ref, xn_ref, mod_ref, w_ref, wxh_ref, wxl_ref, conv_ref, alog_ref, dtb_ref,
                   o_ref, kt_ref, col_ref, row_ref, *, ctx_len, qk_w, qkv_w, n_chunk, dk, n_hv):
    tm = x_ref.shape[1]
    r0 = pl.program_id(1) * tm
    t_tot = pl.num_programs(1) * tm
    scale = 1.0 + mod_ref[0, 1:2, :]
    shift = mod_ref[0, 0:1, :]
    h = x_ref[0] * scale + shift
    hb = h.astype(BF16)
    hb_ext = jnp.concatenate([xp_ref[0] * scale + shift, h, xn_ref[0] * scale + shift], axis=0).astype(BF16)
    keep_lo = jnp.where(jnp.logical_or(r0 == 0, r0 == ctx_len), 0.0, 1.0)
    keep_hi = jnp.where(jnp.logical_or(r0 + tm == ctx_len, r0 + tm == t_tot), 0.0, 1.0)
    conv_w = conv_ref.shape[0]
    pad = conv_w // 2
    n_ext = tm + 2 * CONV_HALO
    mid = slice(CONV_HALO, CONV_HALO + tm)
    tn = w_ref.shape[1] // n_chunk
    for j in range(n_chunk):
        c0 = j * tn
        if c0 >= qkv_w:
            o_ref[0, :, c0:c0 + tn] = _dot(hb, w_ref[:, c0:c0 + tn]).astype(o_ref.dtype)
            continue
        p = _dot(hb_ext, w_ref[:, c0:c0 + tn])
        ext = jnp.concatenate([p[:CONV_HALO] * keep_lo, p[mid], p[CONV_HALO + tm:] * keep_hi], axis=0)
        y = None
        for tap in range(conv_w):
            off = tap - pad
            xs = ext if off == 0 else pltpu.roll(ext, (-off) % n_ext, axis=0)
            term = xs[mid] * conv_ref[tap:tap + 1, c0:c0 + tn]
            y = term if y is None else y + term
        y = _silu(y)
        if c0 >= 2 * qk_w:
            o_ref[0, :, c0:c0 + tn] = y.astype(o_ref.dtype)
            continue
        for hh in range(tn // dk):
            yh = _l2_norm(y[:, hh * dk:(hh + 1) * dk])
            if c0 < qk_w:
                yh = yh * (dk ** -0.5)
            o_ref[0, :, c0 + hh * dk:c0 + (hh + 1) * dk] = yh.astype(o_ref.dtype)
            if c0 >= qk_w:
                for cc in range(tm // CHUNK):
                    kc = yh[cc * CHUNK:(cc + 1) * CHUNK]
                    kt_ref[0, (c0 - qk_w) // dk + hh, cc] = jnp.concatenate([kc, kc], axis=0).T.astype(kt_ref.dtype)
    h_lo = (h - hb.astype(F32)).astype(BF16)
    raw = _dot(hb, wxh_ref[...]) + (_dot(hb, wxl_ref[...]) + _dot(h_lo, wxh_ref[...]))
    _gate_forms(raw, alog_ref[...], dtb_ref[...], col_ref, row_ref, n_hv)


def _gate_forms(raw, a_log, dt_bias, col_ref, row_ref, n_hv):
    beta = _sigmoid(raw)
    xs = raw + dt_bias
    g = -jnp.exp(a_log) * (jnp.maximum(xs, 0.0) + jnp.log1p(jnp.exp(-jnp.abs(xs))))
    r = lax.broadcasted_iota(jnp.int32, (PAIR, PAIR), 0)
    c = lax.broadcasted_iota(jnp.int32, (PAIR, PAIR), 1)
    same = (r < CHUNK) == (c < CHUNK)
    tri_f = jnp.where(jnp.logical_and(same, c <= r), 1.0, 0.0).astype(BF16)
    tri_r = jnp.where(jnp.logical_and(same, c >= r), 1.0, 0.0).astype(BF16)
    lane = lax.broadcasted_iota(jnp.int32, (PAIR, LANES), 1)
    for p in range(raw.shape[0] // PAIR):
        gp = g[p * PAIR:(p + 1) * PAIR]
        g1 = gp.astype(BF16)
        r1 = gp - g1.astype(F32)
        g2 = r1.astype(BF16)
        g3 = (r1 - g2.astype(F32)).astype(BF16)
        cum_f = _dot(tri_f, g1) + (_dot(tri_f, g2) + _dot(tri_f, g3))
        cum_r = _dot(tri_r, g1) + (_dot(tri_r, g2) + _dot(tri_r, g3))
        col = jnp.where(lane < 2 * n_hv, beta[p * PAIR:(p + 1) * PAIR],
                        jnp.where(lane < 3 * n_hv, cum_f, cum_r))
        for grp in range(col_ref.shape[1]):
            shift = (LANES - 2 * HEADS_PER_STEP * grp) % LANES
            col_ref[0, grp, p * PAIR:(p + 1) * PAIR, :] = col if shift == 0 else pltpu.roll(col, shift, axis=1)
        row_ref[0, p] = col.T


def _gdn_in_proj(xs, mods, w, w_gate, conv, a_log_lanes, dt_bias_lanes, n_ctx_tiles, qk_w, n_hv, n_chunk):
    n_batch, t_tot, d = xs.shape
    n = w.shape[1]
    conv_w, qkv_w = conv.shape
    tm = TOKEN_TILE
    tn = n // n_chunk
    assert qk_w % tn == 0 and qkv_w % tn == 0 and tn % GDN_DK == 0 and tm % PAIR == 0
    halo_per_tile = tm // CONV_HALO
    last_halo = t_tot // CONV_HALO - 1
    w_gate_hi, w_gate_lo = _split2(w_gate)
    n_hk = qk_w // GDN_DK
    const = lambda shape: pl.BlockSpec(shape, lambda b, t: (0,) * len(shape))
    return pl.pallas_call(
        functools.partial(_gdn_proj_body, ctx_len=n_ctx_tiles * tm, qk_w=qk_w, qkv_w=qkv_w, n_chunk=n_chunk,
                          dk=GDN_DK, n_hv=n_hv),
        grid=(n_batch, t_tot // tm),
        in_specs=[
            pl.BlockSpec((1, tm, d), lambda b, t: (b, t, 0)),
            pl.BlockSpec((1, CONV_HALO, d), lambda b, t: (b, jnp.maximum(t * halo_per_tile - 1, 0), 0)),
            pl.BlockSpec((1, CONV_HALO, d), lambda b, t: (b, jnp.minimum((t + 1) * halo_per_tile, last_halo), 0)),
            pl.BlockSpec((1, N_MOD, d), _mod_row_map(n_batch, n_ctx_tiles)),
            const((d, n)), const((d, LANES)), const((d, LANES)), const((conv_w, qkv_w)),
            const((1, LANES)), const((1, LANES)),
        ],
        out_specs=[pl.BlockSpec((1, tm, n), lambda b, t: (b, t, 0)),
                   pl.BlockSpec((1, n_hk, tm // CHUNK, GDN_DK, 2 * CHUNK), lambda b, t: (b, 0, t, 0, 0)),
                   pl.BlockSpec((1, n_hk // HEADS_PER_STEP, tm, LANES), lambda b, t: (b, 0, t, 0)),
                   pl.BlockSpec((1, tm // PAIR, LANES, PAIR), lambda b, t: (b, t, 0, 0))],
        out_shape=[jax.ShapeDtypeStruct((n_batch, t_tot, n), BF16),
                   jax.ShapeDtypeStruct((n_batch, n_hk, t_tot // CHUNK, GDN_DK, 2 * CHUNK), BF16),
                   jax.ShapeDtypeStruct((n_batch, n_hk // HEADS_PER_STEP, t_tot, LANES), F32),
                   jax.ShapeDtypeStruct((n_batch, t_tot // PAIR, LANES, PAIR), F32)],
        compiler_params=_params(("parallel", "parallel")),
        name="gdn_in_proj",
    )(xs, xs, xs, mods, w, w_gate_hi, w_gate_lo, conv, a_log_lanes, dt_bias_lanes)


STAGE_TILE = 256
HEADS_PER_STEP = 2
CHUNKS_PER_STEP = 4
SUPER = CHUNKS_PER_STEP * CHUNK


def _delta_body(q_ref, k_ref, v_ref, kt_ref, colg_ref, rowg_ref,
                y_ref, o_s, st_s, uw_s, qkd_s, *, ctx_len, n_hv, n_m):
    g_idx = pl.program_id(1)
    t_tot = q_ref.shape[1]
    dk = q_ref.shape[2] // n_m
    dv = v_ref.shape[2] // (2 * n_m)
    n_sup = t_tot // SUPER
    n_ctx_sup = ctx_len // SUPER

    st_s[...] = jnp.zeros_like(st_s)

    r = lax.broadcasted_iota(jnp.int32, (CHUNK, LANES), 0)
    lane = lax.broadcasted_iota(jnp.int32, (CHUNK, LANES), 1)
    c = lane & (CHUNK - 1)
    left = lane < CHUNK
    left_row = lax.broadcasted_iota(jnp.int32, (1, LANES), 1) < CHUNK
    keep_left = jnp.where(left, 1.0, 0.0).astype(BF16)
    keep_right = jnp.where(left, 0.0, 1.0).astype(BF16)
    eye2 = jnp.where(c == r, 1.0, 0.0)
    n_lvl = CHUNK.bit_length() - 1
    lvl_mask = [jnp.logical_and((r >> (lg + 1)) == (c >> (lg + 1)), (r >> lg) != (c >> lg)) for lg in range(n_lvl)]
    incl = (c <= r, c >= r)
    strict = (c < r, c > r)
    zero_b = jnp.zeros((CHUNK, dv), BF16)

    def block_diag(y):
        return jnp.concatenate([y * keep_left, y * keep_right], axis=0)

    def reverse_super(i):
        return jnp.where(i < n_ctx_sup, n_ctx_sup - 1 - i, n_sup + n_ctx_sup - 1 - i)

    def make_units(i):
        sup = (i, reverse_super(i))
        units = []
        for m in range(n_m):
            for d in (0, 1):
                for seq in range(CHUNKS_PER_STEP):
                    cidx = seq if d == 0 else CHUNKS_PER_STEP - 1 - seq
                    units.append(dict(
                        m=m, d=d, seq=seq, half=cidx % 2, uid=(m * 2 + d) * CHUNKS_PER_STEP + seq,
                        pair=sup[d] * (SUPER // PAIR) + cidx // 2, chunk=sup[d] * CHUNKS_PER_STEP + cidx,
                        rows=pl.ds(pl.multiple_of(sup[d] * SUPER + cidx * CHUNK, CHUNK), CHUNK)))
        return units

    def load_gates(u):
        m, d, half = u["m"], u["d"], u["half"]
        cg = colg_ref[0, 0, u["rows"], :]
        b0, g0 = d * n_hv + 2 * m, (2 + d) * n_hv + 2 * m
        u["beta_c"] = (cg[:, b0:b0 + 1], cg[:, b0 + 1:b0 + 2])
        u["gc_c"] = (cg[:, g0:g0 + 1], cg[:, g0 + 1:g0 + 2])
        head0 = 2 * (n_m * g_idx + m)

        def packed_rows(base):
            rows = rowg_ref[0, u["pair"], pl.ds(base + head0, 2), :]
            a, b = rows[0:1], rows[1:2]
            if half == 0:
                return jnp.where(left_row, a, pltpu.roll(b, CHUNK, axis=1))
            return jnp.where(left_row, pltpu.roll(a, CHUNK, axis=1), b)

        u["beta_r"] = packed_rows(d * n_hv)
        u["gc_r"] = packed_rows((2 + d) * n_hv)

    def state_free_part(i, slot):
        units = make_units(i)
        for u in units:
            m = u["m"]
            load_gates(u)
            qb, kb = q_ref[0, u["rows"], m * dk:(m + 1) * dk], k_ref[0, u["rows"], m * dk:(m + 1) * dk]
            u["kb"] = kb
            u["gram"] = lax.dot_general(jnp.concatenate([kb, qb], axis=0), jnp.concatenate([kb, kb], axis=0),
                                        (((1,), (1,)), ((), ())), preferred_element_type=F32)
        yield
        for u in units:
            d = u["d"]
            gc_c2 = jnp.where(left, u["gc_c"][0], u["gc_c"][1])
            beta_c2 = jnp.where(left, u["beta_c"][0], u["beta_c"][1])
            decay = jnp.where(incl[d], jnp.exp(jnp.where(incl[d], gc_c2 - u["gc_r"], 0.0)), 0.0)
            a = jnp.where(strict[d], u["gram"][0:CHUNK] * decay, 0.0) * beta_c2
            qkd_s[slot, u["uid"]] = (u["gram"][CHUNK:2 * CHUNK] * decay).astype(BF16)
            u["a"] = a
            u["x"] = eye2 - jnp.where(lvl_mask[0], a, 0.0)
        yield
        for lg in range(1, n_lvl):
            for u in units:
                u["xb"] = u["x"].astype(BF16)
                off = jnp.where(lvl_mask[lg], u["a"], 0.0).astype(BF16)
                u["z"] = _dot(off, block_diag(u["xb"]))
            yield
            for u in units:
                u["x"] = u["x"] - _dot(u["xb"], block_diag(u["z"].astype(BF16)))
            yield
        for u in units:
            m = u["m"]
            tb = u["x"] * u["beta_r"]
            tbe = tb * jnp.exp(u["gc_r"])
            v2 = v_ref[0, u["rows"], 2 * m * dv:2 * (m + 1) * dv]
            v_bd = jnp.concatenate([jnp.concatenate([v2[:, :dv], zero_b], axis=1),
                                    jnp.concatenate([zero_b, v2[:, dv:]], axis=1)], axis=0)
            k_bd = jnp.concatenate([jnp.concatenate([u["kb"], zero_b], axis=1),
                                    jnp.concatenate([zero_b, u["kb"]], axis=1)], axis=0)
            uw_s[slot, u["uid"], :, :2 * dv] = _dot(tb.astype(BF16), v_bd)
            uw_s[slot, u["uid"], :, 2 * dv:] = _dot(tbe.astype(BF16), k_bd)
        yield

    def recurrence_part(i, slot):
        units = make_units(i)
        for seq in range(CHUNKS_PER_STEP):
            now = [u for u in units if u["seq"] == seq]
            for u in now:
                m = u["m"]
                load_gates(u)
                uw = uw_s[slot, u["uid"]]
                u["u"] = uw[:, :2 * dv]
                u["s2"] = st_s[2 * m + u["d"]]
                lhs = jnp.concatenate([uw[:, 2 * dv:3 * dv].astype(BF16), uw[:, 3 * dv:].astype(BF16),
                                       q_ref[0, u["rows"], m * dk:(m + 1) * dk]], axis=0)
                u["ws"] = _dot(lhs, u["s2"].astype(BF16))
            yield
            for u in now:
                d, ws = u["d"], u["ws"]
                vn0 = (u["u"][:, :dv] - ws[0:CHUNK, :dv]).astype(BF16)
                vn1 = (u["u"][:, dv:] - ws[CHUNK:2 * CHUNK, dv:]).astype(BF16)
                vn_bd = jnp.concatenate([jnp.concatenate([vn0, zero_b], axis=1),
                                         jnp.concatenate([zero_b, vn1], axis=1)], axis=0)
                last = (CHUNK - 1, 2 * CHUNK - 1) if d == 0 else (0, CHUNK)
                u["gl"] = [u["gc_r"][:, l:l + 1] for l in last]
                gl2 = jnp.where(left_row, u["gl"][0], u["gl"][1])
                kdt = kt_ref[0, u["m"], u["chunk"]].astype(F32) * jnp.exp(gl2 - u["gc_r"])
                lhs = jnp.concatenate([qkd_s[slot, u["uid"]], kdt.astype(BF16)], axis=0)
                u["ov"] = _dot(lhs, vn_bd)
            yield
            for u in now:
                m, ws, ov = u["m"], u["ws"], u["ov"]
                qs = jnp.concatenate([ws[2 * CHUNK:, :dv] * jnp.exp(u["gc_c"][0]),
                                      ws[2 * CHUNK:, dv:] * jnp.exp(u["gc_c"][1])], axis=1)
                o_s[u["d"], u["rows"], 2 * m * dv:2 * (m + 1) * dv] = qs + ov[0:CHUNK]
                egl = jnp.concatenate([jnp.broadcast_to(jnp.exp(u["gl"][0]), (1, dv)),
                                       jnp.broadcast_to(jnp.exp(u["gl"][1]), (1, dv))], axis=1)
                st_s[2 * m + u["d"]] = u["s2"] * egl + ov[CHUNK:]
            yield

    def run_interleaved(parts):
        live = list(parts)
        while live:
            live = [g for g in live if next(g, "done") != "done"]

    def step(i, carry):
        slot = i & 1
        run_interleaved([state_free_part(i, slot), recurrence_part(i - 1, 1 - slot)])
        return carry

    run_interleaved([state_free_part(jnp.int32(0), 0)])
    lax.fori_loop(1, n_sup, step, 0)
    run_interleaved([recurrence_part(jnp.int32(n_sup - 1), (n_sup - 1) & 1)])

    def finish(i, carry):
        rows = pl.ds(pl.multiple_of(i * STAGE_TILE, STAGE_TILE), STAGE_TILE)
        y_ref[0, rows, :] = (o_s[0, rows, :] + o_s[1, rows, :]).astype(y_ref.dtype)
        return carry

    lax.fori_loop(0, t_tot // STAGE_TILE, finish, 0)


def _gdn_delta(p_main, k_t, colg, rowg, dv, ctx_len, n_hk, n_hv):
    n_batch, t_tot, _ = p_main.shape
    dk = GDN_DK
    assert n_hv == 2 * n_hk and dk == dv == LANES and n_hk % HEADS_PER_STEP == 0
    assert ctx_len % SUPER == 0 and t_tot % SUPER == 0 and SUPER % STAGE_TILE == 0
    n_m = HEADS_PER_STEP
    qb = n_m * dk
    vb = 2 * n_m * dv
    k_blk0 = n_hk // n_m
    v_blk0 = (2 * n_hk * dk) // vb
    n_pairs = t_tot // PAIR
    n_units = 2 * n_m * CHUNKS_PER_STEP
    return pl.pallas_call(
        functools.partial(_delta_body, ctx_len=ctx_len, n_hv=n_hv, n_m=n_m),
        grid=(n_batch, n_hk // n_m),
        in_specs=[
            pl.BlockSpec((1, t_tot, qb), lambda b, h: (b, 0, h)),
            pl.BlockSpec((1, t_tot, qb), lambda b, h: (b, 0, k_blk0 + h)),
            pl.BlockSpec((1, t_tot, vb), lambda b, h: (b, 0, v_blk0 + h)),
            pl.BlockSpec((1, n_m, t_tot // CHUNK, dk, 2 * CHUNK), lambda b, h: (b, h, 0, 0, 0)),
            pl.BlockSpec((1, 1, t_tot, LANES), lambda b, h: (b, h, 0, 0)),
            pl.BlockSpec((1, n_pairs, LANES, PAIR), lambda b, h: (b, 0, 0, 0)),
        ],
        out_specs=pl.BlockSpec((1, t_tot, vb), lambda b, h: (b, 0, h)),
        out_shape=jax.ShapeDtypeStruct((n_batch, t_tot, n_hv * dv), BF16),
        scratch_shapes=[
            pltpu.VMEM((2, t_tot, vb), F32),
            pltpu.VMEM((2 * n_m, dk, 2 * dv), F32),
            pltpu.VMEM((2, n_units, CHUNK, 2 * dv + 2 * dk), F32),
            pltpu.VMEM((2, n_units, CHUNK, 2 * CHUNK), BF16),
        ],
        compiler_params=_params(("parallel", "parallel")),
        name="gdn_delta",
    )(p_main, p_main, p_main, k_t, colg, rowg)


def _attn_body(q_ref, kt_ref, v1_ref, o_ref, *, ctx_len, n_group, skip_tiles):
    tq = q_ref.shape[1]
    dh = kt_ref.shape[2]
    t_tot = kt_ref.shape[3]
    r0 = (pl.program_id(2) + skip_tiles) * tq

    def attend(n_keys):
        def scores(g):
            return _dot(q_ref[0, :, g * dh:(g + 1) * dh], kt_ref[0, 0, :, :n_keys])

        s_next = scores(0)
        for g in range(n_group):
            s = s_next
            if g + 1 < n_group:
                s_next = scores(g + 1)
            p = jnp.exp2(s - jnp.max(s, axis=-1, keepdims=True))
            num_den = _dot(p.astype(BF16), v1_ref[0, 0, :n_keys, :])
            o_ref[0, :, g * dh:(g + 1) * dh] = (num_den[:, :dh] / num_den[:, dh:]).astype(o_ref.dtype)

    @pl.when(r0 < ctx_len)
    def _():
        attend(ctx_len)

    @pl.when(r0 >= ctx_len)
    def _():
        attend(t_tot)


def _attention(q, k_t, v1, ctx_len, skip_tiles=0):
    n_batch, t_tot, q_w = q.shape
    _, n_hkv, dh, _ = k_t.shape
    n_group = q_w // (n_hkv * dh)
    tq = TOKEN_TILE
    return pl.pallas_call(
        functools.partial(_attn_body, ctx_len=ctx_len, n_group=n_group, skip_tiles=skip_tiles),
        grid=(n_batch, n_hkv, t_tot // tq - skip_tiles),
        in_specs=[
            pl.BlockSpec((1, tq, n_group * dh), lambda b, h, i: (b, i + skip_tiles, h)),
            pl.BlockSpec((1, 1, dh, t_tot), lambda b, h, i: (b, h, 0, 0)),
            pl.BlockSpec((1, 1, t_tot, 2 * dh), lambda b, h, i: (b, h, 0, 0)),
        ],
        out_specs=pl.BlockSpec((1, tq, n_group * dh), lambda b, h, i: (b, i, h)),
        out_shape=jax.ShapeDtypeStruct((n_batch, t_tot - skip_tiles * tq, q_w), BF16),
        compiler_params=_params(("parallel", "parallel", "parallel")),
        name="attention",
    )(q, k_t, v1)


def _rope_tables(ctx_len, seq_len):
    rows = seq_len // GRID_W
    row = jnp.repeat(jnp.arange(rows), GRID_W).astype(F32)
    col = jnp.tile(jnp.arange(GRID_W), rows).astype(F32)
    n_freq = ATT_DH // 4
    freqs = ROPE_THETA ** (-jnp.arange(n_freq, dtype=F32) / n_freq)
    ang_r = row[:, None] * freqs
    ang_c = col[:, None] * freqs
    ang = jnp.concatenate([ang_r, ang_r, ang_c, ang_c], axis=-1)
    cos = jnp.concatenate([jnp.ones((ctx_len, ATT_DH), F32), jnp.cos(ang)], axis=0)
    sin = jnp.concatenate([jnp.zeros((ctx_len, ATT_DH), F32), jnp.sin(ang)], axis=0)
    first_half = (jnp.arange(ATT_DH) % (2 * n_freq)) < n_freq
    return cos, jnp.where(first_half, -sin, 0.0), jnp.where(first_half, 0.0, sin)


ROW_SPLIT = 2

def _mix_ffn_body(y_ref, *refs, n_chunk, alpha, gated):
    if gated:
        z_ref, ng_ref, wo_ref, x_ref, mod_ref, ln_ref, win_ref, wout_ref, o_ref = refs
        dv = ng_ref.shape[1]
        kw = 2 * dv
        delta = None
        for pp in range(y_ref.shape[2] // kw):
            pair = []
            for hh in (2 * pp, 2 * pp + 1):
                o = y_ref[0, :, hh * dv:(hh + 1) * dv].astype(F32)
                z = z_ref[0, :, hh * dv:(hh + 1) * dv].astype(F32)
                pair.append((_rms_norm(o, ng_ref[...]) * _silu(z)).astype(BF16))
            part = _dot(jnp.concatenate(pair, axis=1), wo_ref[pp * kw:(pp + 1) * kw, :])
            delta = part if delta is None else delta + part
    else:
        wo_ref, x_ref, mod_ref, ln_ref, win_ref, wout_ref, o_ref = refs
        delta = None
    rows = x_ref.shape[1] // ROW_SPLIT
    parts = [slice(i * rows, (i + 1) * rows) for i in range(ROW_SPLIT)]
    deltas = [_dot(y_ref[0, r, :], wo_ref[...]) if delta is None else delta[r] for r in parts]
    x_in = x_ref[0]
    x1s = [_layer_norm(alpha * x_in[r] + mod_ref[0, 2:3, :] * dl, ln_ref[0:1, :], ln_ref[1:2, :])
           for r, dl in zip(parts, deltas)]
    hbs = [(x1 * (1.0 + mod_ref[0, 4:5, :]) + mod_ref[0, 3:4, :]).astype(BF16) for x1 in x1s]
    d_ff = wout_ref.shape[0]
    tc = d_ff // n_chunk
    accs = [None] * ROW_SPLIT
    for c in range(n_chunk):
        for i in range(ROW_SPLIT):
            gate = _dot(hbs[i], win_ref[:, c * tc:(c + 1) * tc])
            up = _dot(hbs[i], win_ref[:, d_ff + c * tc:d_ff + (c + 1) * tc])
            part = _dot((_silu(gate) * up).astype(BF16), wout_ref[c * tc:(c + 1) * tc, :])
            accs[i] = part if accs[i] is None else accs[i] + part
    for i, r in enumerate(parts):
        o_ref[0, r, :] = _layer_norm(alpha * x1s[i] + mod_ref[0, 5:6, :] * accs[i], ln_ref[2:3, :], ln_ref[3:4, :])


def _mix_ffn(y, w_o, xs, mods, ln, w_in, w_out, n_ctx_tiles, alpha, skip_tiles=0, gate=None):
    n_batch, t_tot, d = xs.shape
    k = y.shape[-1]
    d_ff = w_out.shape[0]
    tm = TOKEN_TILE
    n_chunk = 2 if d_ff % (2 * LANES) == 0 else 1
    row_map = _mod_row_map(n_batch, n_ctx_tiles)
    y_skip = skip_tiles - (t_tot - y.shape[1]) // tm
    resident = lambda shape: pl.BlockSpec(shape, lambda b, t: (0,) * len(shape), pipeline_mode=pl.Buffered(1))
    gate_specs, gate_args = [], []
    if gate is not None:
        z_src, z_blk, norm_g = gate
        gate_specs = [pl.BlockSpec((1, tm, k), lambda b, t: (b, t + skip_tiles, z_blk)), resident(norm_g.shape)]
        gate_args = [z_src, norm_g]
    return pl.pallas_call(
        functools.partial(_mix_ffn_body, n_chunk=n_chunk, alpha=alpha, gated=gate is not None),
        grid=(n_batch, t_tot // tm - skip_tiles),
        in_specs=[
            pl.BlockSpec((1, tm, k), lambda b, t: (b, t + y_skip, 0)),
            *gate_specs,
            resident((k, d)),
            pl.BlockSpec((1, tm, d), lambda b, t: (b, t + skip_tiles, 0)),
            pl.BlockSpec((1, N_MOD, d), lambda b, t: row_map(b, t + skip_tiles)),
            resident(ln.shape),
            resident((d, 2 * d_ff)),
            resident((d_ff, d)),
        ],
        out_specs=pl.BlockSpec((1, tm, d), lambda b, t: (b, t, 0)),
        out_shape=jax.ShapeDtypeStruct((n_batch, t_tot - skip_tiles * tm, d), F32),
        compiler_params=_params(("parallel", "parallel")),
        name="mix_ffn",
    )(y, *gate_args, w_o, xs, mods, ln, w_in, w_out)


def kernel(x, c, ctx, c_ctx, w_mod, b_mod, ln_g, ln_b, w_ffn_in, w_ffn_out, gdn_w_in, gdn_conv, gdn_a_log,
           gdn_dt_bias, gdn_norm_g, gdn_w_out, attn_w_qkv, attn_q_norm, attn_k_norm, attn_w_out):
    n_batch, seq_len, d = x.shape
    ctx_len = ctx.shape[1]
    depth = w_mod.shape[0]
    alpha = (2 * depth) ** 0.25
    n_hv = gdn_a_log.shape[-1]
    dv = gdn_norm_g.shape[-1]
    v_w = n_hv * dv
    qkv_w = gdn_conv.shape[-1]
    n_hk = (qkv_w - v_w) // (2 * GDN_DK)
    n_hq = attn_w_out.shape[1] // ATT_DH
    n_hkv = (attn_w_qkv.shape[-1] // ATT_DH - n_hq) // 2
    assert ctx_len % TOKEN_TILE == 0 and seq_len % TOKEN_TILE == 0 and seq_len % GRID_W == 0
    assert 4 * n_hv <= LANES and d % LANES == 0
    n_ctx_tiles = ctx_len // TOKEN_TILE

    xs = jnp.concatenate([ctx, x], axis=1)

    mp = -(-(n_batch + 1) // SUBLANES) * SUBLANES
    cond = jnp.concatenate([c, c_ctx[None, :], jnp.zeros((mp - n_batch - 1, d), F32)], axis=0)
    mods = _modulation(cond, w_mod, b_mod).reshape(depth, mp, N_MOD, d)
    tables = _rope_tables(ctx_len, seq_len)

    for i in range(depth):
        j = i // 2
        m = mods[i]
        skip = n_ctx_tiles if i == depth - 1 else 0
        if i % 2 == 0:
            w_in = gdn_w_in[j]
            gate_w = jnp.pad(w_in[:, qkv_w + v_w:], ((0, 0), (0, LANES - 4 * n_hv)))
            lanes_pad = (0, LANES - 4 * n_hv)
            a_log = jnp.pad(jnp.concatenate([jnp.zeros((2 * n_hv,), F32), gdn_a_log[j].reshape(-1)]), lanes_pad)
            dt_b = jnp.pad(jnp.concatenate([jnp.zeros((2 * n_hv,), F32), gdn_dt_bias[j].reshape(-1)]), lanes_pad)
            p_main, k_t, colg, rowg = _gdn_in_proj(xs, m, w_in[:, :qkv_w + v_w].astype(BF16), gate_w, gdn_conv[j],
                                                   a_log[None, :], dt_b[None, :], n_ctx_tiles,
                                                   qk_w=n_hk * GDN_DK, n_hv=n_hv, n_chunk=6)
            y = _gdn_delta(p_main, k_t, colg, rowg, dv, ctx_len, n_hk, n_hv)
            w_o = gdn_w_out[j]
            assert qkv_w % v_w == 0
            gate = (p_main, qkv_w // v_w, gdn_norm_g[j][None, :])
        else:
            q, k_t, v1 = _attn_in_proj(xs, m, attn_w_qkv[j].astype(BF16), tables, attn_q_norm[j][None, :],
                                       attn_k_norm[j][None, :], n_ctx_tiles, n_hq, n_hkv)
            y = _attention(q, k_t, v1, ctx_len, skip_tiles=skip)
            w_o = attn_w_out[j]
            gate = None
        ln = jnp.stack([ln_g[i, 0], ln_b[i, 0], ln_g[i, 1], ln_b[i, 1]])
        xs = _mix_ffn(y, w_o.astype(BF16), xs, m, ln, w_ffn_in[i].astype(BF16), w_ffn_out[i].astype(BF16),
                      n_ctx_tiles, alpha, skip_tiles=skip, gate=gate)
    return xs
```

```python
import functools
import math

import jax
import jax.numpy as jnp
from jax import lax
from jax.experimental import pallas as pl
from jax.experimental.pallas import tpu as pltpu

F32 = jnp.float32
BF16 = jnp.bfloat16

N_MOD = 6
EPS = 1e-6
GDN_DK = 128
CHUNK = 64
ATT_DH = 128
GRID_W = 64
ROPE_THETA = 10000.0

LANES = 128
SUBLANES = 8
PAIR = 2 * CHUNK
VMEM_LIMIT_BYTES = 56 * 1024 * 1024
TOKEN_TILE = 256
LOG2E = math.log2(math.e)

assert PAIR == LANES


def _params(sem):
    return pltpu.CompilerParams(dimension_semantics=sem, vmem_limit_bytes=VMEM_LIMIT_BYTES)


def _dot(a, b):
    return jnp.dot(a, b, preferred_element_type=F32)


def _split2(a):
    hi = a.astype(BF16)
    return hi, (a - hi.astype(F32)).astype(BF16)


def _dot_hi(a, b):
    a1, a2 = _split2(a)
    b1, b2 = _split2(b)
    return _dot(a1, b1) + (_dot(a1, b2) + _dot(a2, b1))


def _sigmoid(x):
    return 1.0 / (1.0 + jnp.exp(-x))


def _silu(x):
    return x * _sigmoid(x)


def _layer_norm(r, g, b):
    mu = jnp.mean(r, axis=-1, keepdims=True)
    rc = r - mu
    var = jnp.mean(rc * rc, axis=-1, keepdims=True)
    return rc * lax.rsqrt(var + EPS) * g + b


def _mod_body(cond_ref, w_ref, b_ref, o_ref):
    o_ref[0] = _dot_hi(_silu(cond_ref[...]), w_ref[0]) + b_ref[0]


def _modulation(cond, w_mod, b_mod):
    n_layer, d, n = w_mod.shape
    mp = cond.shape[0]
    tn = n // 4
    return pl.pallas_call(
        _mod_body,
        grid=(n_layer, n // tn),
        in_specs=[
            pl.BlockSpec((mp, d), lambda l, j: (0, 0)),
            pl.BlockSpec((1, d, tn), lambda l, j: (l, 0, j)),
            pl.BlockSpec((1, 1, tn), lambda l, j: (l, 0, j)),
        ],
        out_specs=pl.BlockSpec((1, mp, tn), lambda l, j: (l, 0, j)),
        out_shape=jax.ShapeDtypeStruct((n_layer, mp, n), F32),
        compiler_params=_params(("parallel", "parallel")),
        name="modulation",
    )(cond, w_mod, b_mod.reshape(n_layer, 1, n))


def _mod_row_map(n_batch, n_ctx_tiles):
    return lambda b, t: (jnp.where(t < n_ctx_tiles, n_batch, b), 0, 0)


def _stream_specs(stream, rows, blk):
    arrays = stream if isinstance(stream, tuple) else (stream,)
    d = arrays[0].shape[-1]
    if len(arrays) == 1:
        return [pl.BlockSpec((1, rows, d), lambda b, t: (b, blk(t), 0))], list(arrays)
    n_c, n_l = arrays[0].shape[1] // rows, arrays[1].shape[1] // rows
    return [pl.BlockSpec((1, rows, d), lambda b, t: (b, jnp.clip(blk(t), 0, n_c - 1), 0)),
            pl.BlockSpec((1, rows, d), lambda b, t: (b, jnp.clip(blk(t) - n_c, 0, n_l - 1), 0))], list(arrays)


def _stream_block(refs, is_ctx):
    return refs[0][0] if len(refs) == 1 else jnp.where(is_ctx, refs[0][0], refs[1][0])


def _stream_shape(stream):
    if isinstance(stream, tuple):
        return stream[0].shape[0], stream[0].shape[1] + stream[1].shape[1], stream[0].shape[2]
    return stream.shape


def _rope(x, cos, sin_a, sin_b):
    quarter = ATT_DH // 4
    return x * cos + pltpu.roll(x, ATT_DH - quarter, axis=1) * sin_a + pltpu.roll(x, quarter, axis=1) * sin_b


def _rms_norm(x, g):
    return x * lax.rsqrt(jnp.mean(x * x, axis=-1, keepdims=True) + EPS) * g


def _attn_proj_body(x_ref, mod_ref, w_ref, cos_ref, sa_ref, sb_ref, qn_ref, kn_ref, q_ref, kt_ref, v1_ref,
                    *, n_hq, n_hkv):
    dh = ATT_DH
    hb = (x_ref[0] * (1.0 + mod_ref[0, 1:2, :]) + mod_ref[0, 0:1, :]).astype(BF16)
    cos, sin_a, sin_b = cos_ref[...], sa_ref[...], sb_ref[...]
    q_all = _dot(hb, w_ref[:, :n_hq * dh])
    for h in range(n_hq):
        q = _rope(_rms_norm(q_all[:, h * dh:(h + 1) * dh], qn_ref[...]), cos, sin_a, sin_b)
        q_ref[0, :, h * dh:(h + 1) * dh] = (q * (dh ** -0.5 * LOG2E)).astype(q_ref.dtype)
    kv = _dot(hb, w_ref[:, n_hq * dh:])
    for h in range(n_hkv):
        k = _rope(_rms_norm(kv[:, h * dh:(h + 1) * dh], kn_ref[...]), cos, sin_a, sin_b)
        kt_ref[0, h] = k.T.astype(kt_ref.dtype)
        v1_ref[0, h, :, :dh] = kv[:, (n_hkv + h) * dh:(n_hkv + h + 1) * dh].astype(v1_ref.dtype)
        v1_ref[0, h, :, dh:] = jnp.ones((kv.shape[0], dh), v1_ref.dtype)


def _attn_in_proj(xs, mods, w, tables, q_norm, k_norm, n_ctx_tiles, n_hq, n_hkv):
    n_batch, t_tot, d = xs.shape
    dh = ATT_DH
    tm = TOKEN_TILE
    cos, sin_a, sin_b = tables
    const = lambda shape: pl.BlockSpec(shape, lambda b, t: (0,) * len(shape))
    table = pl.BlockSpec((tm, dh), lambda b, t: (t, 0))
    return pl.pallas_call(
        functools.partial(_attn_proj_body, n_hq=n_hq, n_hkv=n_hkv),
        grid=(n_batch, t_tot // tm),
        in_specs=[
            pl.BlockSpec((1, tm, d), lambda b, t: (b, t, 0)),
            pl.BlockSpec((1, N_MOD, d), _mod_row_map(n_batch, n_ctx_tiles)),
            const(w.shape), table, table, table, const((1, dh)), const((1, dh)),
        ],
        out_specs=[pl.BlockSpec((1, tm, n_hq * dh), lambda b, t: (b, t, 0)),
                   pl.BlockSpec((1, n_hkv, dh, tm), lambda b, t: (b, 0, 0, t)),
                   pl.BlockSpec((1, n_hkv, tm, 2 * dh), lambda b, t: (b, 0, t, 0))],
        out_shape=[jax.ShapeDtypeStruct((n_batch, t_tot, n_hq * dh), BF16),
                   jax.ShapeDtypeStruct((n_batch, n_hkv, dh, t_tot), BF16),
                   jax.ShapeDtypeStruct((n_batch, n_hkv, t_tot, 2 * dh), BF16)],
        compiler_params=_params(("parallel", "parallel")),
        name="attn_in_proj",
    )(xs, mods, w, cos, sin_a, sin_b, q_norm, k_norm)


CONV_HALO = SUBLANES


def _l2_norm(x, scale=1.0):
    return x * (lax.rsqrt(jnp.sum(x * x, axis=-1, keepdims=True) + EPS) * scale)


def _gdn_proj_body(*refs, ctx_len, qk_w, qkv_w, n_chunk, dk, n_hv, n_src):
    x_refs, xp_refs, xn_refs = refs[:n_src], refs[n_src:2 * n_src], refs[2 * n_src:3 * n_src]
    mod_ref, w_ref, wxh_ref, wxl_ref, conv_ref, alog_ref, dtb_ref, o_ref, kt_ref, col_ref, row_ref = refs[3 * n_src:]
    tm = o_ref.shape[1]
    r0 = pl.program_id(1) * tm
    t_tot = pl.num_programs(1) * tm
    is_ctx = r0 < ctx_len
    scale = 1.0 + mod_ref[0, 1:2, :]
    shift = mod_ref[0, 0:1, :]
    h = _stream_block(x_refs, is_ctx) * scale + shift
    hb = h.astype(BF16)
    hb_ext = jnp.concatenate([_stream_block(xp_refs, is_ctx) * scale + shift, h,
                              _stream_block(xn_refs, is_ctx) * scale + shift], axis=0).astype(BF16)
    keep_lo = jnp.where(jnp.logical_or(r0 == 0, r0 == ctx_len), 0.0, 1.0)
    keep_hi = jnp.where(jnp.logical_or(r0 + tm == ctx_len, r0 + tm == t_tot), 0.0, 1.0)
    conv_w = conv_ref.shape[0]
    pad = conv_w // 2
    n_ext = tm + 2 * CONV_HALO
    mid = slice(CONV_HALO, CONV_HALO + tm)
    tn = w_ref.shape[1] // n_chunk
    for j in range(n_chunk):
        c0 = j * tn
        if c0 >= qkv_w:
            o_ref[0, :, c0:c0 + tn] = _dot(hb, w_ref[:, c0:c0 + tn]).astype(o_ref.dtype)
            continue
        p = _dot(hb_ext, w_ref[:, c0:c0 + tn])
        ext = jnp.concatenate([p[:CONV_HALO] * keep_lo, p[mid], p[CONV_HALO + tm:] * keep_hi], axis=0)
        y = None
        for tap in range(conv_w):
            off = tap - pad
            xs = ext if off == 0 else pltpu.roll(ext, (-off) % n_ext, axis=0)
            term = xs[mid] * conv_ref[tap:tap + 1, c0:c0 + tn]
            y = term if y is None else y + term
        y = _silu(y)
        if c0 >= 2 * qk_w:
            o_ref[0, :, c0:c0 + tn] = y.astype(o_ref.dtype)
            continue
        for hh in range(tn // dk):
            yh = _l2_norm(y[:, hh * dk:(hh + 1) * dk], scale=dk ** -0.5 if c0 < qk_w else 1.0)
            o_ref[0, :, c0 + hh * dk:c0 + (hh + 1) * dk] = yh.astype(o_ref.dtype)
            if c0 >= qk_w:
                for cc in range(tm // CHUNK):
                    kc = yh[cc * CHUNK:(cc + 1) * CHUNK]
                    kt_ref[0, (c0 - qk_w) // dk + hh, cc] = jnp.concatenate([kc, kc], axis=0).T.astype(kt_ref.dtype)
    h_lo = (h - hb.astype(F32)).astype(BF16)
    raw = _dot(hb, wxh_ref[...]) + (_dot(hb, wxl_ref[...]) + _dot(h_lo, wxh_ref[...]))
    _gate_forms(raw, alog_ref[...], dtb_ref[...], col_ref, row_ref, n_hv)


def _gate_forms(raw, a_log, dt_bias, col_ref, row_ref, n_hv):
    beta = _sigmoid(raw)
    xs = raw + dt_bias
    g = -jnp.exp(a_log) * (jnp.maximum(xs, 0.0) + jnp.log1p(jnp.exp(-jnp.abs(xs))))
    r = lax.broadcasted_iota(jnp.int32, (PAIR, PAIR), 0)
    c = lax.broadcasted_iota(jnp.int32, (PAIR, PAIR), 1)
    same = (r < CHUNK) == (c < CHUNK)
    tri_f = jnp.where(jnp.logical_and(same, c <= r), 1.0, 0.0).astype(BF16)
    tri_r = jnp.where(jnp.logical_and(same, c >= r), 1.0, 0.0).astype(BF16)
    lane = lax.broadcasted_iota(jnp.int32, (PAIR, LANES), 1)
    for p in range(raw.shape[0] // PAIR):
        gp = g[p * PAIR:(p + 1) * PAIR]
        g1 = gp.astype(BF16)
        r1 = gp - g1.astype(F32)
        g2 = r1.astype(BF16)
        g3 = (r1 - g2.astype(F32)).astype(BF16)
        cum_f = _dot(tri_f, g1) + (_dot(tri_f, g2) + _dot(tri_f, g3))
        cum_r = _dot(tri_r, g1) + (_dot(tri_r, g2) + _dot(tri_r, g3))
        col = jnp.where(lane < 2 * n_hv, beta[p * PAIR:(p + 1) * PAIR],
                        jnp.where(lane < 3 * n_hv, cum_f, cum_r))
        for grp in range(col_ref.shape[1]):
            shift = (LANES - 2 * HEADS_PER_STEP * grp) % LANES
            col_ref[0, grp, p * PAIR:(p + 1) * PAIR, :] = col if shift == 0 else pltpu.roll(col, shift, axis=1)
        row_ref[0, p] = col.T


def _gdn_in_proj(xs, mods, w, w_gate, conv, a_log_lanes, dt_bias_lanes, n_ctx_tiles, qk_w, n_hv, n_chunk):
    n_batch, t_tot, d = _stream_shape(xs)
    n = w.shape[1]
    conv_w, qkv_w = conv.shape
    tm = TOKEN_TILE
    tn = n // n_chunk
    assert qk_w % tn == 0 and qkv_w % tn == 0 and tn % GDN_DK == 0 and tm % PAIR == 0
    halo_per_tile = tm // CONV_HALO
    last_halo = t_tot // CONV_HALO - 1
    w_gate_hi, w_gate_lo = _split2(w_gate)
    n_hk = qk_w // GDN_DK
    const = lambda shape: pl.BlockSpec(shape, lambda b, t: (0,) * len(shape))
    tile_specs, tile_args = _stream_specs(xs, tm, lambda t: t)
    prev_specs, prev_args = _stream_specs(xs, CONV_HALO, lambda t: jnp.maximum(t * halo_per_tile - 1, 0))
    next_specs, next_args = _stream_specs(xs, CONV_HALO, lambda t: jnp.minimum((t + 1) * halo_per_tile, last_halo))
    return pl.pallas_call(
        functools.partial(_gdn_proj_body, ctx_len=n_ctx_tiles * tm, qk_w=qk_w, qkv_w=qkv_w, n_chunk=n_chunk,
                          dk=GDN_DK, n_hv=n_hv, n_src=len(tile_args)),
        grid=(n_batch, t_tot // tm),
        in_specs=[
            *tile_specs, *prev_specs, *next_specs,
            pl.BlockSpec((1, N_MOD, d), _mod_row_map(n_batch, n_ctx_tiles)),
            const((d, n)), const((d, LANES)), const((d, LANES)), const((conv_w, qkv_w)),
            const((1, LANES)), const((1, LANES)),
        ],
        out_specs=[pl.BlockSpec((1, tm, n), lambda b, t: (b, t, 0)),
                   pl.BlockSpec((1, n_hk, tm // CHUNK, GDN_DK, 2 * CHUNK), lambda b, t: (b, 0, t, 0, 0)),
                   pl.BlockSpec((1, n_hk // HEADS_PER_STEP, tm, LANES), lambda b, t: (b, 0, t, 0)),
                   pl.BlockSpec((1, tm // PAIR, LANES, PAIR), lambda b, t: (b, t, 0, 0))],
        out_shape=[jax.ShapeDtypeStruct((n_batch, t_tot, n), BF16),
                   jax.ShapeDtypeStruct((n_batch, n_hk, t_tot // CHUNK, GDN_DK, 2 * CHUNK), BF16),
                   jax.ShapeDtypeStruct((n_batch, n_hk // HEADS_PER_STEP, t_tot, LANES), F32),
                   jax.ShapeDtypeStruct((n_batch, t_tot // PAIR, LANES, PAIR), F32)],
        compiler_params=_params(("parallel", "parallel")),
        name="gdn_in_proj",
    )(*tile_args, *prev_args, *next_args, mods, w, w_gate_hi, w_gate_lo, conv, a_log_lanes, dt_bias_lanes)


STAGE_TILE = 256
HEADS_PER_STEP = 2
CHUNKS_PER_STEP = 4
SUPER = CHUNKS_PER_STEP * CHUNK


def _delta_body(q_ref, k_ref, v_ref, kt_ref, colg_ref, rowg_ref,
                y_ref, o_s, st_s, uw_s, qkd_s, *, ctx_len, n_hv, n_m):
    g_idx = pl.program_id(1)
    t_tot = q_ref.shape[1]
    dk = q_ref.shape[2] // n_m
    dv = v_ref.shape[2] // (2 * n_m)
    n_sup = t_tot // SUPER
    n_ctx_sup = ctx_len // SUPER

    st_s[...] = jnp.zeros_like(st_s)

    r = lax.broadcasted_iota(jnp.int32, (CHUNK, LANES), 0)
    lane = lax.broadcasted_iota(jnp.int32, (CHUNK, LANES), 1)
    c = lane & (CHUNK - 1)
    left = lane < CHUNK
    left_row = lax.broadcasted_iota(jnp.int32, (1, LANES), 1) < CHUNK
    keep_left = jnp.where(left, 1.0, 0.0).astype(BF16)
    keep_right = jnp.where(left, 0.0, 1.0).astype(BF16)
    eye2 = jnp.where(c == r, 1.0, 0.0)
    n_lvl = CHUNK.bit_length() - 1
    lvl_mask = [jnp.logical_and((r >> (lg + 1)) == (c >> (lg + 1)), (r >> lg) != (c >> lg)) for lg in range(n_lvl)]
    incl = (c <= r, c >= r)
    strict = (c < r, c > r)
    zero_b = jnp.zeros((CHUNK, dv), BF16)

    def block_diag(y):
        return jnp.concatenate([y * keep_left, y * keep_right], axis=0)

    def reverse_super(i):
        return jnp.where(i < n_ctx_sup, n_ctx_sup - 1 - i, n_sup + n_ctx_sup - 1 - i)

    def make_units(i):
        sup = (i, reverse_super(i))
        units = []
        for m in range(n_m):
            for d in (0, 1):
                for seq in range(CHUNKS_PER_STEP):
                    cidx = seq if d == 0 else CHUNKS_PER_STEP - 1 - seq
                    units.append(dict(
                        m=m, d=d, seq=seq, half=cidx % 2, uid=(m * 2 + d) * CHUNKS_PER_STEP + seq,
                        pair=sup[d] * (SUPER // PAIR) + cidx // 2, chunk=sup[d] * CHUNKS_PER_STEP + cidx,
                        rows=pl.ds(pl.multiple_of(sup[d] * SUPER + cidx * CHUNK, CHUNK), CHUNK)))
        return units

    def load_gates(u):
        m, d, half = u["m"], u["d"], u["half"]
        cg = colg_ref[0, 0, u["rows"], :]
        b0, g0 = d * n_hv + 2 * m, (2 + d) * n_hv + 2 * m
        u["beta_c"] = (cg[:, b0:b0 + 1], cg[:, b0 + 1:b0 + 2])
        u["gc_c"] = (cg[:, g0:g0 + 1], cg[:, g0 + 1:g0 + 2])
        head0 = 2 * (n_m * g_idx + m)

        def packed_rows(base):
            rows = rowg_ref[0, u["pair"], pl.ds(base + head0, 2), :]
            a, b = rows[0:1], rows[1:2]
            if half == 0:
                return jnp.where(left_row, a, pltpu.roll(b, CHUNK, axis=1))
            return jnp.where(left_row, pltpu.roll(a, CHUNK, axis=1), b)

        u["beta_r"] = packed_rows(d * n_hv)
        u["gc_r"] = packed_rows((2 + d) * n_hv)

    def state_free_part(i, slot):
        units = make_units(i)
        for u in units:
            m = u["m"]
            load_gates(u)
            qb, kb = q_ref[0, u["rows"], m * dk:(m + 1) * dk], k_ref[0, u["rows"], m * dk:(m + 1) * dk]
            u["kb"] = kb
            u["gram"] = lax.dot_general(jnp.concatenate([kb, qb], axis=0), jnp.concatenate([kb, kb], axis=0),
                                        (((1,), (1,)), ((), ())), preferred_element_type=F32)
        yield
        for u in units:
            d = u["d"]
            gc_c2 = jnp.where(left, u["gc_c"][0], u["gc_c"][1])
            beta_c2 = jnp.where(left, u["beta_c"][0], u["beta_c"][1])
            decay = jnp.where(incl[d], jnp.exp(jnp.where(incl[d], gc_c2 - u["gc_r"], 0.0)), 0.0)
            a = jnp.where(strict[d], u["gram"][0:CHUNK] * decay, 0.0) * beta_c2
            qkd_s[slot, u["uid"]] = (u["gram"][CHUNK:2 * CHUNK] * decay).astype(BF16)
            u["a"] = a
            u["x"] = eye2 - jnp.where(lvl_mask[0], a, 0.0)
        yield
        for lg in range(1, n_lvl):
            for u in units:
                u["xb"] = u["x"].astype(BF16)
                off = jnp.where(lvl_mask[lg], u["a"], 0.0).astype(BF16)
                u["z"] = _dot(off, block_diag(u["xb"]))
            yield
            for u in units:
                u["x"] = u["x"] - _dot(u["xb"], block_diag(u["z"].astype(BF16)))
            yield
        for u in units:
            m = u["m"]
            tb = u["x"] * u["beta_r"]
            tbe = tb * jnp.exp(u["gc_r"])
            v2 = v_ref[0, u["rows"], 2 * m * dv:2 * (m + 1) * dv]
            v_bd = jnp.concatenate([jnp.concatenate([v2[:, :dv], zero_b], axis=1),
                                    jnp.concatenate([zero_b, v2[:, dv:]], axis=1)], axis=0)
            k_bd = jnp.concatenate([jnp.concatenate([u["kb"], zero_b], axis=1),
                                    jnp.concatenate([zero_b, u["kb"]], axis=1)], axis=0)
            uw_s[slot, u["uid"], :, :2 * dv] = _dot(tb.astype(BF16), v_bd)
            uw_s[slot, u["uid"], :, 2 * dv:] = _dot(tbe.astype(BF16), k_bd)
        yield

    def recurrence_part(i, slot):
        units = make_units(i)
        for seq in range(CHUNKS_PER_STEP):
            now = [u for u in units if u["seq"] == seq]
            for u in now:
                m = u["m"]
                load_gates(u)
                uw = uw_s[slot, u["uid"]]
                u["u"] = uw[:, :2 * dv]
                u["s2"] = st_s[2 * m + u["d"]]
                lhs = jnp.concatenate([uw[:, 2 * dv:3 * dv].astype(BF16), uw[:, 3 * dv:].astype(BF16),
                                       q_ref[0, u["rows"], m * dk:(m + 1) * dk]], axis=0)
                u["ws"] = _dot(lhs, u["s2"].astype(BF16))
            yield
            for u in now:
                d, ws = u["d"], u["ws"]
                vn0 = (u["u"][:, :dv] - ws[0:CHUNK, :dv]).astype(BF16)
                vn1 = (u["u"][:, dv:] - ws[CHUNK:2 * CHUNK, dv:]).astype(BF16)
                vn_bd = jnp.concatenate([jnp.concatenate([vn0, zero_b], axis=1),
                                         jnp.concatenate([zero_b, vn1], axis=1)], axis=0)
                last = (CHUNK - 1, 2 * CHUNK - 1) if d == 0 else (0, CHUNK)
                u["gl"] = [u["gc_r"][:, l:l + 1] for l in last]
                gl2 = jnp.where(left_row, u["gl"][0], u["gl"][1])
                kdt = kt_ref[0, u["m"], u["chunk"]].astype(F32) * jnp.exp(gl2 - u["gc_r"])
                lhs = jnp.concatenate([qkd_s[slot, u["uid"]], kdt.astype(BF16)], axis=0)
                u["ov"] = _dot(lhs, vn_bd)
            yield
            for u in now:
                m, ws, ov = u["m"], u["ws"], u["ov"]
                qs = jnp.concatenate([ws[2 * CHUNK:, :dv] * jnp.exp(u["gc_c"][0]),
                                      ws[2 * CHUNK:, dv:] * jnp.exp(u["gc_c"][1])], axis=1)
                o_s[u["d"], u["rows"], 2 * m * dv:2 * (m + 1) * dv] = qs + ov[0:CHUNK]
                egl = jnp.concatenate([jnp.broadcast_to(jnp.exp(u["gl"][0]), (1, dv)),
                                       jnp.broadcast_to(jnp.exp(u["gl"][1]), (1, dv))], axis=1)
                st_s[2 * m + u["d"]] = u["s2"] * egl + ov[CHUNK:]
            yield

    def run_interleaved(parts):
        live = list(parts)
        while live:
            live = [g for g in live if next(g, "done") != "done"]

    def step(i, carry):
        slot = i & 1
        run_interleaved([state_free_part(i, slot), recurrence_part(i - 1, 1 - slot)])
        return carry

    run_interleaved([state_free_part(jnp.int32(0), 0)])
    lax.fori_loop(1, n_sup, step, 0)
    run_interleaved([recurrence_part(jnp.int32(n_sup - 1), (n_sup - 1) & 1)])

    def finish(i, carry):
        rows = pl.ds(pl.multiple_of(i * STAGE_TILE, STAGE_TILE), STAGE_TILE)
        y_ref[0, rows, :] = (o_s[0, rows, :] + o_s[1, rows, :]).astype(y_ref.dtype)
        return carry

    lax.fori_loop(0, t_tot // STAGE_TILE, finish, 0)


def _gdn_delta(p_main, k_t, colg, rowg, dv, ctx_len, n_hk, n_hv):
    n_batch, t_tot, _ = p_main.shape
    dk = GDN_DK
    assert n_hv == 2 * n_hk and dk == dv == LANES and n_hk % HEADS_PER_STEP == 0
    assert ctx_len % SUPER == 0 and t_tot % SUPER == 0 and SUPER % STAGE_TILE == 0
    n_m = HEADS_PER_STEP
    qb = n_m * dk
    vb = 2 * n_m * dv
    k_blk0 = n_hk // n_m
    v_blk0 = (2 * n_hk * dk) // vb
    n_pairs = t_tot // PAIR
    n_units = 2 * n_m * CHUNKS_PER_STEP
    return pl.pallas_call(
        functools.partial(_delta_body, ctx_len=ctx_len, n_hv=n_hv, n_m=n_m),
        grid=(n_batch, n_hk // n_m),
        in_specs=[
            pl.BlockSpec((1, t_tot, qb), lambda b, h: (b, 0, h)),
            pl.BlockSpec((1, t_tot, qb), lambda b, h: (b, 0, k_blk0 + h)),
            pl.BlockSpec((1, t_tot, vb), lambda b, h: (b, 0, v_blk0 + h)),
            pl.BlockSpec((1, n_m, t_tot // CHUNK, dk, 2 * CHUNK), lambda b, h: (b, h, 0, 0, 0)),
            pl.BlockSpec((1, 1, t_tot, LANES), lambda b, h: (b, h, 0, 0)),
            pl.BlockSpec((1, n_pairs, LANES, PAIR), lambda b, h: (b, 0, 0, 0)),
        ],
        out_specs=pl.BlockSpec((1, t_tot, vb), lambda b, h: (b, 0, h)),
        out_shape=jax.ShapeDtypeStruct((n_batch, t_tot, n_hv * dv), BF16),
        scratch_shapes=[
            pltpu.VMEM((2, t_tot, vb), F32),
            pltpu.VMEM((2 * n_m, dk, 2 * dv), F32),
            pltpu.VMEM((2, n_units, CHUNK, 2 * dv + 2 * dk), F32),
            pltpu.VMEM((2, n_units, CHUNK, 2 * CHUNK), BF16),
        ],
        compiler_params=_params(("parallel", "parallel")),
        name="gdn_delta",
    )(p_main, p_main, p_main, k_t, colg, rowg)


def _attn_body(q_ref, kt_ref, v1_ref, o_ref, *, ctx_len, n_group, skip_tiles):
    tq = q_ref.shape[1]
    dh = kt_ref.shape[2]
    t_tot = kt_ref.shape[3]
    r0 = (pl.program_id(2) + skip_tiles) * tq

    def attend(n_keys):
        def scores(g):
            return _dot(q_ref[0, :, g * dh:(g + 1) * dh], kt_ref[0, 0, :, :n_keys])

        s_next = scores(0)
        for g in range(n_group):
            s = s_next
            if g + 1 < n_group:
                s_next = scores(g + 1)
            p = jnp.exp2(s - jnp.max(s, axis=-1, keepdims=True))
            num_den = _dot(p.astype(BF16), v1_ref[0, 0, :n_keys, :])
            o_ref[0, :, g * dh:(g + 1) * dh] = (num_den[:, :dh] / num_den[:, dh:]).astype(o_ref.dtype)

    @pl.when(r0 < ctx_len)
    def _():
        attend(ctx_len)

    @pl.when(r0 >= ctx_len)
    def _():
        attend(t_tot)


def _attention(q, k_t, v1, ctx_len, skip_tiles=0):
    n_batch, t_tot, q_w = q.shape
    _, n_hkv, dh, _ = k_t.shape
    n_group = q_w // (n_hkv * dh)
    tq = TOKEN_TILE
    return pl.pallas_call(
        functools.partial(_attn_body, ctx_len=ctx_len, n_group=n_group, skip_tiles=skip_tiles),
        grid=(n_batch, n_hkv, t_tot // tq - skip_tiles),
        in_specs=[
            pl.BlockSpec((1, tq, n_group * dh), lambda b, h, i: (b, i + skip_tiles, h)),
            pl.BlockSpec((1, 1, dh, t_tot), lambda b, h, i: (b, h, 0, 0)),
            pl.BlockSpec((1, 1, t_tot, 2 * dh), lambda b, h, i: (b, h, 0, 0)),
        ],
        out_specs=pl.BlockSpec((1, tq, n_group * dh), lambda b, h, i: (b, i, h)),
        out_shape=jax.ShapeDtypeStruct((n_batch, t_tot - skip_tiles * tq, q_w), BF16),
        compiler_params=_params(("parallel", "parallel", "parallel")),
        name="attention",
    )(q, k_t, v1)


def _rope_tables(ctx_len, seq_len):
    rows = seq_len // GRID_W
    row = jnp.repeat(jnp.arange(rows), GRID_W).astype(F32)
    col = jnp.tile(jnp.arange(GRID_W), rows).astype(F32)
    n_freq = ATT_DH // 4
    freqs = ROPE_THETA ** (-jnp.arange(n_freq, dtype=F32) / n_freq)
    ang_r = row[:, None] * freqs
    ang_c = col[:, None] * freqs
    ang = jnp.concatenate([ang_r, ang_r, ang_c, ang_c], axis=-1)
    cos = jnp.concatenate([jnp.ones((ctx_len, ATT_DH), F32), jnp.cos(ang)], axis=0)
    sin = jnp.concatenate([jnp.zeros((ctx_len, ATT_DH), F32), jnp.sin(ang)], axis=0)
    first_half = (jnp.arange(ATT_DH) % (2 * n_freq)) < n_freq
    return cos, jnp.where(first_half, -sin, 0.0), jnp.where(first_half, 0.0, sin)


ROW_SPLIT = 2

def _mix_ffn_body(y_ref, *refs, n_chunk, alpha, gated, n_src, ctx_tiles_left):
    if gated:
        z_ref, ng_ref, wo_ref = refs[:3]
        x_refs, (mod_ref, ln_ref, win_ref, wout_ref, o_ref) = refs[3:3 + n_src], refs[3 + n_src:]
        dv = ng_ref.shape[1]
        kw = 2 * dv
        delta = None
        for pp in range(y_ref.shape[2] // kw):
            pair = []
            for hh in (2 * pp, 2 * pp + 1):
                o = y_ref[0, :, hh * dv:(hh + 1) * dv].astype(F32)
                z = z_ref[0, :, hh * dv:(hh + 1) * dv].astype(F32)
                pair.append((_rms_norm(o, ng_ref[...]) * _silu(z)).astype(BF16))
            part = _dot(jnp.concatenate(pair, axis=1), wo_ref[pp * kw:(pp + 1) * kw, :])
            delta = part if delta is None else delta + part
    else:
        wo_ref = refs[0]
        x_refs, (mod_ref, ln_ref, win_ref, wout_ref, o_ref) = refs[1:1 + n_src], refs[1 + n_src:]
        delta = None
    rows = o_ref.shape[1] // ROW_SPLIT
    parts = [slice(i * rows, (i + 1) * rows) for i in range(ROW_SPLIT)]
    deltas = [_dot(y_ref[0, r, :], wo_ref[...]) if delta is None else delta[r] for r in parts]
    x_in = _stream_block(x_refs, pl.program_id(1) < ctx_tiles_left)
    x1s = [_layer_norm(alpha * x_in[r] + mod_ref[0, 2:3, :] * dl, ln_ref[0:1, :], ln_ref[1:2, :])
           for r, dl in zip(parts, deltas)]
    hbs = [(x1 * (1.0 + mod_ref[0, 4:5, :]) + mod_ref[0, 3:4, :]).astype(BF16) for x1 in x1s]
    d_ff = wout_ref.shape[0]
    tc = d_ff // n_chunk
    accs = [None] * ROW_SPLIT
    for c in range(n_chunk):
        for i in range(ROW_SPLIT):
            gate = _dot(hbs[i], win_ref[:, c * tc:(c + 1) * tc])
            up = _dot(hbs[i], win_ref[:, d_ff + c * tc:d_ff + (c + 1) * tc])
            part = _dot((_silu(gate) * up).astype(BF16), wout_ref[c * tc:(c + 1) * tc, :])
            accs[i] = part if accs[i] is None else accs[i] + part
    for i, r in enumerate(parts):
        o_ref[0, r, :] = _layer_norm(alpha * x1s[i] + mod_ref[0, 5:6, :] * accs[i], ln_ref[2:3, :], ln_ref[3:4, :])


def _mix_ffn(y, w_o, xs, mods, ln, w_in, w_out, n_ctx_tiles, alpha, skip_tiles=0, gate=None):
    n_batch, t_tot, d = _stream_shape(xs)
    k = y.shape[-1]
    d_ff = w_out.shape[0]
    tm = TOKEN_TILE
    n_chunk = 2 if d_ff % (2 * LANES) == 0 else 1
    row_map = _mod_row_map(n_batch, n_ctx_tiles)
    y_skip = skip_tiles - (t_tot - y.shape[1]) // tm
    x_specs, x_args = _stream_specs(xs, tm, lambda t: t + skip_tiles)
    resident = lambda shape: pl.BlockSpec(shape, lambda b, t: (0,) * len(shape), pipeline_mode=pl.Buffered(1))
    gate_specs, gate_args = [], []
    if gate is not None:
        z_src, z_blk, norm_g = gate
        gate_specs = [pl.BlockSpec((1, tm, k), lambda b, t: (b, t + skip_tiles, z_blk)), resident(norm_g.shape)]
        gate_args = [z_src, norm_g]
    return pl.pallas_call(
        functools.partial(_mix_ffn_body, n_chunk=n_chunk, alpha=alpha, gated=gate is not None,
                          n_src=len(x_args), ctx_tiles_left=n_ctx_tiles - skip_tiles),
        grid=(n_batch, t_tot // tm - skip_tiles),
        in_specs=[
            pl.BlockSpec((1, tm, k), lambda b, t: (b, t + y_skip, 0)),
            *gate_specs,
            resident((k, d)),
            *x_specs,
            pl.BlockSpec((1, N_MOD, d), lambda b, t: row_map(b, t + skip_tiles)),
            resident(ln.shape),
            resident((d, 2 * d_ff)),
            resident((d_ff, d)),
        ],
        out_specs=pl.BlockSpec((1, tm, d), lambda b, t: (b, t, 0)),
        out_shape=jax.ShapeDtypeStruct((n_batch, t_tot - skip_tiles * tm, d), F32),
        compiler_params=_params(("parallel", "parallel")),
        name="mix_ffn",
    )(y, *gate_args, w_o, *x_args, mods, ln, w_in, w_out)


def kernel(x, c, ctx, c_ctx, w_mod, b_mod, ln_g, ln_b, w_ffn_in, w_ffn_out, gdn_w_in, gdn_conv, gdn_a_log,
           gdn_dt_bias, gdn_norm_g, gdn_w_out, attn_w_qkv, attn_q_norm, attn_k_norm, attn_w_out):
    n_batch, seq_len, d = x.shape
    ctx_len = ctx.shape[1]
    depth = w_mod.shape[0]
    alpha = (2 * depth) ** 0.25
    n_hv = gdn_a_log.shape[-1]
    dv = gdn_norm_g.shape[-1]
    v_w = n_hv * dv
    qkv_w = gdn_conv.shape[-1]
    n_hk = (qkv_w - v_w) // (2 * GDN_DK)
    n_hq = attn_w_out.shape[1] // ATT_DH
    n_hkv = (attn_w_qkv.shape[-1] // ATT_DH - n_hq) // 2
    assert ctx_len % TOKEN_TILE == 0 and seq_len % TOKEN_TILE == 0 and seq_len % GRID_W == 0
    assert 4 * n_hv <= LANES and d % LANES == 0
    n_ctx_tiles = ctx_len // TOKEN_TILE

    xs = (ctx, x)

    mp = -(-(n_batch + 1) // SUBLANES) * SUBLANES
    cond = jnp.concatenate([c, c_ctx[None, :], jnp.zeros((mp - n_batch - 1, d), F32)], axis=0)
    mods = _modulation(cond, w_mod, b_mod).reshape(depth, mp, N_MOD, d)
    tables = _rope_tables(ctx_len, seq_len)

    for i in range(depth):
        j = i // 2
        m = mods[i]
        skip = n_ctx_tiles if i == depth - 1 else 0
        if i % 2 == 0:
            w_in = gdn_w_in[j]
            gate_w = jnp.pad(w_in[:, qkv_w + v_w:], ((0, 0), (0, LANES - 4 * n_hv)))
            lanes_pad = (0, LANES - 4 * n_hv)
            a_log = jnp.pad(jnp.concatenate([jnp.zeros((2 * n_hv,), F32), gdn_a_log[j].reshape(-1)]), lanes_pad)
            dt_b = jnp.pad(jnp.concatenate([jnp.zeros((2 * n_hv,), F32), gdn_dt_bias[j].reshape(-1)]), lanes_pad)
            p_main, k_t, colg, rowg = _gdn_in_proj(xs, m, w_in[:, :qkv_w + v_w].astype(BF16), gate_w, gdn_conv[j],
                                                   a_log[None, :], dt_b[None, :], n_ctx_tiles,
                                                   qk_w=n_hk * GDN_DK, n_hv=n_hv, n_chunk=6)
            y = _gdn_delta(p_main, k_t, colg, rowg, dv, ctx_len, n_hk, n_hv)
            w_o = gdn_w_out[j]
            assert qkv_w % v_w == 0
            gate = (p_main, qkv_w // v_w, gdn_norm_g[j][None, :])
        else:
            q, k_t, v1 = _attn_in_proj(xs, m, attn_w_qkv[j].astype(BF16), tables, attn_q_norm[j][None, :],
                                       attn_k_norm[j][None, :], n_ctx_tiles, n_hq, n_hkv)
            y = _attention(q, k_t, v1, ctx_len, skip_tiles=skip)
            w_o = attn_w_out[j]
            gate = None
        ln = jnp.stack([ln_g[i, 0], ln_b[i, 0], ln_g[i, 1], ln_b[i, 1]])
        xs = _mix_ffn(y, w_o.astype(BF16), xs, m, ln, w_ffn_in[i].astype(BF16), w_ffn_out[i].astype(BF16),
                      n_ctx_tiles, alpha, skip_tiles=skip, gate=gate)
    return xs
```

```python
import functools
import math

import jax
import jax.numpy as jnp
from jax import lax
from jax.experimental import pallas as pl
from jax.experimental.pallas import tpu as pltpu

F32 = jnp.float32
BF16 = jnp.bfloat16

N_MOD = 6
EPS = 1e-6
GDN_DK = 128
CHUNK = 64
ATT_DH = 128
GRID_W = 64
ROPE_THETA = 10000.0

LANES = 128
SUBLANES = 8
PAIR = 2 * CHUNK
VMEM_LIMIT_BYTES = 56 * 1024 * 1024
TOKEN_TILE = 256
LOG2E = math.log2(math.e)

assert PAIR == LANES


def _params(sem):
    return pltpu.CompilerParams(dimension_semantics=sem, vmem_limit_bytes=VMEM_LIMIT_BYTES)


def _dot(a, b):
    return jnp.dot(a, b, preferred_element_type=F32)


def _split2(a):
    hi = a.astype(BF16)
    return hi, (a - hi.astype(F32)).astype(BF16)


def _dot_hi(a, b):
    a1, a2 = _split2(a)
    b1, b2 = _split2(b)
    return _dot(a1, b1) + (_dot(a1, b2) + _dot(a2, b1))


def _sigmoid(x):
    return 1.0 / (1.0 + jnp.exp(-x))


def _silu(x):
    return x * _sigmoid(x)


def _layer_norm(r, g, b):
    mu = jnp.mean(r, axis=-1, keepdims=True)
    rc = r - mu
    var = jnp.mean(rc * rc, axis=-1, keepdims=True)
    return rc * lax.rsqrt(var + EPS) * g + b


def _mod_body(cond_ref, w_ref, b_ref, o_ref):
    o_ref[0] = _dot_hi(_silu(cond_ref[...]), w_ref[0]) + b_ref[0]


def _modulation(cond, w_mod, b_mod):
    n_layer, d, n = w_mod.shape
    mp = cond.shape[0]
    tn = n // 4
    return pl.pallas_call(
        _mod_body,
        grid=(n_layer, n // tn),
        in_specs=[
            pl.BlockSpec((mp, d), lambda l, j: (0, 0)),
            pl.BlockSpec((1, d, tn), lambda l, j: (l, 0, j)),
            pl.BlockSpec((1, 1, tn), lambda l, j: (l, 0, j)),
        ],
        out_specs=pl.BlockSpec((1, mp, tn), lambda l, j: (l, 0, j)),
        out_shape=jax.ShapeDtypeStruct((n_layer, mp, n), F32),
        compiler_params=_params(("parallel", "parallel")),
        name="modulation",
    )(cond, w_mod, b_mod.reshape(n_layer, 1, n))


def _mod_row_map(n_batch, n_ctx_tiles):
    return lambda b, t: (jnp.where(t < n_ctx_tiles, n_batch, b), 0, 0)


def _stream_specs(stream, rows, blk):
    arrays = stream if isinstance(stream, tuple) else (stream,)
    d = arrays[0].shape[-1]
    if len(arrays) == 1:
        return [pl.BlockSpec((1, rows, d), lambda b, t: (b, blk(t), 0))], list(arrays)
    n_c, n_l = arrays[0].shape[1] // rows, arrays[1].shape[1] // rows
    return [pl.BlockSpec((1, rows, d), lambda b, t: (b, jnp.clip(blk(t), 0, n_c - 1), 0)),
            pl.BlockSpec((1, rows, d), lambda b, t: (b, jnp.clip(blk(t) - n_c, 0, n_l - 1), 0))], list(arrays)


def _stream_block(refs, is_ctx):
    return refs[0][0] if len(refs) == 1 else jnp.where(is_ctx, refs[0][0], refs[1][0])


def _stream_shape(stream):
    if isinstance(stream, tuple):
        return stream[0].shape[0], stream[0].shape[1] + stream[1].shape[1], stream[0].shape[2]
    return stream.shape


def _head_perm():
    n_freq = ATT_DH // 4
    new = jnp.arange(ATT_DH)
    half, axis, f = new // (2 * n_freq), (new % (2 * n_freq)) // n_freq, new % n_freq
    return axis * (2 * n_freq) + half * n_freq + f


def _rope(x, cos, sin_signed):
    return x * cos + pltpu.roll(x, ATT_DH // 2, axis=1) * sin_signed


def _rms_norm(x, g):
    return x * lax.rsqrt(jnp.mean(x * x, axis=-1, keepdims=True) + EPS) * g


def _attn_proj_body(x_ref, mod_ref, w_ref, cos_ref, sin_ref, qn_ref, kn_ref, q_ref, kt_ref, v1_ref,
                    *, n_hq, n_hkv):
    dh = ATT_DH
    hb = (x_ref[0] * (1.0 + mod_ref[0, 1:2, :]) + mod_ref[0, 0:1, :]).astype(BF16)
    cos, sin = cos_ref[...], sin_ref[...]
    q_all = _dot(hb, w_ref[:, :n_hq * dh])
    for h in range(n_hq):
        q = _rope(_rms_norm(q_all[:, h * dh:(h + 1) * dh], qn_ref[...]), cos, sin)
        q_ref[0, :, h * dh:(h + 1) * dh] = (q * (dh ** -0.5 * LOG2E)).astype(q_ref.dtype)
    kv = _dot(hb, w_ref[:, n_hq * dh:])
    for h in range(n_hkv):
        k = _rope(_rms_norm(kv[:, h * dh:(h + 1) * dh], kn_ref[...]), cos, sin)
        kt_ref[0, h] = k.T.astype(kt_ref.dtype)
        v1_ref[0, h, :, :dh] = kv[:, (n_hkv + h) * dh:(n_hkv + h + 1) * dh].astype(v1_ref.dtype)
        v1_ref[0, h, :, dh:] = jnp.ones((kv.shape[0], dh), v1_ref.dtype)


def _attn_in_proj(xs, mods, w, tables, q_norm, k_norm, n_ctx_tiles, n_hq, n_hkv):
    n_batch, t_tot, d = xs.shape
    dh = ATT_DH
    tm = TOKEN_TILE
    cos, sin = tables
    perm = _head_perm()
    n_qk = (n_hq + n_hkv) * dh
    w = jnp.concatenate([w[:, :n_qk].reshape(d, n_hq + n_hkv, dh)[:, :, perm].reshape(d, n_qk), w[:, n_qk:]], axis=1)
    q_norm, k_norm = q_norm[:, perm], k_norm[:, perm]
    const = lambda shape: pl.BlockSpec(shape, lambda b, t: (0,) * len(shape))
    table = pl.BlockSpec((tm, dh), lambda b, t: (t, 0))
    return pl.pallas_call(
        functools.partial(_attn_proj_body, n_hq=n_hq, n_hkv=n_hkv),
        grid=(n_batch, t_tot // tm),
        in_specs=[
            pl.BlockSpec((1, tm, d), lambda b, t: (b, t, 0)),
            pl.BlockSpec((1, N_MOD, d), _mod_row_map(n_batch, n_ctx_tiles)),
            const(w.shape), table, table, const((1, dh)), const((1, dh)),
        ],
        out_specs=[pl.BlockSpec((1, tm, n_hq * dh), lambda b, t: (b, t, 0)),
                   pl.BlockSpec((1, n_hkv, dh, tm), lambda b, t: (b, 0, 0, t)),
                   pl.BlockSpec((1, n_hkv, tm, 2 * dh), lambda b, t: (b, 0, t, 0))],
        out_shape=[jax.ShapeDtypeStruct((n_batch, t_tot, n_hq * dh), BF16),
                   jax.ShapeDtypeStruct((n_batch, n_hkv, dh, t_tot), BF16),
                   jax.ShapeDtypeStruct((n_batch, n_hkv, t_tot, 2 * dh), BF16)],
        compiler_params=_params(("parallel", "parallel")),
        name="attn_in_proj",
    )(xs, mods, w, cos, sin, q_norm, k_norm)


CONV_HALO = SUBLANES


def _l2_norm(x, scale=1.0):
    return x * (lax.rsqrt(jnp.sum(x * x, axis=-1, keepdims=True) + EPS) * scale)


def _gdn_proj_body(*refs, ctx_len, qk_w, qkv_w, n_chunk, dk, n_hv, n_src):
    x_refs, xp_refs, xn_refs = refs[:n_src], refs[n_src:2 * n_src], refs[2 * n_src:3 * n_src]
    mod_ref, w_ref, wxh_ref, wxl_ref, conv_ref, alog_ref, dtb_ref, o_ref, kt_ref, col_ref, row_ref = refs[3 * n_src:]
    tm = o_ref.shape[1]
    r0 = pl.program_id(1) * tm
    t_tot = pl.num_programs(1) * tm
    is_ctx = r0 < ctx_len
    scale = 1.0 + mod_ref[0, 1:2, :]
    shift = mod_ref[0, 0:1, :]
    h = _stream_block(x_refs, is_ctx) * scale + shift
    hb = h.astype(BF16)
    hb_ext = jnp.concatenate([_stream_block(xp_refs, is_ctx) * scale + shift, h,
                              _stream_block(xn_refs, is_ctx) * scale + shift], axis=0).astype(BF16)
    keep_lo = jnp.where(jnp.logical_or(r0 == 0, r0 == ctx_len), 0.0, 1.0)
    keep_hi = jnp.where(jnp.logical_or(r0 + tm == ctx_len, r0 + tm == t_tot), 0.0, 1.0)
    conv_w = conv_ref.shape[0]
    pad = conv_w // 2
    n_ext = tm + 2 * CONV_HALO
    mid = slice(CONV_HALO, CONV_HALO + tm)
    tn = w_ref.shape[1] // n_chunk
    for j in range(n_chunk):
        c0 = j * tn
        if c0 >= qkv_w:
            o_ref[0, :, c0:c0 + tn] = _dot(hb, w_ref[:, c0:c0 + tn]).astype(o_ref.dtype)
            continue
        p = _dot(hb_ext, w_ref[:, c0:c0 + tn])
        ext = jnp.concatenate([p[:CONV_HALO] * keep_lo, p[mid], p[CONV_HALO + tm:] * keep_hi], axis=0)
        y = None
        for tap in range(conv_w):
            off = tap - pad
            xs = ext if off == 0 else pltpu.roll(ext, (-off) % n_ext, axis=0)
            term = xs[mid] * conv_ref[tap:tap + 1, c0:c0 + tn]
            y = term if y is None else y + term
        y = _silu(y)
        if c0 >= 2 * qk_w:
            o_ref[0, :, c0:c0 + tn] = y.astype(o_ref.dtype)
            continue
        for hh in range(tn // dk):
            yh = _l2_norm(y[:, hh * dk:(hh + 1) * dk], scale=dk ** -0.5 if c0 < qk_w else 1.0)
            o_ref[0, :, c0 + hh * dk:c0 + (hh + 1) * dk] = yh.astype(o_ref.dtype)
            if c0 >= qk_w:
                for cc in range(tm // CHUNK):
                    kc = yh[cc * CHUNK:(cc + 1) * CHUNK]
                    kt_ref[0, (c0 - qk_w) // dk + hh, cc] = jnp.concatenate([kc, kc], axis=0).T.astype(kt_ref.dtype)
    h_lo = (h - hb.astype(F32)).astype(BF16)
    raw = _dot(hb, wxh_ref[...]) + (_dot(hb, wxl_ref[...]) + _dot(h_lo, wxh_ref[...]))
    _gate_forms(raw, alog_ref[...], dtb_ref[...], col_ref, row_ref, n_hv)


def _gate_forms(raw, a_log, dt_bias, col_ref, row_ref, n_hv):
    beta = _sigmoid(raw)
    xs = raw + dt_bias
    g = -jnp.exp(a_log) * (jnp.maximum(xs, 0.0) + jnp.log1p(jnp.exp(-jnp.abs(xs))))
    r = lax.broadcasted_iota(jnp.int32, (PAIR, PAIR), 0)
    c = lax.broadcasted_iota(jnp.int32, (PAIR, PAIR), 1)
    same = (r < CHUNK) == (c < CHUNK)
    tri_f = jnp.where(jnp.logical_and(same, c <= r), 1.0, 0.0).astype(BF16)
    tri_r = jnp.where(jnp.logical_and(same, c >= r), 1.0, 0.0).astype(BF16)
    lane = lax.broadcasted_iota(jnp.int32, (PAIR, LANES), 1)
    for p in range(raw.shape[0] // PAIR):
        gp = g[p * PAIR:(p + 1) * PAIR]
        g1 = gp.astype(BF16)
        r1 = gp - g1.astype(F32)
        g2 = r1.astype(BF16)
        g3 = (r1 - g2.astype(F32)).astype(BF16)
        cum_f = _dot(tri_f, g1) + (_dot(tri_f, g2) + _dot(tri_f, g3))
        cum_r = _dot(tri_r, g1) + (_dot(tri_r, g2) + _dot(tri_r, g3))
        col = jnp.where(lane < 2 * n_hv, beta[p * PAIR:(p + 1) * PAIR],
                        jnp.where(lane < 3 * n_hv, cum_f, cum_r))
        for grp in range(col_ref.shape[1]):
            shift = (LANES - 2 * HEADS_PER_STEP * grp) % LANES
            col_ref[0, grp, p * PAIR:(p + 1) * PAIR, :] = col if shift == 0 else pltpu.roll(col, shift, axis=1)
        row_ref[0, p] = col.T


def _gdn_in_proj(xs, mods, w, w_gate, conv, a_log_lanes, dt_bias_lanes, n_ctx_tiles, qk_w, n_hv, n_chunk):
    n_batch, t_tot, d = _stream_shape(xs)
    n = w.shape[1]
    conv_w, qkv_w = conv.shape
    tm = TOKEN_TILE
    tn = n // n_chunk
    assert qk_w % tn == 0 and qkv_w % tn == 0 and tn % GDN_DK == 0 and tm % PAIR == 0
    halo_per_tile = tm // CONV_HALO
    last_halo = t_tot // CONV_HALO - 1
    w_gate_hi, w_gate_lo = _split2(w_gate)
    n_hk = qk_w // GDN_DK
    const = lambda shape: pl.BlockSpec(shape, lambda b, t: (0,) * len(shape))
    tile_specs, tile_args = _stream_specs(xs, tm, lambda t: t)
    prev_specs, prev_args = _stream_specs(xs, CONV_HALO, lambda t: jnp.maximum(t * halo_per_tile - 1, 0))
    next_specs, next_args = _stream_specs(xs, CONV_HALO, lambda t: jnp.minimum((t + 1) * halo_per_tile, last_halo))
    return pl.pallas_call(
        functools.partial(_gdn_proj_body, ctx_len=n_ctx_tiles * tm, qk_w=qk_w, qkv_w=qkv_w, n_chunk=n_chunk,
                          dk=GDN_DK, n_hv=n_hv, n_src=len(tile_args)),
        grid=(n_batch, t_tot // tm),
        in_specs=[
            *tile_specs, *prev_specs, *next_specs,
            pl.BlockSpec((1, N_MOD, d), _mod_row_map(n_batch, n_ctx_tiles)),
            const((d, n)), const((d, LANES)), const((d, LANES)), const((conv_w, qkv_w)),
            const((1, LANES)), const((1, LANES)),
        ],
        out_specs=[pl.BlockSpec((1, tm, n), lambda b, t: (b, t, 0)),
                   pl.BlockSpec((1, n_hk, tm // CHUNK, GDN_DK, 2 * CHUNK), lambda b, t: (b, 0, t, 0, 0)),
                   pl.BlockSpec((1, n_hk // HEADS_PER_STEP, tm, LANES), lambda b, t: (b, 0, t, 0)),
                   pl.BlockSpec((1, tm // PAIR, LANES, PAIR), lambda b, t: (b, t, 0, 0))],
        out_shape=[jax.ShapeDtypeStruct((n_batch, t_tot, n), BF16),
                   jax.ShapeDtypeStruct((n_batch, n_hk, t_tot // CHUNK, GDN_DK, 2 * CHUNK), BF16),
                   jax.ShapeDtypeStruct((n_batch, n_hk // HEADS_PER_STEP, t_tot, LANES), F32),
                   jax.ShapeDtypeStruct((n_batch, t_tot // PAIR, LANES, PAIR), F32)],
        compiler_params=_params(("parallel", "parallel")),
        name="gdn_in_proj",
    )(*tile_args, *prev_args, *next_args, mods, w, w_gate_hi, w_gate_lo, conv, a_log_lanes, dt_bias_lanes)


STAGE_TILE = 256
HEADS_PER_STEP = 2
CHUNKS_PER_STEP = 4
SUPER = CHUNKS_PER_STEP * CHUNK


def _delta_body(q_ref, k_ref, v_ref, kt_ref, colg_ref, rowg_ref,
                y_ref, o_s, st_s, uw_s, qkd_s, *, ctx_len, n_hv, n_m):
    g_idx = pl.program_id(1)
    t_tot = q_ref.shape[1]
    dk = q_ref.shape[2] // n_m
    dv = v_ref.shape[2] // (2 * n_m)
    n_sup = t_tot // SUPER
    n_ctx_sup = ctx_len // SUPER

    st_s[...] = jnp.zeros_like(st_s)

    r = lax.broadcasted_iota(jnp.int32, (CHUNK, LANES), 0)
    lane = lax.broadcasted_iota(jnp.int32, (CHUNK, LANES), 1)
    c = lane & (CHUNK - 1)
    left = lane < CHUNK
    left_row = lax.broadcasted_iota(jnp.int32, (1, LANES), 1) < CHUNK
    keep_left = jnp.where(left, 1.0, 0.0).astype(BF16)
    keep_right = jnp.where(left, 0.0, 1.0).astype(BF16)
    eye2 = jnp.where(c == r, 1.0, 0.0)
    n_lvl = CHUNK.bit_length() - 1
    lvl_mask = [jnp.logical_and((r >> (lg + 1)) == (c >> (lg + 1)), (r >> lg) != (c >> lg)) for lg in range(n_lvl)]
    incl = (c <= r, c >= r)
    strict = (c < r, c > r)
    zero_b = jnp.zeros((CHUNK, dv), BF16)

    def block_diag(y):
        return jnp.concatenate([y * keep_left, y * keep_right], axis=0)

    def reverse_super(i):
        return jnp.where(i < n_ctx_sup, n_ctx_sup - 1 - i, n_sup + n_ctx_sup - 1 - i)

    def make_units(i):
        sup = (i, reverse_super(i))
        units = []
        for m in range(n_m):
            for d in (0, 1):
                for seq in range(CHUNKS_PER_STEP):
                    cidx = seq if d == 0 else CHUNKS_PER_STEP - 1 - seq
                    units.append(dict(
                        m=m, d=d, seq=seq, half=cidx % 2, uid=(m * 2 + d) * CHUNKS_PER_STEP + seq,
                        pair=sup[d] * (SUPER // PAIR) + cidx // 2, chunk=sup[d] * CHUNKS_PER_STEP + cidx,
                        rows=pl.ds(pl.multiple_of(sup[d] * SUPER + cidx * CHUNK, CHUNK), CHUNK)))
        return units

    def load_gates(u):
        m, d, half = u["m"], u["d"], u["half"]
        cg = colg_ref[0, 0, u["rows"], :]
        b0, g0 = d * n_hv + 2 * m, (2 + d) * n_hv + 2 * m
        u["beta_c"] = (cg[:, b0:b0 + 1], cg[:, b0 + 1:b0 + 2])
        u["gc_c"] = (cg[:, g0:g0 + 1], cg[:, g0 + 1:g0 + 2])
        head0 = 2 * (n_m * g_idx + m)

        def packed_rows(base):
            rows = rowg_ref[0, u["pair"], pl.ds(base + head0, 2), :]
            a, b = rows[0:1], rows[1:2]
            if half == 0:
                return jnp.where(left_row, a, pltpu.roll(b, CHUNK, axis=1))
            return jnp.where(left_row, pltpu.roll(a, CHUNK, axis=1), b)

        u["beta_r"] = packed_rows(d * n_hv)
        u["gc_r"] = packed_rows((2 + d) * n_hv)

    def state_free_part(i, slot):
        units = make_units(i)
        for u in units:
            m = u["m"]
            load_gates(u)
            qb, kb = q_ref[0, u["rows"], m * dk:(m + 1) * dk], k_ref[0, u["rows"], m * dk:(m + 1) * dk]
            u["kb"] = kb
            u["gram"] = lax.dot_general(jnp.concatenate([kb, qb], axis=0), jnp.concatenate([kb, kb], axis=0),
                                        (((1,), (1,)), ((), ())), preferred_element_type=F32)
        yield
        for u in units:
            d = u["d"]
            gc_c2 = jnp.where(left, u["gc_c"][0], u["gc_c"][1])
            beta_c2 = jnp.where(left, u["beta_c"][0], u["beta_c"][1])
            decay = jnp.where(incl[d], jnp.exp(jnp.where(incl[d], gc_c2 - u["gc_r"], 0.0)), 0.0)
            a = jnp.where(strict[d], u["gram"][0:CHUNK] * decay, 0.0) * beta_c2
            qkd_s[slot, u["uid"]] = (u["gram"][CHUNK:2 * CHUNK] * decay).astype(BF16)
            u["a"] = a
            u["x"] = eye2 - jnp.where(lvl_mask[0], a, 0.0)
        yield
        for lg in range(1, n_lvl):
            for u in units:
                u["xb"] = u["x"].astype(BF16)
                off = jnp.where(lvl_mask[lg], u["a"], 0.0).astype(BF16)
                u["z"] = _dot(off, block_diag(u["xb"]))
            yield
            for u in units:
                u["x"] = u["x"] - _dot(u["xb"], block_diag(u["z"].astype(BF16)))
            yield
        for u in units:
            m = u["m"]
            tb = u["x"] * u["beta_r"]
            tbe = tb * jnp.exp(u["gc_r"])
            v2 = v_ref[0, u["rows"], 2 * m * dv:2 * (m + 1) * dv]
            v_bd = jnp.concatenate([jnp.concatenate([v2[:, :dv], zero_b], axis=1),
                                    jnp.concatenate([zero_b, v2[:, dv:]], axis=1)], axis=0)
            k_bd = jnp.concatenate([jnp.concatenate([u["kb"], zero_b], axis=1),
                                    jnp.concatenate([zero_b, u["kb"]], axis=1)], axis=0)
            uw_s[slot, u["uid"], :, :2 * dv] = _dot(tb.astype(BF16), v_bd)
            uw_s[slot, u["uid"], :, 2 * dv:] = _dot(tbe.astype(BF16), k_bd)
        yield

    def recurrence_part(i, slot):
        units = make_units(i)
        for seq in range(CHUNKS_PER_STEP):
            now = [u for u in units if u["seq"] == seq]
            for u in now:
                m = u["m"]
                load_gates(u)
                uw = uw_s[slot, u["uid"]]
                u["u"] = uw[:, :2 * dv]
                u["s2"] = st_s[2 * m + u["d"]]
                lhs = jnp.concatenate([uw[:, 2 * dv:3 * dv].astype(BF16), uw[:, 3 * dv:].astype(BF16),
                                       q_ref[0, u["rows"], m * dk:(m + 1) * dk]], axis=0)
                u["ws"] = _dot(lhs, u["s2"].astype(BF16))
            yield
            for u in now:
                d, ws = u["d"], u["ws"]
                vn0 = (u["u"][:, :dv] - ws[0:CHUNK, :dv]).astype(BF16)
                vn1 = (u["u"][:, dv:] - ws[CHUNK:2 * CHUNK, dv:]).astype(BF16)
                vn_bd = jnp.concatenate([jnp.concatenate([vn0, zero_b], axis=1),
                                         jnp.concatenate([zero_b, vn1], axis=1)], axis=0)
                last = (CHUNK - 1, 2 * CHUNK - 1) if d == 0 else (0, CHUNK)
                u["gl"] = [u["gc_r"][:, l:l + 1] for l in last]
                gl2 = jnp.where(left_row, u["gl"][0], u["gl"][1])
                kdt = kt_ref[0, u["m"], u["chunk"]].astype(F32) * jnp.exp(gl2 - u["gc_r"])
                lhs = jnp.concatenate([qkd_s[slot, u["uid"]], kdt.astype(BF16)], axis=0)
                u["ov"] = _dot(lhs, vn_bd)
            yield
            for u in now:
                m, ws, ov = u["m"], u["ws"], u["ov"]
                qs = jnp.concatenate([ws[2 * CHUNK:, :dv] * jnp.exp(u["gc_c"][0]),
                                      ws[2 * CHUNK:, dv:] * jnp.exp(u["gc_c"][1])], axis=1)
                o_s[u["d"], u["rows"], 2 * m * dv:2 * (m + 1) * dv] = qs + ov[0:CHUNK]
                egl = jnp.concatenate([jnp.broadcast_to(jnp.exp(u["gl"][0]), (1, dv)),
                                       jnp.broadcast_to(jnp.exp(u["gl"][1]), (1, dv))], axis=1)
                st_s[2 * m + u["d"]] = u["s2"] * egl + ov[CHUNK:]
            yield

    def run_interleaved(parts):
        live = list(parts)
        while live:
            live = [g for g in live if next(g, "done") != "done"]

    def step(i, carry):
        slot = i & 1
        run_interleaved([state_free_part(i, slot), recurrence_part(i - 1, 1 - slot)])
        return carry

    run_interleaved([state_free_part(jnp.int32(0), 0)])
    lax.fori_loop(1, n_sup, step, 0)
    run_interleaved([recurrence_part(jnp.int32(n_sup - 1), (n_sup - 1) & 1)])

    def finish(i, carry):
        rows = pl.ds(pl.multiple_of(i * STAGE_TILE, STAGE_TILE), STAGE_TILE)
        y_ref[0, rows, :] = (o_s[0, rows, :] + o_s[1, rows, :]).astype(y_ref.dtype)
        return carry

    lax.fori_loop(0, t_tot // STAGE_TILE, finish, 0)


def _gdn_delta(p_main, k_t, colg, rowg, dv, ctx_len, n_hk, n_hv):
    n_batch, t_tot, _ = p_main.shape
    dk = GDN_DK
    assert n_hv == 2 * n_hk and dk == dv == LANES and n_hk % HEADS_PER_STEP == 0
    assert ctx_len % SUPER == 0 and t_tot % SUPER == 0 and SUPER % STAGE_TILE == 0
    n_m = HEADS_PER_STEP
    qb = n_m * dk
    vb = 2 * n_m * dv
    k_blk0 = n_hk // n_m
    v_blk0 = (2 * n_hk * dk) // vb
    n_pairs = t_tot // PAIR
    n_units = 2 * n_m * CHUNKS_PER_STEP
    return pl.pallas_call(
        functools.partial(_delta_body, ctx_len=ctx_len, n_hv=n_hv, n_m=n_m),
        grid=(n_batch, n_hk // n_m),
        in_specs=[
            pl.BlockSpec((1, t_tot, qb), lambda b, h: (b, 0, h)),
            pl.BlockSpec((1, t_tot, qb), lambda b, h: (b, 0, k_blk0 + h)),
            pl.BlockSpec((1, t_tot, vb), lambda b, h: (b, 0, v_blk0 + h)),
            pl.BlockSpec((1, n_m, t_tot // CHUNK, dk, 2 * CHUNK), lambda b, h: (b, h, 0, 0, 0)),
            pl.BlockSpec((1, 1, t_tot, LANES), lambda b, h: (b, h, 0, 0)),
            pl.BlockSpec((1, n_pairs, LANES, PAIR), lambda b, h: (b, 0, 0, 0)),
        ],
        out_specs=pl.BlockSpec((1, t_tot, vb), lambda b, h: (b, 0, h)),
        out_shape=jax.ShapeDtypeStruct((n_batch, t_tot, n_hv * dv), BF16),
        scratch_shapes=[
            pltpu.VMEM((2, t_tot, vb), F32),
            pltpu.VMEM((2 * n_m, dk, 2 * dv), F32),
            pltpu.VMEM((2, n_units, CHUNK, 2 * dv + 2 * dk), F32),
            pltpu.VMEM((2, n_units, CHUNK, 2 * CHUNK), BF16),
        ],
        compiler_params=_params(("parallel", "parallel")),
        name="gdn_delta",
    )(p_main, p_main, p_main, k_t, colg, rowg)


def _attn_body(q_ref, kt_ref, v1_ref, o_ref, *, ctx_len, n_group, skip_tiles):
    tq = q_ref.shape[1]
    dh = kt_ref.shape[2]
    t_tot = kt_ref.shape[3]
    r0 = (pl.program_id(2) + skip_tiles) * tq

    def attend(n_keys):
        def scores(g):
            return _dot(q_ref[0, :, g * dh:(g + 1) * dh], kt_ref[0, 0, :, :n_keys])

        s_next = scores(0)
        for g in range(n_group):
            s = s_next
            if g + 1 < n_group:
                s_next = scores(g + 1)
            p = jnp.exp2(s - jnp.max(s, axis=-1, keepdims=True))
            num_den = _dot(p.astype(BF16), v1_ref[0, 0, :n_keys, :])
            o_ref[0, :, g * dh:(g + 1) * dh] = (num_den[:, :dh] / num_den[:, dh:]).astype(o_ref.dtype)

    @pl.when(r0 < ctx_len)
    def _():
        attend(ctx_len)

    @pl.when(r0 >= ctx_len)
    def _():
        attend(t_tot)


def _attention(q, k_t, v1, ctx_len, skip_tiles=0):
    n_batch, t_tot, q_w = q.shape
    _, n_hkv, dh, _ = k_t.shape
    n_group = q_w // (n_hkv * dh)
    tq = TOKEN_TILE
    return pl.pallas_call(
        functools.partial(_attn_body, ctx_len=ctx_len, n_group=n_group, skip_tiles=skip_tiles),
        grid=(n_batch, n_hkv, t_tot // tq - skip_tiles),
        in_specs=[
            pl.BlockSpec((1, tq, n_group * dh), lambda b, h, i: (b, i + skip_tiles, h)),
            pl.BlockSpec((1, 1, dh, t_tot), lambda b, h, i: (b, h, 0, 0)),
            pl.BlockSpec((1, 1, t_tot, 2 * dh), lambda b, h, i: (b, h, 0, 0)),
        ],
        out_specs=pl.BlockSpec((1, tq, n_group * dh), lambda b, h, i: (b, i, h)),
        out_shape=jax.ShapeDtypeStruct((n_batch, t_tot - skip_tiles * tq, q_w), BF16),
        compiler_params=_params(("parallel", "parallel", "parallel")),
        name="attention",
    )(q, k_t, v1)


def _rope_tables(ctx_len, seq_len):
    rows = seq_len // GRID_W
    row = jnp.repeat(jnp.arange(rows), GRID_W).astype(F32)
    col = jnp.tile(jnp.arange(GRID_W), rows).astype(F32)
    n_freq = ATT_DH // 4
    freqs = ROPE_THETA ** (-jnp.arange(n_freq, dtype=F32) / n_freq)
    ang_r = row[:, None] * freqs
    ang_c = col[:, None] * freqs
    ang = jnp.concatenate([ang_r, ang_c, ang_r, ang_c], axis=-1)
    cos = jnp.concatenate([jnp.ones((ctx_len, ATT_DH), F32), jnp.cos(ang)], axis=0)
    sin = jnp.concatenate([jnp.zeros((ctx_len, ATT_DH), F32), jnp.sin(ang)], axis=0)
    return cos, jnp.where(jnp.arange(ATT_DH) < ATT_DH // 2, -sin, sin)


ROW_SPLIT = 2

def _mix_ffn_body(y_ref, *refs, n_chunk, alpha, gated, n_src, ctx_tiles_left):
    if gated:
        z_ref, ng_ref, wo_ref = refs[:3]
        x_refs, (mod_ref, ln_ref, win_ref, wout_ref, o_ref) = refs[3:3 + n_src], refs[3 + n_src:]
        dv = ng_ref.shape[1]
        kw = 2 * dv
        delta = None
        for pp in range(y_ref.shape[2] // kw):
            pair = []
            for hh in (2 * pp, 2 * pp + 1):
                o = y_ref[0, :, hh * dv:(hh + 1) * dv].astype(F32)
                z = z_ref[0, :, hh * dv:(hh + 1) * dv].astype(F32)
                pair.append((_rms_norm(o, ng_ref[...]) * _silu(z)).astype(BF16))
            part = _dot(jnp.concatenate(pair, axis=1), wo_ref[pp * kw:(pp + 1) * kw, :])
            delta = part if delta is None else delta + part
    else:
        wo_ref = refs[0]
        x_refs, (mod_ref, ln_ref, win_ref, wout_ref, o_ref) = refs[1:1 + n_src], refs[1 + n_src:]
        delta = None
    rows = o_ref.shape[1] // ROW_SPLIT
    parts = [slice(i * rows, (i + 1) * rows) for i in range(ROW_SPLIT)]
    deltas = [_dot(y_ref[0, r, :], wo_ref[...]) if delta is None else delta[r] for r in parts]
    x_in = _stream_block(x_refs, pl.program_id(1) < ctx_tiles_left)
    x1s = [_layer_norm(alpha * x_in[r] + mod_ref[0, 2:3, :] * dl, ln_ref[0:1, :], ln_ref[1:2, :])
           for r, dl in zip(parts, deltas)]
    hbs = [(x1 * (1.0 + mod_ref[0, 4:5, :]) + mod_ref[0, 3:4, :]).astype(BF16) for x1 in x1s]
    d_ff = wout_ref.shape[0]
    tc = d_ff // n_chunk
    accs = [None] * ROW_SPLIT
    for c in range(n_chunk):
        for i in range(ROW_SPLIT):
            gate = _dot(hbs[i], win_ref[:, c * tc:(c + 1) * tc])
            up = _dot(hbs[i], win_ref[:, d_ff + c * tc:d_ff + (c + 1) * tc])
            part = _dot((_silu(gate) * up).astype(BF16), wout_ref[c * tc:(c + 1) * tc, :])
            accs[i] = part if accs[i] is None else accs[i] + part
    for i, r in enumerate(parts):
        o_ref[0, r, :] = _layer_norm(alpha * x1s[i] + mod_ref[0, 5:6, :] * accs[i], ln_ref[2:3, :], ln_ref[3:4, :])


def _mix_ffn(y, w_o, xs, mods, ln, w_in, w_out, n_ctx_tiles, alpha, skip_tiles=0, gate=None):
    n_batch, t_tot, d = _stream_shape(xs)
    k = y.shape[-1]
    d_ff = w_out.shape[0]
    tm = TOKEN_TILE
    n_chunk = 2 if d_ff % (2 * LANES) == 0 else 1
    row_map = _mod_row_map(n_batch, n_ctx_tiles)
    y_skip = skip_tiles - (t_tot - y.shape[1]) // tm
    x_specs, x_args = _stream_specs(xs, tm, lambda t: t + skip_tiles)
    resident = lambda shape: pl.BlockSpec(shape, lambda b, t: (0,) * len(shape), pipeline_mode=pl.Buffered(1))
    gate_specs, gate_args = [], []
    if gate is not None:
        z_src, z_blk, norm_g = gate
        gate_specs = [pl.BlockSpec((1, tm, k), lambda b, t: (b, t + skip_tiles, z_blk)), resident(norm_g.shape)]
        gate_args = [z_src, norm_g]
    return pl.pallas_call(
        functools.partial(_mix_ffn_body, n_chunk=n_chunk, alpha=alpha, gated=gate is not None,
                          n_src=len(x_args), ctx_tiles_left=n_ctx_tiles - skip_tiles),
        grid=(n_batch, t_tot // tm - skip_tiles),
        in_specs=[
            pl.BlockSpec((1, tm, k), lambda b, t: (b, t + y_skip, 0)),
            *gate_specs,
            resident((k, d)),
            *x_specs,
            pl.BlockSpec((1, N_MOD, d), lambda b, t: row_map(b, t + skip_tiles)),
            resident(ln.shape),
            resident((d, 2 * d_ff)),
            resident((d_ff, d)),
        ],
        out_specs=pl.BlockSpec((1, tm, d), lambda b, t: (b, t, 0)),
        out_shape=jax.ShapeDtypeStruct((n_batch, t_tot - skip_tiles * tm, d), F32),
        compiler_params=_params(("parallel", "parallel")),
        name="mix_ffn",
    )(y, *gate_args, w_o, *x_args, mods, ln, w_in, w_out)


def kernel(x, c, ctx, c_ctx, w_mod, b_mod, ln_g, ln_b, w_ffn_in, w_ffn_out, gdn_w_in, gdn_conv, gdn_a_log,
           gdn_dt_bias, gdn_norm_g, gdn_w_out, attn_w_qkv, attn_q_norm, attn_k_norm, attn_w_out):
    n_batch, seq_len, d = x.shape
    ctx_len = ctx.shape[1]
    depth = w_mod.shape[0]
    alpha = (2 * depth) ** 0.25
    n_hv = gdn_a_log.shape[-1]
    dv = gdn_norm_g.shape[-1]
    v_w = n_hv * dv
    qkv_w = gdn_conv.shape[-1]
    n_hk = (qkv_w - v_w) // (2 * GDN_DK)
    n_hq = attn_w_out.shape[1] // ATT_DH
    n_hkv = (attn_w_qkv.shape[-1] // ATT_DH - n_hq) // 2
    assert ctx_len % TOKEN_TILE == 0 and seq_len % TOKEN_TILE == 0 and seq_len % GRID_W == 0
    assert 4 * n_hv <= LANES and d % LANES == 0
    n_ctx_tiles = ctx_len // TOKEN_TILE

    xs = (ctx, x)

    mp = -(-(n_batch + 1) // SUBLANES) * SUBLANES
    cond = jnp.concatenate([c, c_ctx[None, :], jnp.zeros((mp - n_batch - 1, d), F32)], axis=0)
    mods = _modulation(cond, w_mod, b_mod).reshape(depth, mp, N_MOD, d)
    tables = _rope_tables(ctx_len, seq_len)

    for i in range(depth):
        j = i // 2
        m = mods[i]
        skip = n_ctx_tiles if i == depth - 1 else 0
        if i % 2 == 0:
            w_in = gdn_w_in[j]
            gate_w = jnp.pad(w_in[:, qkv_w + v_w:], ((0, 0), (0, LANES - 4 * n_hv)))
            lanes_pad = (0, LANES - 4 * n_hv)
            a_log = jnp.pad(jnp.concatenate([jnp.zeros((2 * n_hv,), F32), gdn_a_log[j].reshape(-1)]), lanes_pad)
            dt_b = jnp.pad(jnp.concatenate([jnp.zeros((2 * n_hv,), F32), gdn_dt_bias[j].reshape(-1)]), lanes_pad)
            p_main, k_t, colg, rowg = _gdn_in_proj(xs, m, w_in[:, :qkv_w + v_w].astype(BF16), gate_w, gdn_conv[j],
                                                   a_log[None, :], dt_b[None, :], n_ctx_tiles,
                                                   qk_w=n_hk * GDN_DK, n_hv=n_hv, n_chunk=6)
            y = _gdn_delta(p_main, k_t, colg, rowg, dv, ctx_len, n_hk, n_hv)
            w_o = gdn_w_out[j]
            assert qkv_w % v_w == 0
            gate = (p_main, qkv_w // v_w, gdn_norm_g[j][None, :])
        else:
            q, k_t, v1 = _attn_in_proj(xs, m, attn_w_qkv[j].astype(BF16), tables, attn_q_norm[j][None, :],
                                       attn_k_norm[j][None, :], n_ctx_tiles, n_hq, n_hkv)
            y = _attention(q, k_t, v1, ctx_len, skip_tiles=skip)
            w_o = attn_w_out[j]
            gate = None
        ln = jnp.stack([ln_g[i, 0], ln_b[i, 0], ln_g[i, 1], ln_b[i, 1]])
        xs = _mix_ffn(y, w_o.astype(BF16), xs, m, ln, w_ffn_in[i].astype(BF16), w_ffn_out[i].astype(BF16),
                      n_ctx_tiles, alpha, skip_tiles=skip, gate=gate)
    return xs
```
